```python
import math
import jax, jax.numpy as jnp
from jax import lax
import numpy as np

D_MODEL = 2048
BATCH = 1
SEQ = 8192
DEPTH = 1

N_ATTN_HEADS = 8
ATTN_HEAD_DIM = 128
ATTN_WIDTH = N_ATTN_HEADS * ATTN_HEAD_DIM
N_IDX_HEADS = 16
IDX_HEAD_DIM = 64
TOPK_MAX = 256
Q_BLOCK = 128

GMLP_CHUNK = 128
GMLP_GROUPS = 8
GMLP_GROUP_DIM = 128
GMLP_WIDTH = GMLP_GROUPS * GMLP_GROUP_DIM

N_REL_BUCKETS = 32
REL_MAX_DISTANCE = 128

PEER_HEADS = 8
PEER_N_KEYS = 128
PEER_N_EXPERTS = PEER_N_KEYS * PEER_N_KEYS
PEER_QUERY_DIM = 256
PEER_HALF = PEER_QUERY_DIM // 2
PEER_TOPK = 16
PEER_TOKEN_BLOCK = 128

PLE_DIM = 256

LN_EPS = 1e-5
DEEPNORM_ALPHA = (2.0 * DEPTH) ** 0.25
DEEPNORM_BETA = (8.0 * DEPTH) ** -0.25

IN_SPLITS = (ATTN_WIDTH, ATTN_WIDTH, ATTN_WIDTH, N_IDX_HEADS * IDX_HEAD_DIM, IDX_HEAD_DIM,
             N_IDX_HEADS, GMLP_WIDTH, GMLP_WIDTH, D_MODEL, D_MODEL)
IN_WIDTH = sum(IN_SPLITS)
IN_OFFSETS = tuple(int(o) for o in np.cumsum(IN_SPLITS)[:-1])

kernel_name = 'hybrid_dsa_gmlp_peer_deepnorm'


def layer_norm(x, g, b):
    xf = x.astype(jnp.float32)
    mu = jnp.mean(xf, axis=-1, keepdims=True)
    var = jnp.mean(jnp.square(xf - mu), axis=-1, keepdims=True)
    y = (xf - mu) * lax.rsqrt(var + LN_EPS)
    return (y * g.astype(jnp.float32) + b.astype(jnp.float32)).astype(x.dtype)


def rel_bucket(dist):
    n = jnp.maximum(dist, 0)
    max_exact = N_REL_BUCKETS // 2
    nf = jnp.maximum(n, 1).astype(jnp.float32)
    large = max_exact + (jnp.log(nf / max_exact) / math.log(REL_MAX_DISTANCE / max_exact)
                         * (N_REL_BUCKETS - max_exact)).astype(jnp.int32)
    large = jnp.minimum(large, N_REL_BUCKETS - 1)
    return jnp.where(n < max_exact, n, large)


def dsa_attention(q, k, v, q_idx, k_idx, w_idx, positions, rel_bias):
    B, S = positions.shape
    top_k = min(TOPK_MAX, S // 4)
    n_blk = S // Q_BLOCK
    f32 = jnp.float32

    def to_blocks(a):
        return jnp.moveaxis(a.reshape((B, n_blk, Q_BLOCK) + a.shape[2:]), 1, 0)

    k_idx_f = k_idx.astype(f32)
    gather = jax.vmap(lambda a, i: a[i])

    def block(args):
        qb, qib, wb, pb = args
        dots = jnp.einsum('bqhd,bsd->bqhs', qib.astype(f32), k_idx_f) * (IDX_HEAD_DIM ** -0.5)
        idx_score = jnp.einsum('bqh,bqhs->bqs', wb.astype(f32) * (N_IDX_HEADS ** -0.5),
                               jax.nn.relu(dots))
        causal = positions[:, None, :] <= pb[:, :, None]
        idx_score = jnp.where(causal, idx_score, -jnp.inf)
        _, sel = lax.top_k(idx_score, top_k)
        kg = gather(k, sel)
        vg = gather(v, sel)
        kpos = gather(positions, sel)
        dist = pb[:, :, None] - kpos
        bias = jnp.moveaxis(rel_bias[rel_bucket(dist)], -1, 2)
        logits = (jnp.einsum('bqhd,bqkhd->bqhk', qb, kg).astype(f32) * (ATTN_HEAD_DIM ** -0.5)
                  + bias.astype(f32))
        logits = jnp.where((dist >= 0)[:, :, None, :], logits, -jnp.inf)
        probs = jax.nn.softmax(logits, axis=-1).astype(v.dtype)
        return jnp.einsum('bqhk,bqkhd->bqhd', probs, vg)

    out = lax.map(block, (to_blocks(q), to_blocks(q_idx), to_blocks(w_idx), to_blocks(positions)))
    return jnp.moveaxis(out, 0, 1).reshape(B, S, ATTN_WIDTH)


def chunked_sgu(u, v, w_s, b_s, ln_g, ln_b):
    B, S, _ = u.shape
    u = jax.nn.gelu(u)
    v = layer_norm(jax.nn.gelu(v), ln_g, ln_b)
    n_chunk = S // GMLP_CHUNK
    vc = v.reshape(B, n_chunk, GMLP_CHUNK, GMLP_GROUPS, GMLP_GROUP_DIM)
    mask = jnp.tril(jnp.ones((GMLP_CHUNK, GMLP_CHUNK), dtype=bool))
    w = jnp.where(mask[None], w_s, jnp.zeros_like(w_s))
    mixed = jnp.einsum('gts,bnsgc->bntgc', w, vc) + b_s.T[None, None, :, :, None]
    return u * mixed.reshape(B, S, GMLP_WIDTH)


def peer_ffn(x, w_q, sub_keys, expert_u, expert_v):
    B, S, D = x.shape
    xt = x.reshape(B * S, D)
    q = (xt @ w_q).reshape(B * S, PEER_HEADS, 2, PEER_HALF).astype(jnp.float32)
    scores = jnp.einsum('nhpd,hpkd->nhpk', q, sub_keys.astype(jnp.float32))
    half_s, half_i = lax.top_k(scores, PEER_TOPK)
    cand = half_s[:, :, 0, :, None] + half_s[:, :, 1, None, :]
    cand = cand.reshape(B * S, PEER_HEADS, PEER_TOPK * PEER_TOPK)
    best_s, best_c = lax.top_k(cand, PEER_TOPK)
    i1 = jnp.take_along_axis(half_i[:, :, 0], best_c // PEER_TOPK, axis=-1)
    i2 = jnp.take_along_axis(half_i[:, :, 1], best_c % PEER_TOPK, axis=-1)
    experts = i1 * PEER_N_KEYS + i2
    gates = jax.nn.softmax(best_s, axis=-1).astype(x.dtype)
    n_blk = (B * S) // PEER_TOKEN_BLOCK

    def block(args):
        xb, eb, gb = args
        act = jax.nn.gelu(jnp.einsum('td,thkd->thk', xb, expert_u[eb]))
        return jnp.einsum('thk,thkd->td', gb * act, expert_v[eb])

    out = lax.map(block, (xt.reshape(n_blk, PEER_TOKEN_BLOCK, D),
                          experts.reshape(n_blk, PEER_TOKEN_BLOCK, PEER_HEADS, PEER_TOPK),
                          gates.reshape(n_blk, PEER_TOKEN_BLOCK, PEER_HEADS, PEER_TOPK)))
    return out.reshape(B, S, D)


def setup_inputs(seed: int = 0) -> dict:
    key = jax.random.key(seed)
    ks = jax.random.split(key, 32)
    f32 = jnp.float32
    nrm = lambda k, shape, s: jax.random.normal(k, shape, f32) * s
    L, D = DEPTH, D_MODEL
    return {
        'x': nrm(ks[0], (BATCH, SEQ, D), 1.0),
        'p': nrm(ks[1], (DEPTH, BATCH, SEQ, PLE_DIM), 1.0),
        'positions': jnp.broadcast_to(jnp.arange(SEQ, dtype=jnp.int32), (BATCH, SEQ)),
        'ln_emb_g': 1.0 + nrm(ks[2], (D,), 0.01),
        'ln_emb_b': nrm(ks[3], (D,), 0.01),
        'rel_bias': nrm(ks[4], (N_REL_BUCKETS, N_ATTN_HEADS), 0.5),
        'w_in': nrm(ks[5], (L, D, IN_WIDTH), D ** -0.5),
        'gmlp_ln_g': 1.0 + nrm(ks[6], (L, GMLP_WIDTH), 0.01),
        'gmlp_ln_b': nrm(ks[7], (L, GMLP_WIDTH), 0.01),
        'gmlp_w_s': nrm(ks[8], (L, GMLP_GROUPS, GMLP_CHUNK, GMLP_CHUNK), 0.5 * GMLP_CHUNK ** -0.5),
        'gmlp_b_s': 1.0 + nrm(ks[9], (L, GMLP_GROUPS, GMLP_CHUNK), 0.01),
        'w_br_attn': nrm(ks[10], (L, ATTN_WIDTH, D), DEEPNORM_BETA * ATTN_WIDTH ** -0.5),
        'w_br_gmlp': nrm(ks[11], (L, GMLP_WIDTH, D), DEEPNORM_BETA * GMLP_WIDTH ** -0.5),
        'w_mix_out': nrm(ks[12], (L, D, D), DEEPNORM_BETA * D ** -0.5),
        'ln1_g': 1.0 + nrm(ks[13], (L, D), 0.01),
        'ln1_b': nrm(ks[14], (L, D), 0.01),
        'peer_w_q': nrm(ks[15], (L, D, PEER_HEADS * PEER_QUERY_DIM), D ** -0.5),
        'peer_sub_keys': nrm(ks[16], (L, PEER_HEADS, 2, PEER_N_KEYS, PEER_HALF), PEER_HALF ** -0.5),
        'peer_u': nrm(ks[17], (L, PEER_N_EXPERTS, D), D ** -0.5),
        'peer_v': nrm(ks[18], (L, PEER_N_EXPERTS, D), DEEPNORM_BETA * PEER_HEADS ** -0.5),
        'ple_w_proj': nrm(ks[19], (L, PLE_DIM, D), DEEPNORM_BETA * PLE_DIM ** -0.5),
        'ple_w_gate': nrm(ks[20], (L, D, D), D ** -0.5),
        'ln2_g': 1.0 + nrm(ks[21], (L, D), 0.01),
        'ln2_b': nrm(ks[22], (L, D), 0.01),
    }


def reference(x, p, positions, ln_emb_g, ln_emb_b, rel_bias, w_in, gmlp_ln_g, gmlp_ln_b,
              gmlp_w_s, gmlp_b_s, w_br_attn, w_br_gmlp, w_mix_out, ln1_g, ln1_b, peer_w_q,
              peer_sub_keys, peer_u, peer_v, ple_w_proj, ple_w_gate, ln2_g, ln2_b):
    B, S, _ = x.shape
    h = layer_norm(x, ln_emb_g, ln_emb_b)
    for i in range(DEPTH):
        z = h @ w_in[i]
        q, k, v, qi, ki, wi, gu, gv, ga, gg = jnp.split(z, IN_OFFSETS, axis=-1)
        attn = dsa_attention(q.reshape(B, S, N_ATTN_HEADS, ATTN_HEAD_DIM),
                             k.reshape(B, S, N_ATTN_HEADS, ATTN_HEAD_DIM),
                             v.reshape(B, S, N_ATTN_HEADS, ATTN_HEAD_DIM),
                             qi.reshape(B, S, N_IDX_HEADS, IDX_HEAD_DIM),
                             ki, wi, positions, rel_bias)
        gm = chunked_sgu(gu, gv, gmlp_w_s[i], gmlp_b_s[i], gmlp_ln_g[i], gmlp_ln_b[i])
        merged = (jax.nn.sigmoid(ga) * (attn @ w_br_attn[i])
                  + jax.nn.sigmoid(gg) * (gm @ w_br_gmlp[i]))
        h = layer_norm(DEEPNORM_ALPHA * h + merged @ w_mix_out[i], ln1_g[i], ln1_b[i])
        r = DEEPNORM_ALPHA * h + peer_ffn(h, peer_w_q[i], peer_sub_keys[i], peer_u[i], peer_v[i])
        ple = jax.nn.sigmoid(r @ ple_w_gate[i]) * (p[i] @ ple_w_proj[i])
        h = layer_norm(r + ple, ln2_g[i], ln2_b[i])
    return h
```

```python
import functools
import math

import jax
import jax.numpy as jnp
import numpy as np
from jax import lax
from jax.experimental import pallas as pl
from jax.experimental.pallas import tpu as pltpu

F32 = jnp.float32
BF16 = jnp.bfloat16
I32 = jnp.int32

LN_EPS = 1e-5
DEPTH = 1
DEEPNORM_ALPHA = (2.0 * DEPTH) ** 0.25

N_ATTN_HEADS = 8
ATTN_HEAD_DIM = 128
ATTN_WIDTH = N_ATTN_HEADS * ATTN_HEAD_DIM
N_IDX_HEADS = 16
IDX_HEAD_DIM = 64
TOPK_MAX = 256
GMLP_CHUNK = 128
GMLP_GROUPS = 8
GMLP_WIDTH = 1024
N_REL_BUCKETS = 32
REL_MAX_DISTANCE = 128
PEER_HEADS = 8
PEER_N_KEYS = 128
PEER_TOPK = 16

LANES = 128
NEG = -1e30
INT_MIN = -(2 ** 31)
NT_DIMS = (((1,), (1,)), ((), ()))
TN_DIMS = (((0,), (0,)), ((), ()))
VMEM_LIMIT = 56 * 1024 * 1024


def _params(*sem):
    return pltpu.CompilerParams(dimension_semantics=sem, vmem_limit_bytes=VMEM_LIMIT)


def _resident(shape, index_map):
    return pl.BlockSpec(shape, index_map, pipeline_mode=pl.Buffered(1))


def _gelu(x):
    return 0.5 * x * (1.0 + jnp.tanh(0.7978845608028654 * (x + 0.044715 * (x * x * x))))


def _layer_norm(x, g, b):
    mu = jnp.mean(x, axis=-1, keepdims=True)
    xc = x - mu
    var = jnp.mean(xc * xc, axis=-1, keepdims=True)
    return xc * lax.rsqrt(var + LN_EPS) * g + b


def _ln_kernel(x_ref, g_ref, b_ref, h_ref, hb_ref):
    y = _layer_norm(x_ref[...], g_ref[...], b_ref[...])
    h_ref[...] = y
    hb_ref[...] = y.astype(BF16)


def _ln_embed(x, g, b, tm):
    S, D = x.shape
    return pl.pallas_call(
        _ln_kernel,
        grid=(S // tm,),
        in_specs=[pl.BlockSpec((tm, D), lambda i: (i, 0)),
                  pl.BlockSpec((1, D), lambda i: (0, 0)),
                  pl.BlockSpec((1, D), lambda i: (0, 0))],
        out_specs=[pl.BlockSpec((tm, D), lambda i: (i, 0)),
                   pl.BlockSpec((tm, D), lambda i: (i, 0))],
        out_shape=[jax.ShapeDtypeStruct((S, D), F32), jax.ShapeDtypeStruct((S, D), BF16)],
        compiler_params=_params("parallel"),
        name="ln_embed",
    )(x, g.reshape(1, D), b.reshape(1, D))


def _mm_kernel(a_ref, w_ref, o_ref):
    o_ref[...] = jnp.dot(a_ref[...], w_ref[...], preferred_element_type=F32).astype(o_ref.dtype)


def _matmul(a, w, out_dtype, tm, tn, name):
    M, K = a.shape
    N = w.shape[1]
    return pl.pallas_call(
        _mm_kernel,
        grid=(M // tm, N // tn),
        in_specs=[pl.BlockSpec((tm, K), lambda i, j: (i, 0)),
                  pl.BlockSpec((K, tn), lambda i, j: (0, j))],
        out_specs=pl.BlockSpec((tm, tn), lambda i, j: (i, j)),
        out_shape=jax.ShapeDtypeStruct((M, N), out_dtype),
        compiler_params=_params("parallel", "arbitrary"),
        name=name,
    )(a, w)


def _sortable_key(x):
    b = lax.bitcast_convert_type(x, I32)
    return b ^ ((b >> 31) & 0x7FFFFFFF)


def _dsa_kernel(qi_ref, kw_ref, q_ref, ki_ref, k_ref, v_ref, bias_ref, o_ref, sc_ref, *, bq, top_k):
    i = pl.program_id(0)
    w = kw_ref[:, IDX_HEAD_DIM:IDX_HEAD_DIM + N_IDX_HEADS] * (
        (IDX_HEAD_DIM ** -0.5) * (N_IDX_HEADS ** -0.5))
    row = lax.broadcasted_iota(I32, (bq, bq), 0)
    col = lax.broadcasted_iota(I32, (bq, bq), 1)
    causal = col <= row

    def rows(c):
        return pl.ds(pl.multiple_of(c * bq, bq), bq)

    def scores(c):
        kic = ki_ref[rows(c), :]
        acc = jnp.zeros((bq, bq), F32)
        for h in range(N_IDX_HEADS):
            d = lax.dot_general(qi_ref[:, h * IDX_HEAD_DIM:(h + 1) * IDX_HEAD_DIM], kic, NT_DIMS,
                                preferred_element_type=F32)
            acc = acc + w[:, h:h + 1] * jnp.maximum(d, 0.0)
        return acc

    def score_chunk(c, carry):
        sc_ref[c] = _sortable_key(scores(c))
        return carry

    lax.fori_loop(0, i, score_chunk, 0)
    sc_ref[i] = _sortable_key(jnp.where(causal, scores(i), -jnp.inf))

    def count_ge(cand):
        def body(c, cnt):
            t = jnp.where(sc_ref[c] >= cand, 1.0, 0.0)
            for j in range(bq // LANES):
                cnt = cnt + t[:, j * LANES:(j + 1) * LANES]
            return cnt
        cnt = lax.fori_loop(0, i + 1, body, jnp.zeros((bq, LANES), F32))
        return jnp.sum(cnt, axis=1, keepdims=True)

    def search_bit(b, tu):
        cu = tu | lax.shift_left(jnp.int32(1), 31 - b)
        ok = count_ge(cu ^ INT_MIN) >= float(top_k)
        return jnp.where(ok, cu, tu)

    tu = lax.fori_loop(0, 32, search_bit, jnp.zeros((bq, 1), I32))
    thresh = tu ^ INT_MIN

    def mask_chunk(c, carry):
        mb = jnp.where(sc_ref[c] >= thresh, 0.0, NEG)
        sc_ref[c] = lax.bitcast_convert_type(mb, I32)
        return carry

    lax.fori_loop(0, i, mask_chunk, 0)
    mb = jnp.where(causal, jnp.where(sc_ref[i] >= thresh, 0.0, NEG), NEG)
    sc_ref[i] = lax.bitcast_convert_type(mb, I32)

    scale = ATTN_HEAD_DIM ** -0.5
    for h in range(N_ATTN_HEADS):
        cols = slice(h * ATTN_HEAD_DIM, (h + 1) * ATTN_HEAD_DIM)
        qh = q_ref[:, cols]

        def step(c, carry, bias, cols=cols, qh=qh):
            m, l, acc = carry
            s = lax.dot_general(qh, k_ref[rows(c), cols], NT_DIMS, preferred_element_type=F32) * scale
            s = s + lax.bitcast_convert_type(sc_ref[c], F32)
            if bias is not None:
                s = s + bias
            m_new = jnp.maximum(m, jnp.max(s, axis=1, keepdims=True))
            p = jnp.exp(s - m_new)
            a = jnp.exp(m - m_new)
            l = a * l + jnp.sum(p, axis=1, keepdims=True)
            acc = a * acc + jnp.dot(p.astype(BF16), v_ref[rows(c), cols], preferred_element_type=F32)
            return m_new, l, acc

        carry = (jnp.full((bq, 1), NEG, F32), jnp.zeros((bq, 1), F32), jnp.zeros((bq, ATTN_HEAD_DIM), F32))
        carry = lax.fori_loop(0, jnp.maximum(i - 1, 0), lambda c, cr: step(c, cr, None), carry)
        carry = lax.cond(i > 0,
                         lambda cr, h=h: step(i - 1, cr, bias_ref[h, :, :bq]),
                         lambda cr: cr, carry)
        m, l, acc = step(i, carry, bias_ref[h, :, bq:])
        o_ref[:, cols] = (acc / l).astype(o_ref.dtype)


def _rel_bucket(dist):
    n = jnp.maximum(dist, 0)
    max_exact = N_REL_BUCKETS // 2
    nf = jnp.maximum(n, 1).astype(F32)
    large = max_exact + (jnp.log(nf / max_exact) / math.log(REL_MAX_DISTANCE / max_exact)
                         * (N_REL_BUCKETS - max_exact)).astype(I32)
    large = jnp.minimum(large, N_REL_BUCKETS - 1)
    return jnp.where(n < max_exact, n, large)


def _near_bias_tiles(rel_bias, bq):
    assert bq >= REL_MAX_DISTANCE
    r = jnp.arange(bq, dtype=I32)[:, None]
    c = jnp.arange(2 * bq, dtype=I32)[None, :]
    dist = r + bq - c
    tiles = rel_bias[_rel_bucket(dist)] - rel_bias[N_REL_BUCKETS - 1]
    return jnp.transpose(tiles, (2, 0, 1))


def _dsa_attention(z1, kw, ki_b, rel_bias, bq):
    S = z1.shape[0]
    top_k = min(TOPK_MAX, S // 4)
    n_blk = S // bq
    bias = _near_bias_tiles(rel_bias, bq)
    kernel = functools.partial(_dsa_kernel, bq=bq, top_k=top_k)
    return pl.pallas_call(
        kernel,
        grid=(n_blk,),
        in_specs=[pl.BlockSpec((bq, N_IDX_HEADS * IDX_HEAD_DIM), lambda i: (i, 3)),
                  pl.BlockSpec((bq, LANES), lambda i: (i, 0)),
                  pl.BlockSpec((bq, ATTN_WIDTH), lambda i: (i, 0)),
                  _resident((S, IDX_HEAD_DIM), lambda i: (0, 0)),
                  _resident((S, ATTN_WIDTH), lambda i: (0, 1)),
                  _resident((S, ATTN_WIDTH), lambda i: (0, 2)),
                  _resident((N_ATTN_HEADS, bq, 2 * bq), lambda i: (0, 0, 0))],
        out_specs=pl.BlockSpec((bq, ATTN_WIDTH), lambda i: (i, 0)),
        out_shape=jax.ShapeDtypeStruct((S, ATTN_WIDTH), BF16),
        scratch_shapes=[pltpu.VMEM((n_blk, bq, bq), I32)],
        compiler_params=_params("arbitrary"),
        name="dsa_attention",
    )(z1, kw, z1, ki_b, z1, z1, bias)


def _sgu_kernel(gu_ref, gv_ref, g_ref, b_ref, ws_ref, bs_ref, o_ref, *, n_chunk):
    u = _gelu(gu_ref[...])
    v = _layer_norm(_gelu(gv_ref[...]), g_ref[...], b_ref[...]).astype(BF16)
    r = lax.broadcasted_iota(I32, (GMLP_CHUNK, GMLP_CHUNK), 0)
    c = lax.broadcasted_iota(I32, (GMLP_CHUNK, GMLP_CHUNK), 1)
    for g in range(GMLP_GROUPS):
        wg = jnp.where(c <= r, ws_ref[g], 0.0).astype(BF16)
        bcol = bs_ref[:, g:g + 1]
        cols = slice(g * LANES, (g + 1) * LANES)
        for n in range(n_chunk):
            rws = slice(n * GMLP_CHUNK, (n + 1) * GMLP_CHUNK)
            mixed = jnp.dot(wg, v[rws, cols], preferred_element_type=F32) + bcol
            o_ref[rws, cols] = (u[rws, cols] * mixed).astype(o_ref.dtype)


def _sgu(zg, ln_g, ln_b, w_s, b_s, tm):
    S = zg.shape[0]
    kernel = functools.partial(_sgu_kernel, n_chunk=tm // GMLP_CHUNK)
    return pl.pallas_call(
        kernel,
        grid=(S // tm,),
        in_specs=[pl.BlockSpec((tm, GMLP_WIDTH), lambda i: (i, 0)),
                  pl.BlockSpec((tm, GMLP_WIDTH), lambda i: (i, 1)),
                  pl.BlockSpec((1, GMLP_WIDTH), lambda i: (0, 0)),
                  pl.BlockSpec((1, GMLP_WIDTH), lambda i: (0, 0)),
                  pl.BlockSpec((GMLP_GROUPS, GMLP_CHUNK, GMLP_CHUNK), lambda i: (0, 0, 0)),
                  pl.BlockSpec((GMLP_CHUNK, GMLP_GROUPS), lambda i: (0, 0))],
        out_specs=pl.BlockSpec((tm, GMLP_WIDTH), lambda i: (i, 0)),
        out_shape=jax.ShapeDtypeStruct((S, GMLP_WIDTH), BF16),
        compiler_params=_params("parallel"),
        name="gmlp_sgu",
    )(zg, zg, ln_g.reshape(1, -1), ln_b.reshape(1, -1), w_s, b_s.T)


def _merge_kernel(at_ref, gm_ref, ga_ref, gg_ref, h_ref, wa_ref, wg_ref, wo_ref, g_ref, b_ref,
                  h1_ref, h1b_ref):
    a1 = jnp.dot(at_ref[...], wa_ref[...], preferred_element_type=F32)
    a2 = jnp.dot(gm_ref[...], wg_ref[...], preferred_element_type=F32)
    merged = jax.nn.sigmoid(ga_ref[...]) * a1 + jax.nn.sigmoid(gg_ref[...]) * a2
    y = DEEPNORM_ALPHA * h_ref[...] + jnp.dot(merged.astype(BF16), wo_ref[...], preferred_element_type=F32)
    h1 = _layer_norm(y, g_ref[...], b_ref[...])
    h1_ref[...] = h1
    h1b_ref[...] = h1.astype(BF16)


def _merge(attn, gm, zg, h, wa, wg, wo, ln_g, ln_b, tm):
    S, D = h.shape
    W = attn.shape[1]
    return pl.pallas_call(
        _merge_kernel,
        grid=(S // tm,),
        in_specs=[pl.BlockSpec((tm, W), lambda i: (i, 0)),
                  pl.BlockSpec((tm, W), lambda i: (i, 0)),
                  pl.BlockSpec((tm, D), lambda i: (i, 1)),
                  pl.BlockSpec((tm, D), lambda i: (i, 2)),
                  pl.BlockSpec((tm, D), lambda i: (i, 0)),
                  _resident((W, D), lambda i: (0, 0)),
                  _resident((W, D), lambda i: (0, 0)),
                  _resident((D, D), lambda i: (0, 0)),
                  pl.BlockSpec((1, D), lambda i: (0, 0)),
                  pl.BlockSpec((1, D), lambda i: (0, 0))],
        out_specs=[pl.BlockSpec((tm, D), lambda i: (i, 0)),
                   pl.BlockSpec((tm, D), lambda i: (i, 0))],
        out_shape=[jax.ShapeDtypeStruct((S, D), F32), jax.ShapeDtypeStruct((S, D), BF16)],
        compiler_params=_params("parallel"),
        name="merge_ln1",
    )(attn, gm, zg, zg, h, wa, wg, wo, ln_g.reshape(1, D), ln_b.reshape(1, D))


def _peer_pairs():
    pairs = [(a, b) for a in range(PEER_TOPK) for b in range(PEER_TOPK) if (a + 1) * (b + 1) <= PEER_TOPK]
    return sorted(pairs, key=lambda ab: ab[0] * PEER_TOPK + ab[1])


def _extract_top(work, n_rounds, on_pick):
    n = work.shape[0]
    iota = lax.broadcasted_iota(I32, work.shape, 0)
    for r in range(n_rounds):
        mx = jnp.max(work, axis=0, keepdims=True)
        first = jnp.min(jnp.where(work == mx, iota, n), axis=0, keepdims=True)
        pick = iota == first
        on_pick(r, mx, pick)
        work = jnp.where(pick, -jnp.inf, work)


def _route_kernel(hb_ref, wqt_ref, sk_ref, l1_ref, a_ref, r2_ref, b_ref, *, T):
    qt = lax.dot_general(wqt_ref[...], hb_ref[...], NT_DIMS, preferred_element_type=F32)
    pairs = _peer_pairs()
    for h in range(PEER_HEADS):
        s, rank, vals = [], [], []
        for p in range(2):
            idx = 2 * h + p
            qhp = qt[idx * LANES:(idx + 1) * LANES, :].astype(BF16)
            sp = jnp.dot(sk_ref[idx], qhp, preferred_element_type=F32)
            state = {"rank": jnp.full(sp.shape, float(PEER_TOPK), F32), "vals": []}

            def on_pick(r, mx, pick, state=state):
                state["rank"] = jnp.where(pick, float(r), state["rank"])
                state["vals"].append(mx)

            _extract_top(sp, PEER_TOPK, on_pick)
            s.append(sp)
            rank.append(state["rank"])
            vals.append(state["vals"])
        cand = jnp.concatenate([vals[0][a] + vals[1][b] for a, b in pairs], axis=0)
        picked = {"mask": jnp.zeros(cand.shape, jnp.bool_)}

        def on_pick_c(r, mx, pick, picked=picked):
            picked["mask"] = picked["mask"] | pick

        _extract_top(cand, PEER_TOPK, on_pick_c)
        top = vals[0][0] + vals[1][0]
        selw = jnp.where(picked["mask"], jnp.exp(cand - top), 0.0)
        z = jnp.sum(selw, axis=0, keepdims=True)
        self32 = jnp.where(picked["mask"], 1.0, 0.0)
        l1 = jnp.zeros(s[0].shape, F32)
        pos = 0
        for a in range(PEER_TOPK):
            cnt = sum(1 for ab in pairs if ab[0] == a)
            n_a = jnp.sum(self32[pos:pos + cnt, :], axis=0, keepdims=True)
            pos += cnt
            l1 = jnp.where(rank[0] == float(a), n_a, l1)
        l1_ref[h] = l1
        a_ref[h] = jnp.exp(s[0] - vals[0][0])
        r2_ref[h] = rank[1]
        b_ref[h] = jnp.exp(s[1] - vals[1][0]) / z


def _peer_route(h1b, wqt, sk, T):
    S, D = h1b.shape
    kernel = functools.partial(_route_kernel, T=T)
    shp = jax.ShapeDtypeStruct((PEER_HEADS, PEER_N_KEYS, S), F32)
    spec = pl.BlockSpec((PEER_HEADS, PEER_N_KEYS, T), lambda t: (0, 0, t))
    return pl.pallas_call(
        kernel,
        grid=(S // T,),
        in_specs=[pl.BlockSpec((T, D), lambda t: (t, 0)),
                  _resident(wqt.shape, lambda t: (0, 0)),
                  _resident(sk.shape, lambda t: (0, 0, 0))],
        out_specs=[spec, spec, spec, spec],
        out_shape=[shp, shp, shp, shp],
        compiler_params=_params("parallel"),
        name="peer_route",
    )(h1b, wqt, sk)


def _peer_kernel(hb_ref, u_ref, v_ref, l1_ref, a_ref, r2_ref, b_ref, o_ref, *, n_i1):
    @pl.when(pl.program_id(1) == 0)
    def _():
        o_ref[...] = jnp.zeros_like(o_ref)

    act = _gelu(lax.dot_general(u_ref[...], hb_ref[...], NT_DIMS, preferred_element_type=F32))
    tiles = []
    for j in range(n_i1):
        gate = None
        for h in range(PEER_HEADS):
            chosen = r2_ref[h] < l1_ref[h, j:j + 1, :]
            val = jnp.where(chosen, a_ref[h, j:j + 1, :] * b_ref[h], 0.0)
            gate = val if gate is None else gate + val
        tiles.append((act[j * LANES:(j + 1) * LANES, :] * gate).astype(BF16))
    pt = jnp.concatenate(tiles, axis=0)
    o_ref[...] += lax.dot_general(pt, v_ref[...], TN_DIMS, preferred_element_type=F32)


def _peer_experts(h1b, u_b, v_b, l1, a, r2, b, T, eb):
    S, D = h1b.shape
    E = u_b.shape[0]
    n_i1 = eb // PEER_N_KEYS
    kernel = functools.partial(_peer_kernel, n_i1=n_i1)
    row_spec = pl.BlockSpec((PEER_HEADS, n_i1, T), lambda t, e: (0, e, t))
    full_spec = pl.BlockSpec((PEER_HEADS, PEER_N_KEYS, T), lambda t, e: (0, 0, t))
    return pl.pallas_call(
        kernel,
        grid=(S // T, E // eb),
        in_specs=[pl.BlockSpec((T, D), lambda t, e: (t, 0)),
                  pl.BlockSpec((eb, D), lambda t, e: (e, 0)),
                  pl.BlockSpec((eb, D), lambda t, e: (e, 0)),
                  row_spec, row_spec, full_spec, full_spec],
        out_specs=pl.BlockSpec((T, D), lambda t, e: (t, 0)),
        out_shape=jax.ShapeDtypeStruct((S, D), F32),
        compiler_params=_params("parallel", "arbitrary"),
        name="peer_experts",
    )(h1b, u_b, v_b, l1, a, r2, b)


def _ple_kernel(h1_ref, peer_ref, p_ref, wg_ref, wp_ref, g_ref, b_ref, o_ref):
    r = DEEPNORM_ALPHA * h1_ref[...] + peer_ref[...]
    gate = jax.nn.sigmoid(jnp.dot(r.astype(BF16), wg_ref[...], preferred_element_type=F32))
    proj = jnp.dot(p_ref[...], wp_ref[...], preferred_element_type=F32)
    o_ref[...] = _layer_norm(r + gate * proj, g_ref[...], b_ref[...])


def _ple(h1, peer, p_b, wg, wp, ln_g, ln_b, tm):
    S, D = h1.shape
    P = p_b.shape[1]
    return pl.pallas_call(
        _ple_kernel,
        grid=(S // tm,),
        in_specs=[pl.BlockSpec((tm, D), lambda i: (i, 0)),
                  pl.BlockSpec((tm, D), lambda i: (i, 0)),
                  pl.BlockSpec((tm, P), lambda i: (i, 0)),
                  _resident((D, D), lambda i: (0, 0)),
                  _resident((P, D), lambda i: (0, 0)),
                  pl.BlockSpec((1, D), lambda i: (0, 0)),
                  pl.BlockSpec((1, D), lambda i: (0, 0))],
        out_specs=pl.BlockSpec((tm, D), lambda i: (i, 0)),
        out_shape=jax.ShapeDtypeStruct((S, D), F32),
        compiler_params=_params("parallel"),
        name="ple_ln2",
    )(h1, peer, p_b, wg, wp, ln_g.reshape(1, D), ln_b.reshape(1, D))


def _layer(x, p, ln_emb_g, ln_emb_b, rel_bias, w_in, gmlp_ln_g, gmlp_ln_b, gmlp_w_s, gmlp_b_s,
           w_br_attn, w_br_gmlp, w_mix_out, ln1_g, ln1_b, peer_w_q, peer_sub_keys, peer_u, peer_v,
           ple_w_proj, ple_w_gate, ln2_g, ln2_b):
    S, D = x.shape
    tm = min(512, S)
    o_qkvi = 3 * ATTN_WIDTH + N_IDX_HEADS * IDX_HEAD_DIM
    o_kw = o_qkvi + IDX_HEAD_DIM + N_IDX_HEADS

    h, hb = _ln_embed(x, ln_emb_g, ln_emb_b, tm)
    w1 = w_in[:, :o_qkvi].astype(BF16)
    w2 = jnp.pad(w_in[:, o_qkvi:o_kw], ((0, 0), (0, LANES - (o_kw - o_qkvi)))).astype(BF16)
    w3 = w_in[:, o_kw:].astype(BF16)
    z1 = _matmul(hb, w1, BF16, min(1024, S), 512, "in_proj_attn")
    kw = _matmul(hb, w2, F32, min(1024, S), LANES, "in_proj_index")
    zg = _matmul(hb, w3, F32, min(1024, S), 512, "in_proj_gmlp")

    attn = _dsa_attention(z1, kw, kw[:, :IDX_HEAD_DIM].astype(BF16), rel_bias, bq=256)
    gm = _sgu(zg, gmlp_ln_g, gmlp_ln_b, gmlp_w_s, gmlp_b_s, tm)
    h1, h1b = _merge(attn, gm, zg, h, w_br_attn.astype(BF16), w_br_gmlp.astype(BF16),
                     w_mix_out.astype(BF16), ln1_g, ln1_b, min(256, S))

    sk = peer_sub_keys.reshape(PEER_HEADS * 2, PEER_N_KEYS, -1).astype(BF16)
    l1, a, r2, b = _peer_route(h1b, peer_w_q.T.astype(BF16), sk, min(256, S))
    peer = _peer_experts(h1b, peer_u.astype(BF16), peer_v.astype(BF16), l1, a, r2, b,
                         T=min(512, S), eb=1024)
    return _ple(h1, peer, p.astype(BF16), ple_w_gate.astype(BF16), ple_w_proj.astype(BF16),
                ln2_g, ln2_b, min(256, S))


def kernel(x, p, positions, ln_emb_g, ln_emb_b, rel_bias, w_in, gmlp_ln_g, gmlp_ln_b, gmlp_w_s, gmlp_b_s, w_br_attn, w_br_gmlp, w_mix_out, ln1_g, ln1_b, peer_w_q, peer_sub_keys, peer_u, peer_v, ple_w_proj, ple_w_gate, ln2_g, ln2_b):
    del positions
    assert x.shape[0] == 1 and w_in.shape[0] == DEPTH
    out = _layer(x[0], p[0, 0], ln_emb_g, ln_emb_b, rel_bias, w_in[0], gmlp_ln_g[0], gmlp_ln_b[0],
                 gmlp_w_s[0], gmlp_b_s[0], w_br_attn[0], w_br_gmlp[0], w_mix_out[0], ln1_g[0],
                 ln1_b[0], peer_w_q[0], peer_sub_keys[0], peer_u[0], peer_v[0], ple_w_proj[0],
                 ple_w_gate[0], ln2_g[0], ln2_b[0])
    return out[None]
```

```python
import functools
import math

import jax
import jax.numpy as jnp
from jax import lax
from jax.experimental import pallas as pl
from jax.experimental.pallas import tpu as pltpu

F32 = jnp.float32
BF16 = jnp.bfloat16
I32 = jnp.int32

LN_EPS = 1e-5
DEPTH = 1
DEEPNORM_ALPHA = (2.0 * DEPTH) ** 0.25

N_ATTN_HEADS = 8
ATTN_HEAD_DIM = 128
ATTN_WIDTH = N_ATTN_HEADS * ATTN_HEAD_DIM
N_IDX_HEADS = 16
IDX_HEAD_DIM = 64
TOPK_MAX = 256
GMLP_CHUNK = 128
GMLP_GROUPS = 8
GMLP_WIDTH = 1024
N_REL_BUCKETS = 32
REL_MAX_DISTANCE = 128
PEER_HEADS = 8
PEER_N_KEYS = 128
PEER_TOPK = 16

LANES = 128
SUBLANES = 8
NEG = -1e30
INT_MIN = -(2 ** 31)
NT_DIMS = (((1,), (1,)), ((), ()))
TN_DIMS = (((0,), (0,)), ((), ()))
VMEM_LIMIT = 56 * 1024 * 1024


def _params(*sem):
    return pltpu.CompilerParams(dimension_semantics=sem, vmem_limit_bytes=VMEM_LIMIT)


def _resident(shape, index_map):
    return pl.BlockSpec(shape, index_map, pipeline_mode=pl.Buffered(1))


def _gelu(x):
    return 0.5 * x * (1.0 + jnp.tanh(0.7978845608028654 * (x + 0.044715 * (x * x * x))))


def _layer_norm(x, g, b):
    mu = jnp.mean(x, axis=-1, keepdims=True)
    xc = x - mu
    var = jnp.mean(xc * xc, axis=-1, keepdims=True)
    return xc * lax.rsqrt(var + LN_EPS) * g + b


def _ln_kernel(x_ref, g_ref, b_ref, h_ref, hb_ref):
    y = _layer_norm(x_ref[...], g_ref[...], b_ref[...])
    h_ref[...] = y
    hb_ref[...] = y.astype(BF16)


def _ln_embed(x, g, b, tm):
    S, D = x.shape
    return pl.pallas_call(
        _ln_kernel,
        grid=(S // tm,),
        in_specs=[pl.BlockSpec((tm, D), lambda i: (i, 0)),
                  pl.BlockSpec((1, D), lambda i: (0, 0)),
                  pl.BlockSpec((1, D), lambda i: (0, 0))],
        out_specs=[pl.BlockSpec((tm, D), lambda i: (i, 0)),
                   pl.BlockSpec((tm, D), lambda i: (i, 0))],
        out_shape=[jax.ShapeDtypeStruct((S, D), F32), jax.ShapeDtypeStruct((S, D), BF16)],
        compiler_params=_params("parallel"),
        name="ln_embed",
    )(x, g.reshape(1, D), b.reshape(1, D))


def _mm_kernel(a_ref, w_ref, o_ref):
    o_ref[...] = jnp.dot(a_ref[...], w_ref[...], preferred_element_type=F32).astype(o_ref.dtype)


def _matmul(a, w, out_dtype, tm, tn, name):
    M, K = a.shape
    N = w.shape[1]
    return pl.pallas_call(
        _mm_kernel,
        grid=(M // tm, N // tn),
        in_specs=[pl.BlockSpec((tm, K), lambda i, j: (i, 0)),
                  pl.BlockSpec((K, tn), lambda i, j: (0, j))],
        out_specs=pl.BlockSpec((tm, tn), lambda i, j: (i, j)),
        out_shape=jax.ShapeDtypeStruct((M, N), out_dtype),
        compiler_params=_params("parallel", "arbitrary"),
        name=name,
    )(a, w)


def _mm_t_kernel(wt_ref, a_ref, o_ref):
    o_ref[...] = lax.dot_general(wt_ref[...], a_ref[...], NT_DIMS,
                                 preferred_element_type=F32).astype(o_ref.dtype)


def _matmul_t(wt, a, out_dtype, tm, name):
    M, K = a.shape
    N = wt.shape[0]
    return pl.pallas_call(
        _mm_t_kernel,
        grid=(M // tm,),
        in_specs=[_resident((N, K), lambda i: (0, 0)),
                  pl.BlockSpec((tm, K), lambda i: (i, 0))],
        out_specs=pl.BlockSpec((N, tm), lambda i: (0, i)),
        out_shape=jax.ShapeDtypeStruct((N, M), out_dtype),
        compiler_params=_params("parallel"),
        name=name,
    )(wt, a)


def _sortable_key(x):
    b = lax.bitcast_convert_type(x, I32)
    return b ^ ((b >> 31) & 0x7FFFFFFF)


def _reduce_row_groups(t, op):
    parts = [t[j * SUBLANES:(j + 1) * SUBLANES, :] for j in range(t.shape[0] // SUBLANES)]
    while len(parts) > 1:
        parts = [op(parts[j], parts[j + 1]) for j in range(0, len(parts) - 1, 2)] + (
            [parts[-1]] if len(parts) % 2 else [])
    return parts[0]


def _dsa_kernel(qit_ref, wt_ref, qt_ref, ki_ref, k_ref, v_ref, bias_ref, o_ref,
                sc_ref, m_ref, l_ref, acc_ref, *, bq, top_k):
    i = pl.program_id(0)
    w = wt_ref[...] * ((IDX_HEAD_DIM ** -0.5) * (N_IDX_HEADS ** -0.5))
    key_i = lax.broadcasted_iota(I32, (bq, bq), 0)
    qry_i = lax.broadcasted_iota(I32, (bq, bq), 1)
    causal = key_i <= qry_i

    def rows(c):
        return pl.ds(pl.multiple_of(c * bq, bq), bq)

    def scores(c):
        kic = ki_ref[rows(c), :]
        acc = jnp.zeros((bq, bq), F32)
        for h in range(N_IDX_HEADS):
            d = jnp.dot(kic, qit_ref[h * IDX_HEAD_DIM:(h + 1) * IDX_HEAD_DIM, :],
                        preferred_element_type=F32)
            acc = acc + w[h:h + 1, :] * jnp.maximum(d, 0.0)
        return acc

    def score_chunk(c, carry):
        sc_ref[c] = _sortable_key(scores(c))
        return carry

    lax.fori_loop(0, i, score_chunk, 0)
    sc_ref[i] = _sortable_key(jnp.where(causal, scores(i), -jnp.inf))

    def count_ge(cand):
        def body(c, cnt):
            return cnt + _reduce_row_groups(jnp.where(sc_ref[c] >= cand, 1.0, 0.0), jnp.add)
        cnt = lax.fori_loop(0, i + 1, body, jnp.zeros((SUBLANES, bq), F32))
        return jnp.sum(cnt, axis=0, keepdims=True)

    def search_bit(b, tu):
        cu = tu | lax.shift_left(jnp.int32(1), 31 - b)
        ok = count_ge(cu ^ INT_MIN) >= float(top_k)
        return jnp.where(ok, cu, tu)

    tu = lax.fori_loop(0, 32, search_bit, jnp.zeros((1, bq), I32))
    thresh = tu ^ INT_MIN

    def mask_chunk(c, carry):
        mb = jnp.where(sc_ref[c] >= thresh, 0.0, NEG)
        sc_ref[c] = lax.bitcast_convert_type(mb, I32)
        return carry

    lax.fori_loop(0, i, mask_chunk, 0)
    mb = jnp.where(causal, jnp.where(sc_ref[i] >= thresh, 0.0, NEG), NEG)
    sc_ref[i] = lax.bitcast_convert_type(mb, I32)

    scale = ATTN_HEAD_DIM ** -0.5
    m_ref[...] = jnp.full(m_ref.shape, NEG, F32)
    l_ref[...] = jnp.zeros(l_ref.shape, F32)
    acc_ref[...] = jnp.zeros(acc_ref.shape, F32)

    def attend(c, near):
        mbias = lax.bitcast_convert_type(sc_ref[c], F32)
        m_all, l_all = m_ref[...], l_ref[...]
        m_rows, l_rows = [], []

        def qk(h):
            cols = slice(h * ATTN_HEAD_DIM, (h + 1) * ATTN_HEAD_DIM)
            return jnp.dot(k_ref[rows(c), cols], qt_ref[cols, :], preferred_element_type=F32)

        s_next = qk(0)
        for h in range(N_ATTN_HEADS):
            cols = slice(h * ATTN_HEAD_DIM, (h + 1) * ATTN_HEAD_DIM)
            s = s_next * scale + mbias
            if h + 1 < N_ATTN_HEADS:
                s_next = qk(h + 1)
            if near is not None:
                s = s + bias_ref[h, near * bq:(near + 1) * bq, :]
            m_old = m_all[h:h + 1, :]
            m_new = jnp.maximum(m_old, jnp.max(_reduce_row_groups(s, jnp.maximum), axis=0, keepdims=True))
            p = jnp.exp(s - m_new)
            a = jnp.exp(m_old - m_new)
            m_rows.append(m_new)
            l_rows.append(a * l_all[h:h + 1, :]
                          + jnp.sum(_reduce_row_groups(p, jnp.add), axis=0, keepdims=True))
            pv = lax.dot_general(v_ref[rows(c), cols], p.astype(BF16), TN_DIMS, preferred_element_type=F32)
            acc_ref[h] = a * acc_ref[h] + pv
        m_ref[...] = jnp.concatenate(m_rows, axis=0)
        l_ref[...] = jnp.concatenate(l_rows, axis=0)

    def far_chunk(c, carry):
        attend(c, None)
        return carry

    lax.fori_loop(0, jnp.maximum(i - 1, 0), far_chunk, 0)

    @pl.when(i > 0)
    def _():
        attend(i - 1, 0)

    attend(i, 1)
    for h in range(N_ATTN_HEADS):
        out_t = acc_ref[h] / l_ref[h:h + 1, :]
        o_ref[:, h * ATTN_HEAD_DIM:(h + 1) * ATTN_HEAD_DIM] = out_t.T.astype(o_ref.dtype)


def _rel_bucket(dist):
    n = jnp.maximum(dist, 0)
    max_exact = N_REL_BUCKETS // 2
    nf = jnp.maximum(n, 1).astype(F32)
    large = max_exact + (jnp.log(nf / max_exact) / math.log(REL_MAX_DISTANCE / max_exact)
                         * (N_REL_BUCKETS - max_exact)).astype(I32)
    large = jnp.minimum(large, N_REL_BUCKETS - 1)
    return jnp.where(n < max_exact, n, large)


def _near_bias_tiles(rel_bias, bq):
    assert bq >= REL_MAX_DISTANCE
    key = jnp.arange(2 * bq, dtype=I32)[:, None]
    qry = jnp.arange(bq, dtype=I32)[None, :]
    bucket = _rel_bucket(qry + bq - key)
    rel = (rel_bias - rel_bias[N_REL_BUCKETS - 1]).T
    tiles = jnp.zeros((rel.shape[0],) + bucket.shape, F32)
    for b in range(N_REL_BUCKETS):
        tiles = jnp.where(bucket[None] == b, rel[:, b][:, None, None], tiles)
    return tiles


def _dsa_attention(qt, kv, ki_b, wt, rel_bias, bq):
    S = kv.shape[0]
    top_k = min(TOPK_MAX, S // 4)
    n_blk = S // bq
    bias = _near_bias_tiles(rel_bias, bq)
    kernel = functools.partial(_dsa_kernel, bq=bq, top_k=top_k)
    return pl.pallas_call(
        kernel,
        grid=(n_blk,),
        in_specs=[pl.BlockSpec((N_IDX_HEADS * IDX_HEAD_DIM, bq), lambda i: (1, i)),
                  pl.BlockSpec((N_IDX_HEADS, bq), lambda i: (0, i)),
                  pl.BlockSpec((ATTN_WIDTH, bq), lambda i: (0, i)),
                  _resident((S, IDX_HEAD_DIM), lambda i: (0, 0)),
                  _resident((S, ATTN_WIDTH), lambda i: (0, 0)),
                  _resident((S, ATTN_WIDTH), lambda i: (0, 1)),
                  _resident((N_ATTN_HEADS, 2 * bq, bq), lambda i: (0, 0, 0))],
        out_specs=pl.BlockSpec((bq, ATTN_WIDTH), lambda i: (i, 0)),
        out_shape=jax.ShapeDtypeStruct((S, ATTN_WIDTH), BF16),
        scratch_shapes=[pltpu.VMEM((n_blk, bq, bq), I32),
                        pltpu.VMEM((N_ATTN_HEADS, bq), F32),
                        pltpu.VMEM((N_ATTN_HEADS, bq), F32),
                        pltpu.VMEM((N_ATTN_HEADS, ATTN_HEAD_DIM, bq), F32)],
        compiler_params=_params("arbitrary"),
        name="dsa_attention",
    )(qt, wt, qt, ki_b, kv, kv, bias)


def _sgu_kernel(gu_ref, gv_ref, g_ref, b_ref, ws_ref, bs_ref, o_ref, *, n_chunk):
    u = _gelu(gu_ref[...])
    v = _layer_norm(_gelu(gv_ref[...]), g_ref[...], b_ref[...]).astype(BF16)
    r = lax.broadcasted_iota(I32, (GMLP_CHUNK, GMLP_CHUNK), 0)
    c = lax.broadcasted_iota(I32, (GMLP_CHUNK, GMLP_CHUNK), 1)
    for g in range(GMLP_GROUPS):
        wg = jnp.where(c <= r, ws_ref[g], 0.0).astype(BF16)
        bcol = bs_ref[:, g:g + 1]
        cols = slice(g * LANES, (g + 1) * LANES)
        for n in range(n_chunk):
            rws = slice(n * GMLP_CHUNK, (n + 1) * GMLP_CHUNK)
            mixed = jnp.dot(wg, v[rws, cols], preferred_element_type=F32) + bcol
            o_ref[rws, cols] = (u[rws, cols] * mixed).astype(o_ref.dtype)


def _sgu(zg, ln_g, ln_b, w_s, b_s, tm):
    S = zg.shape[0]
    kernel = functools.partial(_sgu_kernel, n_chunk=tm // GMLP_CHUNK)
    return pl.pallas_call(
        kernel,
        grid=(S // tm,),
        in_specs=[pl.BlockSpec((tm, GMLP_WIDTH), lambda i: (i, 0)),
                  pl.BlockSpec((tm, GMLP_WIDTH), lambda i: (i, 1)),
                  pl.BlockSpec((1, GMLP_WIDTH), lambda i: (0, 0)),
                  pl.BlockSpec((1, GMLP_WIDTH), lambda i: (0, 0)),
                  pl.BlockSpec((GMLP_GROUPS, GMLP_CHUNK, GMLP_CHUNK), lambda i: (0, 0, 0)),
                  pl.BlockSpec((GMLP_CHUNK, GMLP_GROUPS), lambda i: (0, 0))],
        out_specs=pl.BlockSpec((tm, GMLP_WIDTH), lambda i: (i, 0)),
        out_shape=jax.ShapeDtypeStruct((S, GMLP_WIDTH), BF16),
        compiler_params=_params("parallel"),
        name="gmlp_sgu",
    )(zg, zg, ln_g.reshape(1, -1), ln_b.reshape(1, -1), w_s, b_s.T)


def _merge_kernel(at_ref, gm_ref, ga_ref, gg_ref, h_ref, wa_ref, wg_ref, wo_ref, g_ref, b_ref,
                  h1_ref, h1b_ref):
    a1 = jnp.dot(at_ref[...], wa_ref[...], preferred_element_type=F32)
    a2 = jnp.dot(gm_ref[...], wg_ref[...], preferred_element_type=F32)
    merged = jax.nn.sigmoid(ga_ref[...]) * a1 + jax.nn.sigmoid(gg_ref[...]) * a2
    y = DEEPNORM_ALPHA * h_ref[...] + jnp.dot(merged.astype(BF16), wo_ref[...], preferred_element_type=F32)
    h1 = _layer_norm(y, g_ref[...], b_ref[...])
    h1_ref[...] = h1
    h1b_ref[...] = h1.astype(BF16)


def _merge(attn, gm, zg, h, wa, wg, wo, ln_g, ln_b, tm):
    S, D = h.shape
    W = attn.shape[1]
    return pl.pallas_call(
        _merge_kernel,
        grid=(S // tm,),
        in_specs=[pl.BlockSpec((tm, W), lambda i: (i, 0)),
                  pl.BlockSpec((tm, W), lambda i: (i, 0)),
                  pl.BlockSpec((tm, D), lambda i: (i, 1)),
                  pl.BlockSpec((tm, D), lambda i: (i, 2)),
                  pl.BlockSpec((tm, D), lambda i: (i, 0)),
                  _resident((W, D), lambda i: (0, 0)),
                  _resident((W, D), lambda i: (0, 0)),
                  _resident((D, D), lambda i: (0, 0)),
                  pl.BlockSpec((1, D), lambda i: (0, 0)),
                  pl.BlockSpec((1, D), lambda i: (0, 0))],
        out_specs=[pl.BlockSpec((tm, D), lambda i: (i, 0)),
                   pl.BlockSpec((tm, D), lambda i: (i, 0))],
        out_shape=[jax.ShapeDtypeStruct((S, D), F32), jax.ShapeDtypeStruct((S, D), BF16)],
        compiler_params=_params("parallel"),
        name="merge_ln1",
    )(attn, gm, zg, zg, h, wa, wg, wo, ln_g.reshape(1, D), ln_b.reshape(1, D))


def _peer_pairs():
    pairs = [(a, b) for a in range(PEER_TOPK) for b in range(PEER_TOPK) if (a + 1) * (b + 1) <= PEER_TOPK]
    return sorted(pairs, key=lambda ab: ab[0] * PEER_TOPK + ab[1])


def _extract_top(work, n_rounds, on_pick):
    n = work.shape[0]
    iota = lax.broadcasted_iota(I32, work.shape, 0)
    for r in range(n_rounds):
        mx = jnp.max(work, axis=0, keepdims=True)
        first = jnp.min(jnp.where(work == mx, iota, n), axis=0, keepdims=True)
        pick = iota == first
        on_pick(r, mx, pick)
        work = jnp.where(pick, -jnp.inf, work)


def _route_kernel(hb_ref, wqt_ref, sk_ref, l1_ref, a_ref, r2_ref, b_ref, *, T):
    qt = lax.dot_general(wqt_ref[...], hb_ref[...], NT_DIMS, preferred_element_type=F32)
    pairs = _peer_pairs()
    for h in range(PEER_HEADS):
        s, rank, vals = [], [], []
        for p in range(2):
            idx = 2 * h + p
            qhp = qt[idx * LANES:(idx + 1) * LANES, :].astype(BF16)
            sp = jnp.dot(sk_ref[idx], qhp, preferred_element_type=F32)
            state = {"rank": jnp.full(sp.shape, float(PEER_TOPK), F32), "vals": []}

            def on_pick(r, mx, pick, state=state):
                state["rank"] = jnp.where(pick, float(r), state["rank"])
                state["vals"].append(mx)

            _extract_top(sp, PEER_TOPK, on_pick)
            s.append(sp)
            rank.append(state["rank"])
            vals.append(state["vals"])
        cand = jnp.concatenate([vals[0][a] + vals[1][b] for a, b in pairs], axis=0)
        picked = {"mask": jnp.zeros(cand.shape, jnp.bool_)}

        def on_pick_c(r, mx, pick, picked=picked):
            picked["mask"] = picked["mask"] | pick

        _extract_top(cand, PEER_TOPK, on_pick_c)
        top = vals[0][0] + vals[1][0]
        selw = jnp.where(picked["mask"], jnp.exp(cand - top), 0.0)
        z = jnp.sum(selw, axis=0, keepdims=True)
        self32 = jnp.where(picked["mask"], 1.0, 0.0)
        l1 = jnp.zeros(s[0].shape, F32)
        pos = 0
        for a in range(PEER_TOPK):
            cnt = sum(1 for ab in pairs if ab[0] == a)
            n_a = jnp.sum(self32[pos:pos + cnt, :], axis=0, keepdims=True)
            pos += cnt
            l1 = jnp.where(rank[0] == float(a), n_a, l1)
        l1_ref[h] = l1
        a_ref[h] = jnp.exp(s[0] - vals[0][0])
        r2_ref[h] = rank[1]
        b_ref[h] = jnp.exp(s[1] - vals[1][0]) / z


def _peer_route(h1b, wqt, sk, T):
    S, D = h1b.shape
    kernel = functools.partial(_route_kernel, T=T)
    shp = jax.ShapeDtypeStruct((PEER_HEADS, PEER_N_KEYS, S), F32)
    spec = pl.BlockSpec((PEER_HEADS, PEER_N_KEYS, T), lambda t: (0, 0, t))
    return pl.pallas_call(
        kernel,
        grid=(S // T,),
        in_specs=[pl.BlockSpec((T, D), lambda t: (t, 0)),
                  _resident(wqt.shape, lambda t: (0, 0)),
                  _resident(sk.shape, lambda t: (0, 0, 0))],
        out_specs=[spec, spec, spec, spec],
        out_shape=[shp, shp, shp, shp],
        compiler_params=_params("parallel"),
        name="peer_route",
    )(h1b, wqt, sk)


def _peer_kernel(hb_ref, u_ref, v_ref, l1_ref, a_ref, r2_ref, b_ref, o_ref, *, n_i1):
    @pl.when(pl.program_id(1) == 0)
    def _():
        o_ref[...] = jnp.zeros_like(o_ref)

    act = _gelu(lax.dot_general(u_ref[...], hb_ref[...], NT_DIMS, preferred_element_type=F32))
    tiles = []
    for j in range(n_i1):
        gate = None
        for h in range(PEER_HEADS):
            chosen = r2_ref[h] < l1_ref[h, j:j + 1, :]
            val = jnp.where(chosen, a_ref[h, j:j + 1, :] * b_ref[h], 0.0)
            gate = val if gate is None else gate + val
        tiles.append((act[j * LANES:(j + 1) * LANES, :] * gate).astype(BF16))
    pt = jnp.concatenate(tiles, axis=0)
    o_ref[...] += lax.dot_general(pt, v_ref[...], TN_DIMS, preferred_element_type=F32)


def _peer_experts(h1b, u_b, v_b, l1, a, r2, b, T, eb):
    S, D = h1b.shape
    E = u_b.shape[0]
    n_i1 = eb // PEER_N_KEYS
    kernel = functools.partial(_peer_kernel, n_i1=n_i1)
    row_spec = pl.BlockSpec((PEER_HEADS, n_i1, T), lambda t, e: (0, e, t))
    full_spec = pl.BlockSpec((PEER_HEADS, PEER_N_KEYS, T), lambda t, e: (0, 0, t))
    return pl.pallas_call(
        kernel,
        grid=(S // T, E // eb),
        in_specs=[pl.BlockSpec((T, D), lambda t, e: (t, 0)),
                  pl.BlockSpec((eb, D), lambda t, e: (e, 0)),
                  pl.BlockSpec((eb, D), lambda t, e: (e, 0)),
                  row_spec, row_spec, full_spec, full_spec],
        out_specs=pl.BlockSpec((T, D), lambda t, e: (t, 0)),
        out_shape=jax.ShapeDtypeStruct((S, D), F32),
        compiler_params=_params("parallel", "arbitrary"),
        name="peer_experts",
    )(h1b, u_b, v_b, l1, a, r2, b)


def _ple_kernel(h1_ref, peer_ref, p_ref, wg_ref, wp_ref, g_ref, b_ref, o_ref):
    r = DEEPNORM_ALPHA * h1_ref[...] + peer_ref[...]
    gate = jax.nn.sigmoid(jnp.dot(r.astype(BF16), wg_ref[...], preferred_element_type=F32))
    proj = jnp.dot(p_ref[...], wp_ref[...], preferred_element_type=F32)
    o_ref[...] = _layer_norm(r + gate * proj, g_ref[...], b_ref[...])


def _ple(h1, peer, p_b, wg, wp, ln_g, ln_b, tm):
    S, D = h1.shape
    P = p_b.shape[1]
    return pl.pallas_call(
        _ple_kernel,
        grid=(S // tm,),
        in_specs=[pl.BlockSpec((tm, D), lambda i: (i, 0)),
                  pl.BlockSpec((tm, D), lambda i: (i, 0)),
                  pl.BlockSpec((tm, P), lambda i: (i, 0)),
                  _resident((D, D), lambda i: (0, 0)),
                  _resident((P, D), lambda i: (0, 0)),
                  pl.BlockSpec((1, D), lambda i: (0, 0)),
                  pl.BlockSpec((1, D), lambda i: (0, 0))],
        out_specs=pl.BlockSpec((tm, D), lambda i: (i, 0)),
        out_shape=jax.ShapeDtypeStruct((S, D), F32),
        compiler_params=_params("parallel"),
        name="ple_ln2",
    )(h1, peer, p_b, wg, wp, ln_g.reshape(1, D), ln_b.reshape(1, D))


def _layer(x, p, ln_emb_g, ln_emb_b, rel_bias, w_in, gmlp_ln_g, gmlp_ln_b, gmlp_w_s, gmlp_b_s,
           w_br_attn, w_br_gmlp, w_mix_out, ln1_g, ln1_b, peer_w_q, peer_sub_keys, peer_u, peer_v,
           ple_w_proj, ple_w_gate, ln2_g, ln2_b):
    S, D = x.shape
    tm = min(512, S)
    o_k, o_v, o_qi = ATTN_WIDTH, 2 * ATTN_WIDTH, 3 * ATTN_WIDTH
    o_ki = o_qi + N_IDX_HEADS * IDX_HEAD_DIM
    o_gu = o_ki + IDX_HEAD_DIM + N_IDX_HEADS

    h, hb = _ln_embed(x, ln_emb_g, ln_emb_b, tm)
    wqt = jnp.concatenate([w_in[:, :o_k], w_in[:, o_qi:o_ki]], axis=1).T.astype(BF16)
    w1 = w_in[:, o_k:o_qi].astype(BF16)
    w2 = jnp.pad(w_in[:, o_ki:o_gu], ((0, 0), (0, LANES - (o_gu - o_ki)))).astype(BF16)
    w3 = w_in[:, o_gu:].astype(BF16)
    qt = _matmul_t(wqt, hb, BF16, tm, "in_proj_q")
    kv = _matmul(hb, w1, BF16, min(1024, S), 512, "in_proj_kv")
    kw = _matmul(hb, w2, F32, min(1024, S), LANES, "in_proj_index")
    zg = _matmul(hb, w3, F32, min(1024, S), 512, "in_proj_gmlp")

    attn = _dsa_attention(qt, kv, kw[:, :IDX_HEAD_DIM].astype(BF16),
                          kw[:, IDX_HEAD_DIM:IDX_HEAD_DIM + N_IDX_HEADS].T, rel_bias, bq=256)
    gm = _sgu(zg, gmlp_ln_g, gmlp_ln_b, gmlp_w_s, gmlp_b_s, tm)
    h1, h1b = _merge(attn, gm, zg, h, w_br_attn.astype(BF16), w_br_gmlp.astype(BF16),
                     w_mix_out.astype(BF16), ln1_g, ln1_b, min(256, S))

    sk = peer_sub_keys.reshape(PEER_HEADS * 2, PEER_N_KEYS, -1).astype(BF16)
    l1, a, r2, b = _peer_route(h1b, peer_w_q.T.astype(BF16), sk, min(256, S))
    peer = _peer_experts(h1b, peer_u.astype(BF16), peer_v.astype(BF16), l1, a, r2, b,
                         T=min(512, S), eb=1024)
    return _ple(h1, peer, p.astype(BF16), ple_w_gate.astype(BF16), ple_w_proj.astype(BF16),
                ln2_g, ln2_b, min(256, S))


def kernel(x, p, positions, ln_emb_g, ln_emb_b, rel_bias, w_in, gmlp_ln_g, gmlp_ln_b, gmlp_w_s, gmlp_b_s, w_br_attn, w_br_gmlp, w_mix_out, ln1_g, ln1_b, peer_w_q, peer_sub_keys, peer_u, peer_v, ple_w_proj, ple_w_gate, ln2_g, ln2_b):
    del positions
    assert x.shape[0] == 1 and w_in.shape[0] == DEPTH
    out = _layer(x[0], p[0, 0], ln_emb_g, ln_emb_b, rel_bias, w_in[0], gmlp_ln_g[0], gmlp_ln_b[0],
                 gmlp_w_s[0], gmlp_b_s[0], w_br_attn[0], w_br_gmlp[0], w_mix_out[0], ln1_g[0],
                 ln1_b[0], peer_w_q[0], peer_sub_keys[0], peer_u[0], peer_v[0], ple_w_proj[0],
                 ple_w_gate[0], ln2_g[0], ln2_b[0])
    return out[None]
```

```python
import functools
import math

import jax
import jax.numpy as jnp
from jax import lax
from jax.experimental import pallas as pl
from jax.experimental.pallas import tpu as pltpu

F32 = jnp.float32
BF16 = jnp.bfloat16
I32 = jnp.int32

LN_EPS = 1e-5
DEPTH = 1
DEEPNORM_ALPHA = (2.0 * DEPTH) ** 0.25

N_ATTN_HEADS = 8
ATTN_HEAD_DIM = 128
ATTN_WIDTH = N_ATTN_HEADS * ATTN_HEAD_DIM
N_IDX_HEADS = 16
IDX_HEAD_DIM = 64
TOPK_MAX = 256
GMLP_CHUNK = 128
GMLP_GROUPS = 8
GMLP_WIDTH = 1024
N_REL_BUCKETS = 32
REL_MAX_DISTANCE = 128
PEER_HEADS = 8
PEER_N_KEYS = 128
PEER_TOPK = 16

LANES = 128
SUBLANES = 8
PACKED_ROWS = 2 * SUBLANES
NEG = -1e30
INT_MIN = -(2 ** 31)
NT_DIMS = (((1,), (1,)), ((), ()))
TN_DIMS = (((0,), (0,)), ((), ()))
VMEM_LIMIT = 56 * 1024 * 1024


def _params(*sem):
    return pltpu.CompilerParams(dimension_semantics=sem, vmem_limit_bytes=VMEM_LIMIT)


def _resident(shape, index_map):
    return pl.BlockSpec(shape, index_map, pipeline_mode=pl.Buffered(1))


def _gelu(x):
    return 0.5 * x * (1.0 + jnp.tanh(0.7978845608028654 * (x + 0.044715 * (x * x * x))))


def _layer_norm(x, g, b):
    mu = jnp.mean(x, axis=-1, keepdims=True)
    xc = x - mu
    var = jnp.mean(xc * xc, axis=-1, keepdims=True)
    return xc * lax.rsqrt(var + LN_EPS) * g + b


def _ln_kernel(x_ref, g_ref, b_ref, h_ref, hb_ref):
    y = _layer_norm(x_ref[...], g_ref[...], b_ref[...])
    h_ref[...] = y
    hb_ref[...] = y.astype(BF16)


def _ln_embed(x, g, b, tm):
    S, D = x.shape
    return pl.pallas_call(
        _ln_kernel,
        grid=(S // tm,),
        in_specs=[pl.BlockSpec((tm, D), lambda i: (i, 0)),
                  pl.BlockSpec((1, D), lambda i: (0, 0)),
                  pl.BlockSpec((1, D), lambda i: (0, 0))],
        out_specs=[pl.BlockSpec((tm, D), lambda i: (i, 0)),
                   pl.BlockSpec((tm, D), lambda i: (i, 0))],
        out_shape=[jax.ShapeDtypeStruct((S, D), F32), jax.ShapeDtypeStruct((S, D), BF16)],
        compiler_params=_params("parallel"),
        name="ln_embed",
    )(x, g.reshape(1, D), b.reshape(1, D))


def _mm_kernel(a_ref, w_ref, o_ref):
    o_ref[...] = jnp.dot(a_ref[...], w_ref[...], preferred_element_type=F32).astype(o_ref.dtype)


def _matmul(a, w, out_dtype, tm, tn, name):
    M, K = a.shape
    N = w.shape[1]
    return pl.pallas_call(
        _mm_kernel,
        grid=(M // tm, N // tn),
        in_specs=[pl.BlockSpec((tm, K), lambda i, j: (i, 0)),
                  pl.BlockSpec((K, tn), lambda i, j: (0, j))],
        out_specs=pl.BlockSpec((tm, tn), lambda i, j: (i, j)),
        out_shape=jax.ShapeDtypeStruct((M, N), out_dtype),
        compiler_params=_params("parallel", "arbitrary"),
        name=name,
    )(a, w)


def _mm_t_kernel(wt_ref, a_ref, o_ref):
    o_ref[...] = lax.dot_general(wt_ref[...], a_ref[...], NT_DIMS,
                                 preferred_element_type=F32).astype(o_ref.dtype)


def _matmul_t(wt, a, out_dtype, tm, name):
    M, K = a.shape
    N = wt.shape[0]
    return pl.pallas_call(
        _mm_t_kernel,
        grid=(M // tm,),
        in_specs=[_resident((N, K), lambda i: (0, 0)),
                  pl.BlockSpec((tm, K), lambda i: (i, 0))],
        out_specs=pl.BlockSpec((N, tm), lambda i: (0, i)),
        out_shape=jax.ShapeDtypeStruct((N, M), out_dtype),
        compiler_params=_params("parallel"),
        name=name,
    )(wt, a)


def _sortable_key(x):
    b = lax.bitcast_convert_type(x, I32)
    return b ^ ((b >> 31) & 0x7FFFFFFF)


def _reduce_row_groups(t, op):
    parts = [t[j * SUBLANES:(j + 1) * SUBLANES, :] for j in range(t.shape[0] // SUBLANES)]
    while len(parts) > 1:
        parts = [op(parts[j], parts[j + 1]) for j in range(0, len(parts) - 1, 2)] + (
            [parts[-1]] if len(parts) % 2 else [])
    return parts[0]


def _dsa_kernel(qit_ref, wt_ref, qt_ref, ki_ref, k_ref, v_ref, bias_ref, o_ref,
                sc_ref, m_ref, l_ref, acc_ref, *, bq, top_k):
    i = pl.program_id(0)
    w = wt_ref[...] * ((IDX_HEAD_DIM ** -0.5) * (N_IDX_HEADS ** -0.5))
    key_i = lax.broadcasted_iota(I32, (bq, bq), 0)
    qry_i = lax.broadcasted_iota(I32, (bq, bq), 1)
    causal = key_i <= qry_i

    def rows(c):
        return pl.ds(pl.multiple_of(c * bq, bq), bq)

    def scores(c):
        kic = ki_ref[rows(c), :]
        acc = jnp.zeros((bq, bq), F32)
        for h in range(N_IDX_HEADS):
            d = jnp.dot(kic, qit_ref[h * IDX_HEAD_DIM:(h + 1) * IDX_HEAD_DIM, :],
                        preferred_element_type=F32)
            acc = acc + w[h:h + 1, :] * jnp.maximum(d, 0.0)
        return acc

    def score_chunk(c, carry):
        sc_ref[c] = _sortable_key(scores(c))
        return carry

    lax.fori_loop(0, i, score_chunk, 0)
    sc_ref[i] = _sortable_key(jnp.where(causal, scores(i), -jnp.inf))

    def count_ge(cand):
        def body(c, cnt):
            return cnt + _reduce_row_groups(jnp.where(sc_ref[c] >= cand, 1.0, 0.0), jnp.add)
        cnt = lax.fori_loop(0, i + 1, body, jnp.zeros((SUBLANES, bq), F32))
        return jnp.sum(cnt, axis=0, keepdims=True)

    def search_bit(b, tu):
        cu = tu | lax.shift_left(jnp.int32(1), 31 - b)
        ok = count_ge(cu ^ INT_MIN) >= float(top_k)
        return jnp.where(ok, cu, tu)

    tu = lax.fori_loop(0, 32, search_bit, jnp.zeros((1, bq), I32))
    thresh = tu ^ INT_MIN

    def mask_chunk(c, carry):
        mb = jnp.where(sc_ref[c] >= thresh, 0.0, NEG)
        sc_ref[c] = lax.bitcast_convert_type(mb, I32)
        return carry

    lax.fori_loop(0, i, mask_chunk, 0)
    mb = jnp.where(causal, jnp.where(sc_ref[i] >= thresh, 0.0, NEG), NEG)
    sc_ref[i] = lax.bitcast_convert_type(mb, I32)

    scale = ATTN_HEAD_DIM ** -0.5
    m_ref[...] = jnp.full(m_ref.shape, NEG, F32)
    l_ref[...] = jnp.zeros(l_ref.shape, F32)
    acc_ref[...] = jnp.zeros(acc_ref.shape, F32)

    def attend(c, near):
        mbias = lax.bitcast_convert_type(sc_ref[c], F32)
        m_all, l_all = m_ref[...], l_ref[...]
        m_rows, l_rows = [], []

        def qk(h):
            cols = slice(h * ATTN_HEAD_DIM, (h + 1) * ATTN_HEAD_DIM)
            return jnp.dot(k_ref[rows(c), cols], qt_ref[cols, :], preferred_element_type=F32)

        s_next = qk(0)
        for h in range(N_ATTN_HEADS):
            cols = slice(h * ATTN_HEAD_DIM, (h + 1) * ATTN_HEAD_DIM)
            s = s_next * scale + mbias
            if h + 1 < N_ATTN_HEADS:
                s_next = qk(h + 1)
            if near is not None:
                s = s + bias_ref[h, near * bq:(near + 1) * bq, :]
            m_old = m_all[h:h + 1, :]
            m_new = jnp.maximum(m_old, jnp.max(_reduce_row_groups(s, jnp.maximum), axis=0, keepdims=True))
            p = jnp.exp(s - m_new)
            a = jnp.exp(m_old - m_new)
            m_rows.append(m_new)
            l_rows.append(a * l_all[h:h + 1, :]
                          + jnp.sum(_reduce_row_groups(p, jnp.add), axis=0, keepdims=True))
            pv = lax.dot_general(v_ref[rows(c), cols], p.astype(BF16), TN_DIMS, preferred_element_type=F32)
            acc_ref[h] = a * acc_ref[h] + pv
        m_ref[...] = jnp.concatenate(m_rows, axis=0)
        l_ref[...] = jnp.concatenate(l_rows, axis=0)

    def far_chunk(c, carry):
        attend(c, None)
        return carry

    lax.fori_loop(0, jnp.maximum(i - 1, 0), far_chunk, 0)

    @pl.when(i > 0)
    def _():
        attend(i - 1, 0)

    attend(i, 1)
    for h in range(N_ATTN_HEADS):
        out_t = acc_ref[h] / l_ref[h:h + 1, :]
        o_ref[:, h * ATTN_HEAD_DIM:(h + 1) * ATTN_HEAD_DIM] = out_t.T.astype(o_ref.dtype)


def _rel_bucket(dist):
    n = jnp.maximum(dist, 0)
    max_exact = N_REL_BUCKETS // 2
    nf = jnp.maximum(n, 1).astype(F32)
    large = max_exact + (jnp.log(nf / max_exact) / math.log(REL_MAX_DISTANCE / max_exact)
                         * (N_REL_BUCKETS - max_exact)).astype(I32)
    large = jnp.minimum(large, N_REL_BUCKETS - 1)
    return jnp.where(n < max_exact, n, large)


def _near_bias_tiles(rel_bias, bq):
    assert bq >= REL_MAX_DISTANCE
    key = jnp.arange(2 * bq, dtype=I32)[:, None]
    qry = jnp.arange(bq, dtype=I32)[None, :]
    bucket = _rel_bucket(qry + bq - key)
    rel = (rel_bias - rel_bias[N_REL_BUCKETS - 1]).T
    tiles = jnp.zeros((rel.shape[0],) + bucket.shape, F32)
    for b in range(N_REL_BUCKETS):
        tiles = jnp.where(bucket[None] == b, rel[:, b][:, None, None], tiles)
    return tiles


def _dsa_attention(qt, kv, ki_b, wt, rel_bias, bq):
    S = kv.shape[0]
    top_k = min(TOPK_MAX, S // 4)
    n_blk = S // bq
    bias = _near_bias_tiles(rel_bias, bq)
    kernel = functools.partial(_dsa_kernel, bq=bq, top_k=top_k)
    return pl.pallas_call(
        kernel,
        grid=(n_blk,),
        in_specs=[pl.BlockSpec((N_IDX_HEADS * IDX_HEAD_DIM, bq), lambda i: (1, i)),
                  pl.BlockSpec((N_IDX_HEADS, bq), lambda i: (0, i)),
                  pl.BlockSpec((ATTN_WIDTH, bq), lambda i: (0, i)),
                  _resident((S, IDX_HEAD_DIM), lambda i: (0, 0)),
                  _resident((S, ATTN_WIDTH), lambda i: (0, 0)),
                  _resident((S, ATTN_WIDTH), lambda i: (0, 1)),
                  _resident((N_ATTN_HEADS, 2 * bq, bq), lambda i: (0, 0, 0))],
        out_specs=pl.BlockSpec((bq, ATTN_WIDTH), lambda i: (i, 0)),
        out_shape=jax.ShapeDtypeStruct((S, ATTN_WIDTH), BF16),
        scratch_shapes=[pltpu.VMEM((n_blk, bq, bq), I32),
                        pltpu.VMEM((N_ATTN_HEADS, bq), F32),
                        pltpu.VMEM((N_ATTN_HEADS, bq), F32),
                        pltpu.VMEM((N_ATTN_HEADS, ATTN_HEAD_DIM, bq), F32)],
        compiler_params=_params("arbitrary"),
        name="dsa_attention",
    )(qt, wt, qt, ki_b, kv, kv, bias)


def _sgu_kernel(gu_ref, gv_ref, g_ref, b_ref, ws_ref, bs_ref, o_ref, *, n_chunk):
    u = _gelu(gu_ref[...])
    v = _layer_norm(_gelu(gv_ref[...]), g_ref[...], b_ref[...]).astype(BF16)
    r = lax.broadcasted_iota(I32, (GMLP_CHUNK, GMLP_CHUNK), 0)
    c = lax.broadcasted_iota(I32, (GMLP_CHUNK, GMLP_CHUNK), 1)
    for g in range(GMLP_GROUPS):
        wg = jnp.where(c <= r, ws_ref[g], 0.0).astype(BF16)
        bcol = bs_ref[:, g:g + 1]
        cols = slice(g * LANES, (g + 1) * LANES)
        for n in range(n_chunk):
            rws = slice(n * GMLP_CHUNK, (n + 1) * GMLP_CHUNK)
            mixed = jnp.dot(wg, v[rws, cols], preferred_element_type=F32) + bcol
            o_ref[rws, cols] = (u[rws, cols] * mixed).astype(o_ref.dtype)


def _sgu(zg, ln_g, ln_b, w_s, b_s, tm):
    S = zg.shape[0]
    kernel = functools.partial(_sgu_kernel, n_chunk=tm // GMLP_CHUNK)
    return pl.pallas_call(
        kernel,
        grid=(S // tm,),
        in_specs=[pl.BlockSpec((tm, GMLP_WIDTH), lambda i: (i, 0)),
                  pl.BlockSpec((tm, GMLP_WIDTH), lambda i: (i, 1)),
                  pl.BlockSpec((1, GMLP_WIDTH), lambda i: (0, 0)),
                  pl.BlockSpec((1, GMLP_WIDTH), lambda i: (0, 0)),
                  pl.BlockSpec((GMLP_GROUPS, GMLP_CHUNK, GMLP_CHUNK), lambda i: (0, 0, 0)),
                  pl.BlockSpec((GMLP_CHUNK, GMLP_GROUPS), lambda i: (0, 0))],
        out_specs=pl.BlockSpec((tm, GMLP_WIDTH), lambda i: (i, 0)),
        out_shape=jax.ShapeDtypeStruct((S, GMLP_WIDTH), BF16),
        compiler_params=_params("parallel"),
        name="gmlp_sgu",
    )(zg, zg, ln_g.reshape(1, -1), ln_b.reshape(1, -1), w_s, b_s.T)


def _merge_kernel(at_ref, gm_ref, ga_ref, gg_ref, h_ref, wa_ref, wg_ref, wo_ref, g_ref, b_ref,
                  h1_ref, h1b_ref, h1bt_ref):
    a1 = jnp.dot(at_ref[...], wa_ref[...], preferred_element_type=F32)
    a2 = jnp.dot(gm_ref[...], wg_ref[...], preferred_element_type=F32)
    merged = jax.nn.sigmoid(ga_ref[...]) * a1 + jax.nn.sigmoid(gg_ref[...]) * a2
    y = DEEPNORM_ALPHA * h_ref[...] + jnp.dot(merged.astype(BF16), wo_ref[...], preferred_element_type=F32)
    h1 = _layer_norm(y, g_ref[...], b_ref[...])
    h1_ref[...] = h1
    h1b_ref[...] = h1.astype(BF16)
    h1bt_ref[...] = h1.T.astype(BF16)


def _merge(attn, gm, zg, h, wa, wg, wo, ln_g, ln_b, tm):
    S, D = h.shape
    W = attn.shape[1]
    return pl.pallas_call(
        _merge_kernel,
        grid=(S // tm,),
        in_specs=[pl.BlockSpec((tm, W), lambda i: (i, 0)),
                  pl.BlockSpec((tm, W), lambda i: (i, 0)),
                  pl.BlockSpec((tm, D), lambda i: (i, 1)),
                  pl.BlockSpec((tm, D), lambda i: (i, 2)),
                  pl.BlockSpec((tm, D), lambda i: (i, 0)),
                  _resident((W, D), lambda i: (0, 0)),
                  _resident((W, D), lambda i: (0, 0)),
                  _resident((D, D), lambda i: (0, 0)),
                  pl.BlockSpec((1, D), lambda i: (0, 0)),
                  pl.BlockSpec((1, D), lambda i: (0, 0))],
        out_specs=[pl.BlockSpec((tm, D), lambda i: (i, 0)),
                   pl.BlockSpec((tm, D), lambda i: (i, 0)),
                   pl.BlockSpec((D, tm), lambda i: (0, i))],
        out_shape=[jax.ShapeDtypeStruct((S, D), F32), jax.ShapeDtypeStruct((S, D), BF16),
                   jax.ShapeDtypeStruct((D, S), BF16)],
        compiler_params=_params("parallel"),
        name="merge_ln1",
    )(attn, gm, zg, zg, h, wa, wg, wo, ln_g.reshape(1, D), ln_b.reshape(1, D))


def _peer_pairs():
    pairs = [(a, b) for a in range(PEER_TOPK) for b in range(PEER_TOPK) if (a + 1) * (b + 1) <= PEER_TOPK]
    return sorted(pairs, key=lambda ab: ab[0] * PEER_TOPK + ab[1])


def _extract_top(work, n_rounds, on_pick):
    n = work.shape[0]
    iota = lax.broadcasted_iota(I32, work.shape, 0)
    for r in range(n_rounds):
        mx = jnp.max(work, axis=0, keepdims=True)
        first = jnp.min(jnp.where(work == mx, iota, n), axis=0, keepdims=True)
        pick = iota == first
        on_pick(r, mx, pick)
        work = jnp.where(pick, -jnp.inf, work)


def _twice_bf16(x):
    hi = lax.bitcast_convert_type(x.astype(BF16).astype(F32), I32) & jnp.int32(-65536)
    return hi | lax.shift_right_logical(hi, jnp.int32(16))


def _route_kernel(hb_ref, wqt_ref, sk_ref, l1_ref, a_ref, r2_ref, b_ref, *, T):
    qt = lax.dot_general(wqt_ref[...], hb_ref[...], NT_DIMS, preferred_element_type=F32)
    pairs = _peer_pairs()
    for h in range(PEER_HEADS):
        s, rank, vals = [], [], []
        for p in range(2):
            idx = 2 * h + p
            qhp = qt[idx * LANES:(idx + 1) * LANES, :].astype(BF16)
            sp = jnp.dot(sk_ref[idx], qhp, preferred_element_type=F32)
            state = {"rank": jnp.full(sp.shape, float(PEER_TOPK), F32), "vals": []}

            def on_pick(r, mx, pick, state=state):
                state["rank"] = jnp.where(pick, float(r), state["rank"])
                state["vals"].append(mx)

            _extract_top(sp, PEER_TOPK, on_pick)
            s.append(sp)
            rank.append(state["rank"])
            vals.append(state["vals"])
        cand = jnp.concatenate([vals[0][a] + vals[1][b] for a, b in pairs], axis=0)
        picked = {"mask": jnp.zeros(cand.shape, jnp.bool_)}

        def on_pick_c(r, mx, pick, picked=picked):
            picked["mask"] = picked["mask"] | pick

        _extract_top(cand, PEER_TOPK, on_pick_c)
        top = vals[0][0] + vals[1][0]
        selw = jnp.where(picked["mask"], jnp.exp(cand - top), 0.0)
        z = jnp.sum(selw, axis=0, keepdims=True)
        self32 = jnp.where(picked["mask"], 1.0, 0.0)
        l1 = jnp.zeros(s[0].shape, F32)
        pos = 0
        for a in range(PEER_TOPK):
            cnt = sum(1 for ab in pairs if ab[0] == a)
            n_a = jnp.sum(self32[pos:pos + cnt, :], axis=0, keepdims=True)
            pos += cnt
            l1 = jnp.where(rank[0] == float(a), n_a, l1)
        l1_ref[h] = _twice_bf16(l1)
        a_ref[h] = _twice_bf16(jnp.exp(s[0] - vals[0][0]))
        r2_ref[h] = rank[1].astype(r2_ref.dtype)
        b_ref[h] = (jnp.exp(s[1] - vals[1][0]) / z).astype(b_ref.dtype)


def _peer_route(h1b, wqt, sk, T):
    S, D = h1b.shape
    kernel = functools.partial(_route_kernel, T=T)
    shp = jax.ShapeDtypeStruct((PEER_HEADS, PEER_N_KEYS, S), I32)
    shp_b = jax.ShapeDtypeStruct((PEER_HEADS, PEER_N_KEYS, S), BF16)
    spec = pl.BlockSpec((PEER_HEADS, PEER_N_KEYS, T), lambda t: (0, 0, t))
    return pl.pallas_call(
        kernel,
        grid=(S // T,),
        in_specs=[pl.BlockSpec((T, D), lambda t: (t, 0)),
                  _resident(wqt.shape, lambda t: (0, 0)),
                  _resident(sk.shape, lambda t: (0, 0, 0))],
        out_specs=[spec, spec, spec, spec],
        out_shape=[shp, shp, shp_b, shp_b],
        compiler_params=_params("parallel"),
        name="peer_route",
    )(h1b, wqt, sk)


def _peer_kernel(ht_ref, u_ref, v_ref, l1_ref, a_ref, r2_ref, b_ref, o_ref, *, n_i1, tq):
    @pl.when(pl.program_id(1) == 0)
    def _():
        o_ref[...] = jnp.zeros_like(o_ref)

    n_sub = ht_ref.shape[1] // tq

    def hidden(q):
        return jnp.dot(u_ref[...], ht_ref[:, q * tq:(q + 1) * tq], preferred_element_type=F32)

    def gated(act, q):
        qs = slice(q * tq, (q + 1) * tq)
        r2 = [r2_ref[h, :, qs] for h in range(PEER_HEADS)]
        b = [b_ref[h, :, qs] for h in range(PEER_HEADS)]
        tiles = []
        for j in range(n_i1):
            gate = [None] * (PEER_N_KEYS // PACKED_ROWS)
            for h in range(PEER_HEADS):
                l1 = pltpu.bitcast(jnp.broadcast_to(l1_ref[h, j:j + 1, qs], (SUBLANES, tq)), BF16)
                a = pltpu.bitcast(jnp.broadcast_to(a_ref[h, j:j + 1, qs], (SUBLANES, tq)), BF16)
                for s in range(len(gate)):
                    slab = slice(s * PACKED_ROWS, (s + 1) * PACKED_ROWS)
                    val = jnp.where(r2[h][slab] < l1, a * b[h][slab], 0.0)
                    gate[s] = val if gate[s] is None else gate[s] + val
            g = _gelu(act[j * LANES:(j + 1) * LANES, :]).astype(BF16)
            tiles.extend(g[s * PACKED_ROWS:(s + 1) * PACKED_ROWS] * gate[s] for s in range(len(gate)))
        return jnp.concatenate(tiles, axis=0)

    act_next = hidden(0)
    for q in range(n_sub):
        act = act_next
        if q + 1 < n_sub:
            act_next = hidden(q + 1)
        pt = gated(act, q)
        o_ref[q * tq:(q + 1) * tq, :] += lax.dot_general(pt, v_ref[...], TN_DIMS,
                                                         preferred_element_type=F32)


def _peer_experts(h1bt, u_b, v_b, l1, a, r2, b, T, eb):
    D, S = h1bt.shape
    E = u_b.shape[0]
    n_i1 = eb // PEER_N_KEYS
    kernel = functools.partial(_peer_kernel, n_i1=n_i1, tq=min(256, T))
    row_spec = pl.BlockSpec((PEER_HEADS, n_i1, T), lambda t, e: (0, e, t))
    full_spec = pl.BlockSpec((PEER_HEADS, PEER_N_KEYS, T), lambda t, e: (0, 0, t))
    return pl.pallas_call(
        kernel,
        grid=(S // T, E // eb),
        in_specs=[pl.BlockSpec((D, T), lambda t, e: (0, t)),
                  pl.BlockSpec((eb, D), lambda t, e: (e, 0)),
                  pl.BlockSpec((eb, D), lambda t, e: (e, 0)),
                  row_spec, row_spec, full_spec, full_spec],
        out_specs=pl.BlockSpec((T, D), lambda t, e: (t, 0)),
        out_shape=jax.ShapeDtypeStruct((S, D), F32),
        compiler_params=_params("parallel", "arbitrary"),
        name="peer_experts",
    )(h1bt, u_b, v_b, l1, a, r2, b)


def _ple_kernel(h1_ref, peer_ref, p_ref, wg_ref, wp_ref, g_ref, b_ref, o_ref):
    r = DEEPNORM_ALPHA * h1_ref[...] + peer_ref[...]
    gate = jax.nn.sigmoid(jnp.dot(r.astype(BF16), wg_ref[...], preferred_element_type=F32))
    proj = jnp.dot(p_ref[...], wp_ref[...], preferred_element_type=F32)
    o_ref[...] = _layer_norm(r + gate * proj, g_ref[...], b_ref[...])


def _ple(h1, peer, p_b, wg, wp, ln_g, ln_b, tm):
    S, D = h1.shape
    P = p_b.shape[1]
    return pl.pallas_call(
        _ple_kernel,
        grid=(S // tm,),
        in_specs=[pl.BlockSpec((tm, D), lambda i: (i, 0)),
                  pl.BlockSpec((tm, D), lambda i: (i, 0)),
                  pl.BlockSpec((tm, P), lambda i: (i, 0)),
                  _resident((D, D), lambda i: (0, 0)),
                  _resident((P, D), lambda i: (0, 0)),
                  pl.BlockSpec((1, D), lambda i: (0, 0)),
                  pl.BlockSpec((1, D), lambda i: (0, 0))],
        out_specs=pl.BlockSpec((tm, D), lambda i: (i, 0)),
        out_shape=jax.ShapeDtypeStruct((S, D), F32),
        compiler_params=_params("parallel"),
        name="ple_ln2",
    )(h1, peer, p_b, wg, wp, ln_g.reshape(1, D), ln_b.reshape(1, D))


def _layer(x, p, ln_emb_g, ln_emb_b, rel_bias, w_in, gmlp_ln_g, gmlp_ln_b, gmlp_w_s, gmlp_b_s,
           w_br_attn, w_br_gmlp, w_mix_out, ln1_g, ln1_b, peer_w_q, peer_sub_keys, peer_u, peer_v,
           ple_w_proj, ple_w_gate, ln2_g, ln2_b):
    S, D = x.shape
    tm = min(512, S)
    o_k, o_v, o_qi = ATTN_WIDTH, 2 * ATTN_WIDTH, 3 * ATTN_WIDTH
    o_ki = o_qi + N_IDX_HEADS * IDX_HEAD_DIM
    o_gu = o_ki + IDX_HEAD_DIM + N_IDX_HEADS

    h, hb = _ln_embed(x, ln_emb_g, ln_emb_b, tm)
    wqt = jnp.concatenate([w_in[:, :o_k], w_in[:, o_qi:o_ki]], axis=1).T.astype(BF16)
    w1 = w_in[:, o_k:o_qi].astype(BF16)
    w2 = jnp.pad(w_in[:, o_ki:o_gu], ((0, 0), (0, LANES - (o_gu - o_ki)))).astype(BF16)
    w3 = w_in[:, o_gu:].astype(BF16)
    qt = _matmul_t(wqt, hb, BF16, tm, "in_proj_q")
    kv = _matmul(hb, w1, BF16, min(1024, S), 512, "in_proj_kv")
    kw = _matmul(hb, w2, F32, min(1024, S), LANES, "in_proj_index")
    zg = _matmul(hb, w3, F32, min(1024, S), 512, "in_proj_gmlp")

    attn = _dsa_attention(qt, kv, kw[:, :IDX_HEAD_DIM].astype(BF16),
                          kw[:, IDX_HEAD_DIM:IDX_HEAD_DIM + N_IDX_HEADS].T, rel_bias, bq=256)
    gm = _sgu(zg, gmlp_ln_g, gmlp_ln_b, gmlp_w_s, gmlp_b_s, tm)
    h1, h1b, h1bt = _merge(attn, gm, zg, h, w_br_attn.astype(BF16), w_br_gmlp.astype(BF16),
                           w_mix_out.astype(BF16), ln1_g, ln1_b, min(256, S))

    sk = peer_sub_keys.reshape(PEER_HEADS * 2, PEER_N_KEYS, -1).astype(BF16)
    l1, a, r2, b = _peer_route(h1b, peer_w_q.T.astype(BF16), sk, min(256, S))
    peer = _peer_experts(h1bt, peer_u.astype(BF16), peer_v.astype(BF16), l1, a, r2, b,
                         T=min(512, S), eb=1024)
    return _ple(h1, peer, p.astype(BF16), ple_w_gate.astype(BF16), ple_w_proj.astype(BF16),
                ln2_g, ln2_b, min(256, S))


def kernel(x, p, positions, ln_emb_g, ln_emb_b, rel_bias, w_in, gmlp_ln_g, gmlp_ln_b, gmlp_w_s, gmlp_b_s, w_br_attn, w_br_gmlp, w_mix_out, ln1_g, ln1_b, peer_w_q, peer_sub_keys, peer_u, peer_v, ple_w_proj, ple_w_gate, ln2_g, ln2_b):
    del positions
    assert x.shape[0] == 1 and w_in.shape[0] == DEPTH
    out = _layer(x[0], p[0, 0], ln_emb_g, ln_emb_b, rel_bias, w_in[0], gmlp_ln_g[0], gmlp_ln_b[0],
                 gmlp_w_s[0], gmlp_b_s[0], w_br_attn[0], w_br_gmlp[0], w_mix_out[0], ln1_g[0],
                 ln1_b[0], peer_w_q[0], peer_sub_keys[0], peer_u[0], peer_v[0], ple_w_proj[0],
                 ple_w_gate[0], ln2_g[0], ln2_b[0])
    return out[None]
```

```python
import functools
import math

import jax
import jax.numpy as jnp
from jax import lax
from jax.experimental import pallas as pl
from jax.experimental.pallas import tpu as pltpu

F32 = jnp.float32
BF16 = jnp.bfloat16
I32 = jnp.int32

LN_EPS = 1e-5
DEPTH = 1
DEEPNORM_ALPHA = (2.0 * DEPTH) ** 0.25

N_ATTN_HEADS = 8
ATTN_HEAD_DIM = 128
ATTN_WIDTH = N_ATTN_HEADS * ATTN_HEAD_DIM
N_IDX_HEADS = 16
IDX_HEAD_DIM = 64
TOPK_MAX = 256
GMLP_CHUNK = 128
GMLP_GROUPS = 8
GMLP_WIDTH = 1024
N_REL_BUCKETS = 32
REL_MAX_DISTANCE = 128
PEER_HEADS = 8
PEER_N_KEYS = 128
PEER_TOPK = 16

LANES = 128
SUBLANES = 8
PACKED_ROWS = 2 * SUBLANES
NEG = -1e30
INT_MIN = -(2 ** 31)
NT_DIMS = (((1,), (1,)), ((), ()))
TN_DIMS = (((0,), (0,)), ((), ()))
VMEM_LIMIT = 56 * 1024 * 1024


def _params(*sem):
    return pltpu.CompilerParams(dimension_semantics=sem, vmem_limit_bytes=VMEM_LIMIT)


def _resident(shape, index_map):
    return pl.BlockSpec(shape, index_map, pipeline_mode=pl.Buffered(1))


def _gelu(x):
    return 0.5 * x * (1.0 + jnp.tanh(0.7978845608028654 * (x + 0.044715 * (x * x * x))))


def _layer_norm(x, g, b):
    mu = jnp.mean(x, axis=-1, keepdims=True)
    xc = x - mu
    var = jnp.mean(xc * xc, axis=-1, keepdims=True)
    return xc * lax.rsqrt(var + LN_EPS) * g + b


def _ln_kernel(x_ref, g_ref, b_ref, h_ref, hb_ref):
    y = _layer_norm(x_ref[...], g_ref[...], b_ref[...])
    h_ref[...] = y
    hb_ref[...] = y.astype(BF16)


def _ln_embed(x, g, b, tm):
    S, D = x.shape
    return pl.pallas_call(
        _ln_kernel,
        grid=(S // tm,),
        in_specs=[pl.BlockSpec((tm, D), lambda i: (i, 0)),
                  pl.BlockSpec((1, D), lambda i: (0, 0)),
                  pl.BlockSpec((1, D), lambda i: (0, 0))],
        out_specs=[pl.BlockSpec((tm, D), lambda i: (i, 0)),
                   pl.BlockSpec((tm, D), lambda i: (i, 0))],
        out_shape=[jax.ShapeDtypeStruct((S, D), F32), jax.ShapeDtypeStruct((S, D), BF16)],
        compiler_params=_params("parallel"),
        name="ln_embed",
    )(x, g.reshape(1, D), b.reshape(1, D))


def _mm_kernel(a_ref, w_ref, o_ref):
    o_ref[...] = jnp.dot(a_ref[...], w_ref[...], preferred_element_type=F32).astype(o_ref.dtype)


def _matmul(a, w, out_dtype, tm, tn, name):
    M, K = a.shape
    N = w.shape[1]
    return pl.pallas_call(
        _mm_kernel,
        grid=(M // tm, N // tn),
        in_specs=[pl.BlockSpec((tm, K), lambda i, j: (i, 0)),
                  pl.BlockSpec((K, tn), lambda i, j: (0, j))],
        out_specs=pl.BlockSpec((tm, tn), lambda i, j: (i, j)),
        out_shape=jax.ShapeDtypeStruct((M, N), out_dtype),
        compiler_params=_params("parallel", "arbitrary"),
        name=name,
    )(a, w)


def _mm_t_kernel(wt_ref, a_ref, o_ref):
    o_ref[...] = lax.dot_general(wt_ref[...], a_ref[...], NT_DIMS,
                                 preferred_element_type=F32).astype(o_ref.dtype)


def _matmul_t(wt, a, out_dtype, tm, name):
    M, K = a.shape
    N = wt.shape[0]
    return pl.pallas_call(
        _mm_t_kernel,
        grid=(M // tm,),
        in_specs=[_resident((N, K), lambda i: (0, 0)),
                  pl.BlockSpec((tm, K), lambda i: (i, 0))],
        out_specs=pl.BlockSpec((N, tm), lambda i: (0, i)),
        out_shape=jax.ShapeDtypeStruct((N, M), out_dtype),
        compiler_params=_params("parallel"),
        name=name,
    )(wt, a)


def _sortable_key(x):
    b = lax.bitcast_convert_type(x, I32)
    return b ^ ((b >> 31) & 0x7FFFFFFF)


def _reduce_row_groups(t, op):
    parts = [t[j * SUBLANES:(j + 1) * SUBLANES, :] for j in range(t.shape[0] // SUBLANES)]
    while len(parts) > 1:
        parts = [op(parts[j], parts[j + 1]) for j in range(0, len(parts) - 1, 2)] + (
            [parts[-1]] if len(parts) % 2 else [])
    return parts[0]


def _dsa_kernel(qit_ref, wt_ref, qt_ref, ki_ref, k_ref, v_ref, bias_ref, o_ref,
                sc_ref, m_ref, l_ref, acc_ref, *, bq, top_k):
    i = pl.program_id(0)
    w = wt_ref[...] * ((IDX_HEAD_DIM ** -0.5) * (N_IDX_HEADS ** -0.5))
    key_i = lax.broadcasted_iota(I32, (bq, bq), 0)
    qry_i = lax.broadcasted_iota(I32, (bq, bq), 1)
    causal = key_i <= qry_i

    def rows(c):
        return pl.ds(pl.multiple_of(c * bq, bq), bq)

    def scores(c):
        kic = ki_ref[rows(c), :]
        acc = jnp.zeros((bq, bq), F32)
        for h in range(N_IDX_HEADS):
            d = jnp.dot(kic, qit_ref[h * IDX_HEAD_DIM:(h + 1) * IDX_HEAD_DIM, :],
                        preferred_element_type=F32)
            acc = acc + w[h:h + 1, :] * jnp.maximum(d, 0.0)
        return acc

    def score_chunk(c, carry):
        sc_ref[c] = _sortable_key(scores(c))
        return carry

    lax.fori_loop(0, i, score_chunk, 0)
    sc_ref[i] = _sortable_key(jnp.where(causal, scores(i), -jnp.inf))

    def count_ge(cand):
        def body(c, cnt):
            return cnt + _reduce_row_groups(jnp.where(sc_ref[c] >= cand, 1.0, 0.0), jnp.add)
        cnt = lax.fori_loop(0, i + 1, body, jnp.zeros((SUBLANES, bq), F32))
        return jnp.sum(cnt, axis=0, keepdims=True)

    def search_bit(state):
        b, tu, settled = state
        cu = tu | lax.shift_left(jnp.int32(1), 31 - b)
        cnt = count_ge(cu ^ INT_MIN)
        tu = jnp.where((cnt >= float(top_k)) & (settled == 0.0), cu, tu)
        settled = jnp.where(cnt == float(top_k), 1.0, settled)
        return b + 1, tu, settled

    def unsettled(state):
        b, _, settled = state
        return (b < 32) & (jnp.min(settled) == 0.0)

    _, tu, _ = lax.while_loop(unsettled, search_bit,
                              (jnp.int32(0), jnp.zeros((1, bq), I32), jnp.zeros((1, bq), F32)))
    thresh = tu ^ INT_MIN

    def mask_chunk(c, carry):
        mb = jnp.where(sc_ref[c] >= thresh, 0.0, NEG)
        sc_ref[c] = lax.bitcast_convert_type(mb, I32)
        return carry

    lax.fori_loop(0, i, mask_chunk, 0)
    mb = jnp.where(causal, jnp.where(sc_ref[i] >= thresh, 0.0, NEG), NEG)
    sc_ref[i] = lax.bitcast_convert_type(mb, I32)

    scale = ATTN_HEAD_DIM ** -0.5
    m_ref[...] = jnp.full(m_ref.shape, NEG, F32)
    l_ref[...] = jnp.zeros(l_ref.shape, F32)
    acc_ref[...] = jnp.zeros(acc_ref.shape, F32)

    def attend(c, near):
        mbias = lax.bitcast_convert_type(sc_ref[c], F32)
        m_all, l_all = m_ref[...], l_ref[...]
        m_rows, l_rows = [], []

        def qk(h):
            cols = slice(h * ATTN_HEAD_DIM, (h + 1) * ATTN_HEAD_DIM)
            return jnp.dot(k_ref[rows(c), cols], qt_ref[cols, :], preferred_element_type=F32)

        s_next = qk(0)
        for h in range(N_ATTN_HEADS):
            cols = slice(h * ATTN_HEAD_DIM, (h + 1) * ATTN_HEAD_DIM)
            s = s_next * scale + mbias
            if h + 1 < N_ATTN_HEADS:
                s_next = qk(h + 1)
            if near is not None:
                s = s + bias_ref[h, near * bq:(near + 1) * bq, :]
            m_old = m_all[h:h + 1, :]
            m_new = jnp.maximum(m_old, jnp.max(_reduce_row_groups(s, jnp.maximum), axis=0, keepdims=True))
            p = jnp.exp(s - m_new)
            a = jnp.exp(m_old - m_new)
            m_rows.append(m_new)
            l_rows.append(a * l_all[h:h + 1, :]
                          + jnp.sum(_reduce_row_groups(p, jnp.add), axis=0, keepdims=True))
            pv = lax.dot_general(v_ref[rows(c), cols], p.astype(BF16), TN_DIMS, preferred_element_type=F32)
            acc_ref[h] = a * acc_ref[h] + pv
        m_ref[...] = jnp.concatenate(m_rows, axis=0)
        l_ref[...] = jnp.concatenate(l_rows, axis=0)

    def far_chunk(c, carry):
        attend(c, None)
        return carry

    lax.fori_loop(0, jnp.maximum(i - 1, 0), far_chunk, 0)

    @pl.when(i > 0)
    def _():
        attend(i - 1, 0)

    attend(i, 1)
    for h in range(N_ATTN_HEADS):
        out_t = acc_ref[h] / l_ref[h:h + 1, :]
        o_ref[:, h * ATTN_HEAD_DIM:(h + 1) * ATTN_HEAD_DIM] = out_t.T.astype(o_ref.dtype)


def _rel_bucket(dist):
    n = jnp.maximum(dist, 0)
    max_exact = N_REL_BUCKETS // 2
    nf = jnp.maximum(n, 1).astype(F32)
    large = max_exact + (jnp.log(nf / max_exact) / math.log(REL_MAX_DISTANCE / max_exact)
                         * (N_REL_BUCKETS - max_exact)).astype(I32)
    large = jnp.minimum(large, N_REL_BUCKETS - 1)
    return jnp.where(n < max_exact, n, large)


def _near_bias_tiles(rel_bias, bq):
    assert bq >= REL_MAX_DISTANCE
    key = jnp.arange(2 * bq, dtype=I32)[:, None]
    qry = jnp.arange(bq, dtype=I32)[None, :]
    bucket = _rel_bucket(qry + bq - key)
    rel = (rel_bias - rel_bias[N_REL_BUCKETS - 1]).T
    tiles = jnp.zeros((rel.shape[0],) + bucket.shape, F32)
    for b in range(N_REL_BUCKETS):
        tiles = jnp.where(bucket[None] == b, rel[:, b][:, None, None], tiles)
    return tiles


def _dsa_attention(qt, kv, ki_b, wt, rel_bias, bq):
    S = kv.shape[0]
    top_k = min(TOPK_MAX, S // 4)
    n_blk = S // bq
    bias = _near_bias_tiles(rel_bias, bq)
    kernel = functools.partial(_dsa_kernel, bq=bq, top_k=top_k)
    return pl.pallas_call(
        kernel,
        grid=(n_blk,),
        in_specs=[pl.BlockSpec((N_IDX_HEADS * IDX_HEAD_DIM, bq), lambda i: (1, i)),
                  pl.BlockSpec((N_IDX_HEADS, bq), lambda i: (0, i)),
                  pl.BlockSpec((ATTN_WIDTH, bq), lambda i: (0, i)),
                  _resident((S, IDX_HEAD_DIM), lambda i: (0, 0)),
                  _resident((S, ATTN_WIDTH), lambda i: (0, 0)),
                  _resident((S, ATTN_WIDTH), lambda i: (0, 1)),
                  _resident((N_ATTN_HEADS, 2 * bq, bq), lambda i: (0, 0, 0))],
        out_specs=pl.BlockSpec((bq, ATTN_WIDTH), lambda i: (i, 0)),
        out_shape=jax.ShapeDtypeStruct((S, ATTN_WIDTH), BF16),
        scratch_shapes=[pltpu.VMEM((n_blk, bq, bq), I32),
                        pltpu.VMEM((N_ATTN_HEADS, bq), F32),
                        pltpu.VMEM((N_ATTN_HEADS, bq), F32),
                        pltpu.VMEM((N_ATTN_HEADS, ATTN_HEAD_DIM, bq), F32)],
        compiler_params=_params("arbitrary"),
        name="dsa_attention",
    )(qt, wt, qt, ki_b, kv, kv, bias)


def _sgu_kernel(gu_ref, gv_ref, g_ref, b_ref, ws_ref, bs_ref, o_ref, *, n_chunk):
    u = _gelu(gu_ref[...])
    v = _layer_norm(_gelu(gv_ref[...]), g_ref[...], b_ref[...]).astype(BF16)
    r = lax.broadcasted_iota(I32, (GMLP_CHUNK, GMLP_CHUNK), 0)
    c = lax.broadcasted_iota(I32, (GMLP_CHUNK, GMLP_CHUNK), 1)
    for g in range(GMLP_GROUPS):
        wg = jnp.where(c <= r, ws_ref[g], 0.0).astype(BF16)
        bcol = bs_ref[:, g:g + 1]
        cols = slice(g * LANES, (g + 1) * LANES)
        for n in range(n_chunk):
            rws = slice(n * GMLP_CHUNK, (n + 1) * GMLP_CHUNK)
            mixed = jnp.dot(wg, v[rws, cols], preferred_element_type=F32) + bcol
            o_ref[rws, cols] = (u[rws, cols] * mixed).astype(o_ref.dtype)


def _sgu(zg, ln_g, ln_b, w_s, b_s, tm):
    S = zg.shape[0]
    kernel = functools.partial(_sgu_kernel, n_chunk=tm // GMLP_CHUNK)
    return pl.pallas_call(
        kernel,
        grid=(S // tm,),
        in_specs=[pl.BlockSpec((tm, GMLP_WIDTH), lambda i: (i, 0)),
                  pl.BlockSpec((tm, GMLP_WIDTH), lambda i: (i, 1)),
                  pl.BlockSpec((1, GMLP_WIDTH), lambda i: (0, 0)),
                  pl.BlockSpec((1, GMLP_WIDTH), lambda i: (0, 0)),
                  pl.BlockSpec((GMLP_GROUPS, GMLP_CHUNK, GMLP_CHUNK), lambda i: (0, 0, 0)),
                  pl.BlockSpec((GMLP_CHUNK, GMLP_GROUPS), lambda i: (0, 0))],
        out_specs=pl.BlockSpec((tm, GMLP_WIDTH), lambda i: (i, 0)),
        out_shape=jax.ShapeDtypeStruct((S, GMLP_WIDTH), BF16),
        compiler_params=_params("parallel"),
        name="gmlp_sgu",
    )(zg, zg, ln_g.reshape(1, -1), ln_b.reshape(1, -1), w_s, b_s.T)


def _merge_kernel(at_ref, gm_ref, ga_ref, gg_ref, h_ref, wa_ref, wg_ref, wo_ref, g_ref, b_ref,
                  h1_ref, h1b_ref, h1bt_ref):
    a1 = jnp.dot(at_ref[...], wa_ref[...], preferred_element_type=F32)
    a2 = jnp.dot(gm_ref[...], wg_ref[...], preferred_element_type=F32)
    merged = jax.nn.sigmoid(ga_ref[...]) * a1 + jax.nn.sigmoid(gg_ref[...]) * a2
    y = DEEPNORM_ALPHA * h_ref[...] + jnp.dot(merged.astype(BF16), wo_ref[...], preferred_element_type=F32)
    h1 = _layer_norm(y, g_ref[...], b_ref[...])
    h1_ref[...] = h1
    h1b_ref[...] = h1.astype(BF16)
    h1bt_ref[...] = h1.T.astype(BF16)


def _merge(attn, gm, zg, h, wa, wg, wo, ln_g, ln_b, tm):
    S, D = h.shape
    W = attn.shape[1]
    return pl.pallas_call(
        _merge_kernel,
        grid=(S // tm,),
        in_specs=[pl.BlockSpec((tm, W), lambda i: (i, 0)),
                  pl.BlockSpec((tm, W), lambda i: (i, 0)),
                  pl.BlockSpec((tm, D), lambda i: (i, 1)),
                  pl.BlockSpec((tm, D), lambda i: (i, 2)),
                  pl.BlockSpec((tm, D), lambda i: (i, 0)),
                  _resident((W, D), lambda i: (0, 0)),
                  _resident((W, D), lambda i: (0, 0)),
                  _resident((D, D), lambda i: (0, 0)),
                  pl.BlockSpec((1, D), lambda i: (0, 0)),
                  pl.BlockSpec((1, D), lambda i: (0, 0))],
        out_specs=[pl.BlockSpec((tm, D), lambda i: (i, 0)),
                   pl.BlockSpec((tm, D), lambda i: (i, 0)),
                   pl.BlockSpec((D, tm), lambda i: (0, i))],
        out_shape=[jax.ShapeDtypeStruct((S, D), F32), jax.ShapeDtypeStruct((S, D), BF16),
                   jax.ShapeDtypeStruct((D, S), BF16)],
        compiler_params=_params("parallel"),
        name="merge_ln1",
    )(attn, gm, zg, zg, h, wa, wg, wo, ln_g.reshape(1, D), ln_b.reshape(1, D))


def _peer_pairs():
    pairs = [(a, b) for a in range(PEER_TOPK) for b in range(PEER_TOPK) if (a + 1) * (b + 1) <= PEER_TOPK]
    return sorted(pairs, key=lambda ab: ab[0] * PEER_TOPK + ab[1])


def _extract_top(work, n_rounds, on_pick, exact):
    n = float(work.shape[0])
    iota = lax.broadcasted_iota(I32, work.shape, 0).astype(F32)
    for r in range(n_rounds):
        mx = jnp.max(work, axis=0, keepdims=True)
        pick = work == mx
        if exact:
            first = jnp.min(jnp.where(pick, iota, n), axis=0, keepdims=True)
            pick = iota == first
        on_pick(r, mx, pick)
        work = jnp.where(pick, -jnp.inf, work)


def _twice_bf16(x):
    hi = lax.bitcast_convert_type(x.astype(BF16).astype(F32), I32) & jnp.int32(-65536)
    return hi | lax.shift_right_logical(hi, jnp.int32(16))


def _route_kernel(hb_ref, wqt_ref, sk_ref, l1_ref, a_ref, r2_ref, b_ref, *, T):
    qt = lax.dot_general(wqt_ref[...], hb_ref[...], NT_DIMS, preferred_element_type=F32)
    pairs = _peer_pairs()
    scores = [jnp.dot(sk_ref[idx], qt[idx * LANES:(idx + 1) * LANES, :].astype(BF16),
                      preferred_element_type=F32) for idx in range(2 * PEER_HEADS)]

    def route(exact):
        miscount = jnp.zeros((1, T), F32)
        for h in range(PEER_HEADS):
            s, rank, vals = [], [], []
            for p in range(2):
                sp = scores[2 * h + p]
                state = {"rank": jnp.full(sp.shape, float(PEER_TOPK), F32), "vals": []}

                def on_pick(r, mx, pick, state=state):
                    state["rank"] = jnp.where(pick, float(r), state["rank"])
                    state["vals"].append(mx)

                _extract_top(sp, PEER_TOPK, on_pick, exact)
                n_ranked = jnp.sum(jnp.where(state["rank"] < float(PEER_TOPK), 1.0, 0.0), axis=0, keepdims=True)
                miscount = jnp.maximum(miscount, jnp.abs(n_ranked - float(PEER_TOPK)))
                s.append(sp)
                rank.append(state["rank"])
                vals.append(state["vals"])
            cand = jnp.concatenate([vals[0][a] + vals[1][b] for a, b in pairs], axis=0)
            picked = {"mask": jnp.zeros(cand.shape, jnp.bool_)}

            def on_pick_c(r, mx, pick, picked=picked):
                picked["mask"] = picked["mask"] | pick

            _extract_top(cand, PEER_TOPK, on_pick_c, exact)
            top = vals[0][0] + vals[1][0]
            selw = jnp.where(picked["mask"], jnp.exp(cand - top), 0.0)
            z = jnp.sum(selw, axis=0, keepdims=True)
            self32 = jnp.where(picked["mask"], 1.0, 0.0)
            miscount = jnp.maximum(miscount, jnp.abs(jnp.sum(self32, axis=0, keepdims=True) - float(PEER_TOPK)))
            l1 = jnp.zeros(s[0].shape, F32)
            pos = 0
            for a in range(PEER_TOPK):
                cnt = sum(1 for ab in pairs if ab[0] == a)
                n_a = jnp.sum(self32[pos:pos + cnt, :], axis=0, keepdims=True)
                pos += cnt
                l1 = jnp.where(rank[0] == float(a), n_a, l1)
            l1_ref[h] = _twice_bf16(l1)
            a_ref[h] = _twice_bf16(jnp.exp(s[0] - vals[0][0]))
            r2_ref[h] = rank[1].astype(r2_ref.dtype)
            b_ref[h] = (jnp.exp(s[1] - vals[1][0]) / z).astype(b_ref.dtype)
        return miscount

    miscount = route(exact=False)

    @pl.when(jnp.max(miscount) > 0.0)
    def _():
        route(exact=True)


def _peer_route(h1b, wqt, sk, T):
    S, D = h1b.shape
    kernel = functools.partial(_route_kernel, T=T)
    shp = jax.ShapeDtypeStruct((PEER_HEADS, PEER_N_KEYS, S), I32)
    shp_b = jax.ShapeDtypeStruct((PEER_HEADS, PEER_N_KEYS, S), BF16)
    spec = pl.BlockSpec((PEER_HEADS, PEER_N_KEYS, T), lambda t: (0, 0, t))
    return pl.pallas_call(
        kernel,
        grid=(S // T,),
        in_specs=[pl.BlockSpec((T, D), lambda t: (t, 0)),
                  _resident(wqt.shape, lambda t: (0, 0)),
                  _resident(sk.shape, lambda t: (0, 0, 0))],
        out_specs=[spec, spec, spec, spec],
        out_shape=[shp, shp, shp_b, shp_b],
        compiler_params=_params("parallel"),
        name="peer_route",
    )(h1b, wqt, sk)


def _peer_kernel(ht_ref, u_ref, v_ref, l1_ref, a_ref, r2_ref, b_ref, o_ref, *, n_i1, tq):
    @pl.when(pl.program_id(1) == 0)
    def _():
        o_ref[...] = jnp.zeros_like(o_ref)

    n_sub = ht_ref.shape[1] // tq
    n_slab = PEER_N_KEYS // PACKED_ROWS

    def gates(q):
        qs = slice(q * tq, (q + 1) * tq)
        r2 = [r2_ref[h, :, qs] for h in range(PEER_HEADS)]
        b = [b_ref[h, :, qs] for h in range(PEER_HEADS)]
        tiles = []
        for j in range(n_i1):
            gate = [None] * n_slab
            for h in range(PEER_HEADS):
                l1 = pltpu.bitcast(jnp.broadcast_to(l1_ref[h, j:j + 1, qs], (SUBLANES, tq)), BF16)
                a = pltpu.bitcast(jnp.broadcast_to(a_ref[h, j:j + 1, qs], (SUBLANES, tq)), BF16)
                for s in range(n_slab):
                    slab = slice(s * PACKED_ROWS, (s + 1) * PACKED_ROWS)
                    val = jnp.where(r2[h][slab] < l1, a * b[h][slab], 0.0)
                    gate[s] = val if gate[s] is None else gate[s] + val
            tiles.extend(gate)
        return tiles

    def hidden(q):
        return jnp.dot(u_ref[...], ht_ref[:, q * tq:(q + 1) * tq], preferred_element_type=F32)

    g, act = [], []
    for q in range(n_sub):
        g.append(gates(q))
        act.append(hidden(q))
    for q in range(n_sub):
        gel = _gelu(act[q]).astype(BF16)
        pt = jnp.concatenate([gel[k * PACKED_ROWS:(k + 1) * PACKED_ROWS] * g[q][k]
                              for k in range(len(g[q]))], axis=0)
        o_ref[q * tq:(q + 1) * tq, :] += lax.dot_general(pt, v_ref[...], TN_DIMS,
                                                         preferred_element_type=F32)


def _peer_experts(h1bt, u_b, v_b, l1, a, r2, b, T, eb):
    D, S = h1bt.shape
    E = u_b.shape[0]
    n_i1 = eb // PEER_N_KEYS
    kernel = functools.partial(_peer_kernel, n_i1=n_i1, tq=min(256, T))
    row_spec = pl.BlockSpec((PEER_HEADS, n_i1, T), lambda t, e: (0, e, t))
    full_spec = pl.BlockSpec((PEER_HEADS, PEER_N_KEYS, T), lambda t, e: (0, 0, t))
    return pl.pallas_call(
        kernel,
        grid=(S // T, E // eb),
        in_specs=[pl.BlockSpec((D, T), lambda t, e: (0, t)),
                  pl.BlockSpec((eb, D), lambda t, e: (e, 0)),
                  pl.BlockSpec((eb, D), lambda t, e: (e, 0)),
                  row_spec, row_spec, full_spec, full_spec],
        out_specs=pl.BlockSpec((T, D), lambda t, e: (t, 0)),
        out_shape=jax.ShapeDtypeStruct((S, D), F32),
        compiler_params=_params("parallel", "arbitrary"),
        name="peer_experts",
    )(h1bt, u_b, v_b, l1, a, r2, b)


def _ple_kernel(h1_ref, peer_ref, p_ref, wg_ref, wp_ref, g_ref, b_ref, o_ref):
    r = DEEPNORM_ALPHA * h1_ref[...] + peer_ref[...]
    gate = jax.nn.sigmoid(jnp.dot(r.astype(BF16), wg_ref[...], preferred_element_type=F32))
    proj = jnp.dot(p_ref[...], wp_ref[...], preferred_element_type=F32)
    o_ref[...] = _layer_norm(r + gate * proj, g_ref[...], b_ref[...])


def _ple(h1, peer, p_b, wg, wp, ln_g, ln_b, tm):
    S, D = h1.shape
    P = p_b.shape[1]
    return pl.pallas_call(
        _ple_kernel,
        grid=(S // tm,),
        in_specs=[pl.BlockSpec((tm, D), lambda i: (i, 0)),
                  pl.BlockSpec((tm, D), lambda i: (i, 0)),
                  pl.BlockSpec((tm, P), lambda i: (i, 0)),
                  _resident((D, D), lambda i: (0, 0)),
                  _resident((P, D), lambda i: (0, 0)),
                  pl.BlockSpec((1, D), lambda i: (0, 0)),
                  pl.BlockSpec((1, D), lambda i: (0, 0))],
        out_specs=pl.BlockSpec((tm, D), lambda i: (i, 0)),
        out_shape=jax.ShapeDtypeStruct((S, D), F32),
        compiler_params=_params("parallel"),
        name="ple_ln2",
    )(h1, peer, p_b, wg, wp, ln_g.reshape(1, D), ln_b.reshape(1, D))


def _layer(x, p, ln_emb_g, ln_emb_b, rel_bias, w_in, gmlp_ln_g, gmlp_ln_b, gmlp_w_s, gmlp_b_s,
           w_br_attn, w_br_gmlp, w_mix_out, ln1_g, ln1_b, peer_w_q, peer_sub_keys, peer_u, peer_v,
           ple_w_proj, ple_w_gate, ln2_g, ln2_b):
    S, D = x.shape
    tm = min(512, S)
    o_k, o_v, o_qi = ATTN_WIDTH, 2 * ATTN_WIDTH, 3 * ATTN_WIDTH
    o_ki = o_qi + N_IDX_HEADS * IDX_HEAD_DIM
    o_gu = o_ki + IDX_HEAD_DIM + N_IDX_HEADS

    h, hb = _ln_embed(x, ln_emb_g, ln_emb_b, tm)
    wqt = jnp.concatenate([w_in[:, :o_k], w_in[:, o_qi:o_ki]], axis=1).T.astype(BF16)
    w1 = w_in[:, o_k:o_qi].astype(BF16)
    w2 = jnp.pad(w_in[:, o_ki:o_gu], ((0, 0), (0, LANES - (o_gu - o_ki)))).astype(BF16)
    w3 = w_in[:, o_gu:].astype(BF16)
    qt = _matmul_t(wqt, hb, BF16, tm, "in_proj_q")
    kv = _matmul(hb, w1, BF16, min(1024, S), 512, "in_proj_kv")
    kw = _matmul(hb, w2, F32, min(1024, S), LANES, "in_proj_index")
    zg = _matmul(hb, w3, F32, min(1024, S), 512, "in_proj_gmlp")

    attn = _dsa_attention(qt, kv, kw[:, :IDX_HEAD_DIM].astype(BF16),
                          kw[:, IDX_HEAD_DIM:IDX_HEAD_DIM + N_IDX_HEADS].T, rel_bias, bq=256)
    gm = _sgu(zg, gmlp_ln_g, gmlp_ln_b, gmlp_w_s, gmlp_b_s, tm)
    h1, h1b, h1bt = _merge(attn, gm, zg, h, w_br_attn.astype(BF16), w_br_gmlp.astype(BF16),
                           w_mix_out.astype(BF16), ln1_g, ln1_b, min(256, S))

    sk = peer_sub_keys.reshape(PEER_HEADS * 2, PEER_N_KEYS, -1).astype(BF16)
    l1, a, r2, b = _peer_route(h1b, peer_w_q.T.astype(BF16), sk, min(256, S))
    peer = _peer_experts(h1bt, peer_u.astype(BF16), peer_v.astype(BF16), l1, a, r2, b,
                         T=min(512, S), eb=1024)
    return _ple(h1, peer, p.astype(BF16), ple_w_gate.astype(BF16), ple_w_proj.astype(BF16),
                ln2_g, ln2_b, min(256, S))


def kernel(x, p, positions, ln_emb_g, ln_emb_b, rel_bias, w_in, gmlp_ln_g, gmlp_ln_b, gmlp_w_s, gmlp_b_s, w_br_attn, w_br_gmlp, w_mix_out, ln1_g, ln1_b, peer_w_q, peer_sub_keys, peer_u, peer_v, ple_w_proj, ple_w_gate, ln2_g, ln2_b):
    del positions
    assert x.shape[0] == 1 and w_in.shape[0] == DEPTH
    out = _layer(x[0], p[0, 0], ln_emb_g, ln_emb_b, rel_bias, w_in[0], gmlp_ln_g[0], gmlp_ln_b[0],
                 gmlp_w_s[0], gmlp_b_s[0], w_br_attn[0], w_br_gmlp[0], w_mix_out[0], ln1_g[0],
                 ln1_b[0], peer_w_q[0], peer_sub_keys[0], peer_u[0], peer_v[0], ple_w_proj[0],
                 ple_w_gate[0], ln2_g[0], ln2_b[0])
    return out[None]
```

```python
import functools
import math

import jax
import jax.numpy as jnp
from jax import lax
from jax.experimental import pallas as pl
from jax.experimental.pallas import tpu as pltpu

F32 = jnp.float32
BF16 = jnp.bfloat16
I32 = jnp.int32

LN_EPS = 1e-5
DEPTH = 1
DEEPNORM_ALPHA = (2.0 * DEPTH) ** 0.25

N_ATTN_HEADS = 8
ATTN_HEAD_DIM = 128
ATTN_WIDTH = N_ATTN_HEADS * ATTN_HEAD_DIM
N_IDX_HEADS = 16
IDX_HEAD_DIM = 64
TOPK_MAX = 256
GMLP_CHUNK = 128
GMLP_GROUPS = 8
GMLP_WIDTH = 1024
N_REL_BUCKETS = 32
REL_MAX_DISTANCE = 128
PEER_HEADS = 8
PEER_N_KEYS = 128
PEER_TOPK = 16

LANES = 128
SUBLANES = 8
PACKED_ROWS = 2 * SUBLANES
MXU_DIM = 256
NEG = -1e30
SEARCH_STEPS_MAX = 64
NT_DIMS = (((1,), (1,)), ((), ()))
TN_DIMS = (((0,), (0,)), ((), ()))
VMEM_LIMIT = 56 * 1024 * 1024


def _params(*sem):
    return pltpu.CompilerParams(dimension_semantics=sem, vmem_limit_bytes=VMEM_LIMIT)


def _resident(shape, index_map):
    return pl.BlockSpec(shape, index_map, pipeline_mode=pl.Buffered(1))


def _gelu(x):
    return 0.5 * x * (1.0 + jnp.tanh(0.7978845608028654 * (x + 0.044715 * (x * x * x))))


def _layer_norm(x, g, b):
    mu = jnp.mean(x, axis=-1, keepdims=True)
    xc = x - mu
    var = jnp.mean(xc * xc, axis=-1, keepdims=True)
    return xc * lax.rsqrt(var + LN_EPS) * g + b


def _ln_kernel(x_ref, g_ref, b_ref, h_ref, hb_ref):
    y = _layer_norm(x_ref[...], g_ref[...], b_ref[...])
    h_ref[...] = y
    hb_ref[...] = y.astype(BF16)


def _ln_embed(x, g, b, tm):
    S, D = x.shape
    return pl.pallas_call(
        _ln_kernel,
        grid=(S // tm,),
        in_specs=[pl.BlockSpec((tm, D), lambda i: (i, 0)),
                  pl.BlockSpec((1, D), lambda i: (0, 0)),
                  pl.BlockSpec((1, D), lambda i: (0, 0))],
        out_specs=[pl.BlockSpec((tm, D), lambda i: (i, 0)),
                   pl.BlockSpec((tm, D), lambda i: (i, 0))],
        out_shape=[jax.ShapeDtypeStruct((S, D), F32), jax.ShapeDtypeStruct((S, D), BF16)],
        compiler_params=_params("parallel"),
        name="ln_embed",
    )(x, g.reshape(1, D), b.reshape(1, D))


def _mm_kernel(a_ref, w_ref, o_ref):
    o_ref[...] = jnp.dot(a_ref[...], w_ref[...], preferred_element_type=F32).astype(o_ref.dtype)


def _matmul(a, w, out_dtype, tm, tn, name):
    M, K = a.shape
    N = w.shape[1]
    return pl.pallas_call(
        _mm_kernel,
        grid=(M // tm, N // tn),
        in_specs=[pl.BlockSpec((tm, K), lambda i, j: (i, 0)),
                  pl.BlockSpec((K, tn), lambda i, j: (0, j))],
        out_specs=pl.BlockSpec((tm, tn), lambda i, j: (i, j)),
        out_shape=jax.ShapeDtypeStruct((M, N), out_dtype),
        compiler_params=_params("parallel", "arbitrary"),
        name=name,
    )(a, w)


def _mm_t_kernel(wt_ref, a_ref, o_ref):
    o_ref[...] = lax.dot_general(wt_ref[...], a_ref[...], NT_DIMS,
                                 preferred_element_type=F32).astype(o_ref.dtype)


def _matmul_t(wt, a, out_dtype, tm, name):
    M, K = a.shape
    N = wt.shape[0]
    return pl.pallas_call(
        _mm_t_kernel,
        grid=(M // tm,),
        in_specs=[_resident((N, K), lambda i: (0, 0)),
                  pl.BlockSpec((tm, K), lambda i: (i, 0))],
        out_specs=pl.BlockSpec((N, tm), lambda i: (0, i)),
        out_shape=jax.ShapeDtypeStruct((N, M), out_dtype),
        compiler_params=_params("parallel"),
        name=name,
    )(wt, a)


def _reduce_row_groups(t, op):
    parts = [t[j * SUBLANES:(j + 1) * SUBLANES, :] for j in range(t.shape[0] // SUBLANES)]
    while len(parts) > 1:
        parts = [op(parts[j], parts[j + 1]) for j in range(0, len(parts) - 1, 2)] + (
            [parts[-1]] if len(parts) % 2 else [])
    return parts[0]


def _dsa_kernel(qit_ref, wt_ref, qt_ref, ki_ref, k_ref, v_ref, bias_ref, o_ref,
                sc_ref, m_ref, l_ref, acc_ref, *, bq, top_k):
    i = pl.program_id(0)
    w = wt_ref[...] * ((IDX_HEAD_DIM ** -0.5) * (N_IDX_HEADS ** -0.5))
    key_i = lax.broadcasted_iota(I32, (bq, bq), 0)
    qry_i = lax.broadcasted_iota(I32, (bq, bq), 1)
    causal = key_i <= qry_i

    def rows(c):
        return pl.ds(pl.multiple_of(c * bq, bq), bq)

    def scores(c):
        kic = ki_ref[rows(c), :]
        acc = jnp.zeros((bq, bq), F32)
        for h in range(N_IDX_HEADS):
            d = jnp.dot(kic, qit_ref[h * IDX_HEAD_DIM:(h + 1) * IDX_HEAD_DIM, :],
                        preferred_element_type=F32)
            acc = acc + w[h:h + 1, :] * jnp.maximum(d, 0.0)
        return acc

    def score_chunk(c, carry):
        sc_ref[c] = scores(c)
        return carry

    lax.fori_loop(0, i, score_chunk, 0)
    sc_ref[i] = jnp.where(causal, scores(i), -jnp.inf)

    def lowest_highest(c, lh):
        x = sc_ref[c]
        return (jnp.minimum(lh[0], _reduce_row_groups(x, jnp.minimum)),
                jnp.maximum(lh[1], _reduce_row_groups(x, jnp.maximum)))

    lo8, hi8 = lax.fori_loop(0, i, lowest_highest, (jnp.full((SUBLANES, bq), jnp.inf, F32),
                                                    jnp.full((SUBLANES, bq), -jnp.inf, F32)))
    diag = sc_ref[i]
    lo8 = jnp.minimum(lo8, _reduce_row_groups(jnp.where(causal, diag, jnp.inf), jnp.minimum))
    hi8 = jnp.maximum(hi8, _reduce_row_groups(diag, jnp.maximum))
    lo0 = jnp.min(lo8, axis=0, keepdims=True)
    hi0 = jnp.max(hi8, axis=0, keepdims=True)

    def count_ge(cand):
        def body(c, cnt):
            return cnt + _reduce_row_groups(jnp.where(sc_ref[c] >= cand, 1.0, 0.0), jnp.add)
        cnt = lax.fori_loop(0, i + 1, body, jnp.zeros((SUBLANES, bq), F32))
        return jnp.sum(cnt, axis=0, keepdims=True)

    n_keys = (i * bq + 1 + lax.broadcasted_iota(I32, (1, bq), 1)).astype(F32)
    few_keys = n_keys < float(top_k)

    def bisect(state):
        it, lo, hi, settled = state
        mid = lo + (hi - lo) * 0.5
        cnt = count_ge(mid)
        enough = cnt >= float(top_k)
        open_ = settled == 0.0
        lo_new = jnp.where(open_ & enough, mid, lo)
        hi_new = jnp.where(open_ & jnp.logical_not(enough), mid, hi)
        done = (cnt == float(top_k)) | (mid <= lo) | (mid >= hi)
        return it + 1, lo_new, hi_new, jnp.where(done, 1.0, settled)

    def unsettled(state):
        it, _, _, settled = state
        return (it < SEARCH_STEPS_MAX) & (jnp.min(settled) == 0.0)

    _, lo, _, _ = lax.while_loop(
        unsettled, bisect, (jnp.int32(0), lo0, hi0, jnp.where(few_keys, 1.0, 0.0)))
    thresh = jnp.where(few_keys, -jnp.inf, lo)

    def mask_chunk(c, carry):
        sc_ref[c] = jnp.where(sc_ref[c] >= thresh, 0.0, NEG)
        return carry

    lax.fori_loop(0, i, mask_chunk, 0)
    sc_ref[i] = jnp.where(causal, jnp.where(sc_ref[i] >= thresh, 0.0, NEG), NEG)

    scale = ATTN_HEAD_DIM ** -0.5
    m_ref[...] = jnp.full(m_ref.shape, NEG, F32)
    l_ref[...] = jnp.zeros(l_ref.shape, F32)
    acc_ref[...] = jnp.zeros(acc_ref.shape, F32)

    def attend(c, near):
        mbias = sc_ref[c]
        m_all, l_all = m_ref[...], l_ref[...]
        m_rows, l_rows = [], []

        def qk(h):
            cols = slice(h * ATTN_HEAD_DIM, (h + 1) * ATTN_HEAD_DIM)
            return jnp.dot(k_ref[rows(c), cols], qt_ref[cols, :], preferred_element_type=F32)

        ahead = N_ATTN_HEADS
        s_raw = [qk(h) for h in range(ahead)]
        for h in range(N_ATTN_HEADS):
            cols = slice(h * ATTN_HEAD_DIM, (h + 1) * ATTN_HEAD_DIM)
            s = s_raw[h] * scale + mbias
            if h + ahead < N_ATTN_HEADS:
                s_raw.append(qk(h + ahead))
            if near is not None:
                s = s + bias_ref[h, near * bq:(near + 1) * bq, :]
            m_old = m_all[h:h + 1, :]
            m_new = jnp.maximum(m_old, jnp.max(_reduce_row_groups(s, jnp.maximum), axis=0, keepdims=True))
            p = jnp.exp(s - m_new)
            a = jnp.exp(m_old - m_new)
            m_rows.append(m_new)
            l_rows.append(a * l_all[h:h + 1, :]
                          + jnp.sum(_reduce_row_groups(p, jnp.add), axis=0, keepdims=True))
            pv = lax.dot_general(v_ref[rows(c), cols], p.astype(BF16), TN_DIMS, preferred_element_type=F32)
            acc_ref[h] = a * acc_ref[h] + pv
        m_ref[...] = jnp.concatenate(m_rows, axis=0)
        l_ref[...] = jnp.concatenate(l_rows, axis=0)

    def far_chunk(c, carry):
        attend(c, None)
        return carry

    lax.fori_loop(0, jnp.maximum(i - 1, 0), far_chunk, 0)

    @pl.when(i > 0)
    def _():
        attend(i - 1, 0)

    attend(i, 1)
    for h in range(N_ATTN_HEADS):
        out_t = acc_ref[h] / l_ref[h:h + 1, :]
        o_ref[:, h * ATTN_HEAD_DIM:(h + 1) * ATTN_HEAD_DIM] = out_t.T.astype(o_ref.dtype)


def _rel_bucket(dist):
    n = jnp.maximum(dist, 0)
    max_exact = N_REL_BUCKETS // 2
    nf = jnp.maximum(n, 1).astype(F32)
    large = max_exact + (jnp.log(nf / max_exact) / math.log(REL_MAX_DISTANCE / max_exact)
                         * (N_REL_BUCKETS - max_exact)).astype(I32)
    large = jnp.minimum(large, N_REL_BUCKETS - 1)
    return jnp.where(n < max_exact, n, large)


def _near_bias_tiles(rel_bias, bq):
    assert bq >= REL_MAX_DISTANCE
    key = jnp.arange(2 * bq, dtype=I32)[:, None]
    qry = jnp.arange(bq, dtype=I32)[None, :]
    bucket = _rel_bucket(qry + bq - key)
    rel = (rel_bias - rel_bias[N_REL_BUCKETS - 1]).T
    tiles = jnp.broadcast_to(rel[:, 0][:, None, None], (rel.shape[0],) + bucket.shape)
    for b in range(1, N_REL_BUCKETS):
        tiles = jnp.where(bucket[None] >= b, rel[:, b][:, None, None], tiles)
    return tiles


def _dsa_attention(qt, kv, ki_b, wt, rel_bias, bq):
    S = kv.shape[0]
    top_k = min(TOPK_MAX, S // 4)
    n_blk = S // bq
    bias = _near_bias_tiles(rel_bias, bq)
    kernel = functools.partial(_dsa_kernel, bq=bq, top_k=top_k)
    return pl.pallas_call(
        kernel,
        grid=(n_blk,),
        in_specs=[pl.BlockSpec((N_IDX_HEADS * IDX_HEAD_DIM, bq), lambda i: (1, i)),
                  pl.BlockSpec((N_IDX_HEADS, bq), lambda i: (0, i)),
                  pl.BlockSpec((ATTN_WIDTH, bq), lambda i: (0, i)),
                  _resident((S, IDX_HEAD_DIM), lambda i: (0, 0)),
                  _resident((S, ATTN_WIDTH), lambda i: (0, 0)),
                  _resident((S, ATTN_WIDTH), lambda i: (0, 1)),
                  _resident((N_ATTN_HEADS, 2 * bq, bq), lambda i: (0, 0, 0))],
        out_specs=pl.BlockSpec((bq, ATTN_WIDTH), lambda i: (i, 0)),
        out_shape=jax.ShapeDtypeStruct((S, ATTN_WIDTH), BF16),
        scratch_shapes=[pltpu.VMEM((n_blk, bq, bq), F32),
                        pltpu.VMEM((N_ATTN_HEADS, bq), F32),
                        pltpu.VMEM((N_ATTN_HEADS, bq), F32),
                        pltpu.VMEM((N_ATTN_HEADS, ATTN_HEAD_DIM, bq), F32)],
        compiler_params=_params("arbitrary"),
        name="dsa_attention",
    )(qt, wt, qt, ki_b, kv, kv, bias)


def _sgu_kernel(gu_ref, gv_ref, g_ref, b_ref, ws_ref, bs_ref, o_ref, *, n_chunk):
    u = _gelu(gu_ref[...])
    v = _layer_norm(_gelu(gv_ref[...]), g_ref[...], b_ref[...]).astype(BF16)
    r = lax.broadcasted_iota(I32, (GMLP_CHUNK, GMLP_CHUNK), 0)
    c = lax.broadcasted_iota(I32, (GMLP_CHUNK, GMLP_CHUNK), 1)
    for g in range(GMLP_GROUPS):
        wg = jnp.where(c <= r, ws_ref[g], 0.0).astype(BF16)
        bcol = bs_ref[:, g:g + 1]
        cols = slice(g * LANES, (g + 1) * LANES)
        for n in range(n_chunk):
            rws = slice(n * GMLP_CHUNK, (n + 1) * GMLP_CHUNK)
            mixed = jnp.dot(wg, v[rws, cols], preferred_element_type=F32) + bcol
            o_ref[rws, cols] = (u[rws, cols] * mixed).astype(o_ref.dtype)


def _sgu(zg, ln_g, ln_b, w_s, b_s, tm):
    S = zg.shape[0]
    kernel = functools.partial(_sgu_kernel, n_chunk=tm // GMLP_CHUNK)
    return pl.pallas_call(
        kernel,
        grid=(S // tm,),
        in_specs=[pl.BlockSpec((tm, GMLP_WIDTH), lambda i: (i, 0)),
                  pl.BlockSpec((tm, GMLP_WIDTH), lambda i: (i, 1)),
                  pl.BlockSpec((1, GMLP_WIDTH), lambda i: (0, 0)),
                  pl.BlockSpec((1, GMLP_WIDTH), lambda i: (0, 0)),
                  pl.BlockSpec((GMLP_GROUPS, GMLP_CHUNK, GMLP_CHUNK), lambda i: (0, 0, 0)),
                  pl.BlockSpec((GMLP_CHUNK, GMLP_GROUPS), lambda i: (0, 0))],
        out_specs=pl.BlockSpec((tm, GMLP_WIDTH), lambda i: (i, 0)),
        out_shape=jax.ShapeDtypeStruct((S, GMLP_WIDTH), BF16),
        compiler_params=_params("parallel"),
        name="gmlp_sgu",
    )(zg, zg, ln_g.reshape(1, -1), ln_b.reshape(1, -1), w_s, b_s.T)


def _merge_kernel(at_ref, gm_ref, ga_ref, gg_ref, h_ref, wa_ref, wg_ref, wo_ref, g_ref, b_ref,
                  h1_ref, h1b_ref, h1bt_ref):
    a1 = jnp.dot(at_ref[...], wa_ref[...], preferred_element_type=F32)
    a2 = jnp.dot(gm_ref[...], wg_ref[...], preferred_element_type=F32)
    merged = jax.nn.sigmoid(ga_ref[...]) * a1 + jax.nn.sigmoid(gg_ref[...]) * a2
    y = DEEPNORM_ALPHA * h_ref[...] + jnp.dot(merged.astype(BF16), wo_ref[...], preferred_element_type=F32)
    h1 = _layer_norm(y, g_ref[...], b_ref[...])
    h1_ref[...] = h1
    h1b_ref[...] = h1.astype(BF16)
    h1bt_ref[...] = h1.T.astype(BF16)


def _merge(attn, gm, zg, h, wa, wg, wo, ln_g, ln_b, tm):
    S, D = h.shape
    W = attn.shape[1]
    return pl.pallas_call(
        _merge_kernel,
        grid=(S // tm,),
        in_specs=[pl.BlockSpec((tm, W), lambda i: (i, 0)),
                  pl.BlockSpec((tm, W), lambda i: (i, 0)),
                  pl.BlockSpec((tm, D), lambda i: (i, 1)),
                  pl.BlockSpec((tm, D), lambda i: (i, 2)),
                  pl.BlockSpec((tm, D), lambda i: (i, 0)),
                  _resident((W, D), lambda i: (0, 0)),
                  _resident((W, D), lambda i: (0, 0)),
                  _resident((D, D), lambda i: (0, 0)),
                  pl.BlockSpec((1, D), lambda i: (0, 0)),
                  pl.BlockSpec((1, D), lambda i: (0, 0))],
        out_specs=[pl.BlockSpec((tm, D), lambda i: (i, 0)),
                   pl.BlockSpec((tm, D), lambda i: (i, 0)),
                   pl.BlockSpec((D, tm), lambda i: (0, i))],
        out_shape=[jax.ShapeDtypeStruct((S, D), F32), jax.ShapeDtypeStruct((S, D), BF16),
                   jax.ShapeDtypeStruct((D, S), BF16)],
        compiler_params=_params("parallel"),
        name="merge_ln1",
    )(attn, gm, zg, zg, h, wa, wg, wo, ln_g.reshape(1, D), ln_b.reshape(1, D))


def _peer_pairs():
    pairs = [(a, b) for a in range(PEER_TOPK) for b in range(PEER_TOPK) if (a + 1) * (b + 1) <= PEER_TOPK]
    return sorted(pairs, key=lambda ab: ab[0] * PEER_TOPK + ab[1])


def _extract_top(work, n_rounds, on_pick, exact):
    n = float(work.shape[0])
    iota = lax.broadcasted_iota(I32, work.shape, 0).astype(F32)
    for r in range(n_rounds):
        mx = jnp.max(work, axis=0, keepdims=True)
        pick = work == mx
        if exact:
            first = jnp.min(jnp.where(pick, iota, n), axis=0, keepdims=True)
            pick = iota == first
        on_pick(r, mx, pick)
        work = jnp.where(pick, -jnp.inf, work)


def _route_kernel(hb_ref, wqt_ref, sk_ref, l1_ref, a_ref, r2_ref, b_ref, *, T):
    qt = lax.dot_general(wqt_ref[...], hb_ref[...], NT_DIMS, preferred_element_type=F32)
    pairs = _peer_pairs()
    scores = [jnp.dot(sk_ref[idx], qt[idx * LANES:(idx + 1) * LANES, :].astype(BF16),
                      preferred_element_type=F32) for idx in range(2 * PEER_HEADS)]

    def route(exact):
        miscount = jnp.zeros((1, T), F32)
        for h in range(PEER_HEADS):
            s, rank, vals = [], [], []
            for p in range(2):
                sp = scores[2 * h + p]
                state = {"rank": jnp.full(sp.shape, float(PEER_TOPK), F32), "vals": []}

                def on_pick(r, mx, pick, state=state):
                    state["rank"] = jnp.where(pick, float(r), state["rank"])
                    state["vals"].append(mx)

                _extract_top(sp, PEER_TOPK, on_pick, exact)
                n_ranked = jnp.sum(jnp.where(state["rank"] < float(PEER_TOPK), 1.0, 0.0), axis=0, keepdims=True)
                miscount = jnp.maximum(miscount, jnp.abs(n_ranked - float(PEER_TOPK)))
                s.append(sp)
                rank.append(state["rank"])
                vals.append(state["vals"])
            cand = jnp.concatenate([vals[0][a] + vals[1][b] for a, b in pairs], axis=0)
            picked = {"mask": jnp.zeros(cand.shape, jnp.bool_)}

            def on_pick_c(r, mx, pick, picked=picked):
                picked["mask"] = picked["mask"] | pick

            _extract_top(cand, PEER_TOPK, on_pick_c, exact)
            top = vals[0][0] + vals[1][0]
            selw = jnp.where(picked["mask"], jnp.exp(cand - top), 0.0)
            z = jnp.sum(selw, axis=0, keepdims=True)
            self32 = jnp.where(picked["mask"], 1.0, 0.0)
            miscount = jnp.maximum(miscount, jnp.abs(jnp.sum(self32, axis=0, keepdims=True) - float(PEER_TOPK)))
            l1 = jnp.zeros(s[0].shape, F32)
            pos = 0
            for a in range(PEER_TOPK):
                cnt = sum(1 for ab in pairs if ab[0] == a)
                n_a = jnp.sum(self32[pos:pos + cnt, :], axis=0, keepdims=True)
                pos += cnt
                l1 = jnp.where(rank[0] == float(a), n_a, l1)
            l1_ref[h] = l1
            a_ref[h] = jnp.exp(s[0] - vals[0][0])
            r2_ref[h] = rank[1].astype(r2_ref.dtype)
            b_ref[h] = (jnp.exp(s[1] - vals[1][0]) / z).astype(b_ref.dtype)
        return miscount

    miscount = route(exact=False)

    @pl.when(jnp.max(miscount) > 0.0)
    def _():
        route(exact=True)


def _peer_route(h1b, wqt, sk, T):
    S, D = h1b.shape
    kernel = functools.partial(_route_kernel, T=T)
    shp = jax.ShapeDtypeStruct((PEER_HEADS, PEER_N_KEYS, S), F32)
    shp_b = jax.ShapeDtypeStruct((PEER_HEADS, PEER_N_KEYS, S), BF16)
    spec = pl.BlockSpec((PEER_HEADS, PEER_N_KEYS, T), lambda t: (0, 0, t))
    return pl.pallas_call(
        kernel,
        grid=(S // T,),
        in_specs=[pl.BlockSpec((T, D), lambda t: (t, 0)),
                  _resident(wqt.shape, lambda t: (0, 0)),
                  _resident(sk.shape, lambda t: (0, 0, 0))],
        out_specs=[spec, spec, spec, spec],
        out_shape=[shp, shp, shp_b, shp_b],
        compiler_params=_params("parallel"),
        name="peer_route",
    )(h1b, wqt, sk)


def _peer_kernel(ht_ref, u_ref, v_ref, l1_ref, a_ref, r2_ref, b_ref, o_ref, act_ref, *, n_i1, ce):
    @pl.when(pl.program_id(1) == 0)
    def _():
        o_ref[...] = jnp.zeros_like(o_ref)

    T = ht_ref.shape[1]
    n_slab = PEER_N_KEYS // PACKED_ROWS
    i1_per_chunk = ce // PEER_N_KEYS
    r2 = [r2_ref[h] for h in range(PEER_HEADS)]
    b = [b_ref[h] for h in range(PEER_HEADS)]

    def gates(c):
        tiles = []
        for j in range(c * i1_per_chunk, (c + 1) * i1_per_chunk):
            gate = [None] * n_slab
            for h in range(PEER_HEADS):
                l1 = jnp.broadcast_to(l1_ref[h, j:j + 1, :], (PACKED_ROWS, T)).astype(BF16)
                a = jnp.broadcast_to(a_ref[h, j:j + 1, :], (PACKED_ROWS, T)).astype(BF16)
                for s in range(n_slab):
                    slab = slice(s * PACKED_ROWS, (s + 1) * PACKED_ROWS)
                    val = jnp.where(r2[h][slab] < l1, b[h][slab], 0.0) * a
                    gate[s] = val if gate[s] is None else gate[s] + val
            tiles.extend(gate)
        return tiles

    n_chunk = n_i1 * PEER_N_KEYS // ce
    zero = jnp.minimum(pl.program_id(1), 0)
    for c in range(n_chunk):
        act_ref[c + zero] = jnp.dot(u_ref[c * ce:(c + 1) * ce, :], ht_ref[...],
                                    preferred_element_type=F32)
    tiles = []
    for c in range(n_chunk):
        gel = _gelu(act_ref[c + zero]).astype(BF16)
        g = gates(c)
        tiles.extend(gel[k * PACKED_ROWS:(k + 1) * PACKED_ROWS] * g[k] for k in range(len(g)))
    pt = jnp.concatenate(tiles, axis=0)
    o_ref[...] += lax.dot_general(pt, v_ref[...], TN_DIMS, preferred_element_type=F32)


def _peer_experts(h1bt, u_b, v_b, l1, a, r2, b, T, eb):
    D, S = h1bt.shape
    E = u_b.shape[0]
    n_i1 = eb // PEER_N_KEYS
    kernel = functools.partial(_peer_kernel, n_i1=n_i1, ce=MXU_DIM)
    row_spec = pl.BlockSpec((PEER_HEADS, n_i1, T), lambda t, e: (0, e, t))
    full_spec = pl.BlockSpec((PEER_HEADS, PEER_N_KEYS, T), lambda t, e: (0, 0, t))
    return pl.pallas_call(
        kernel,
        grid=(S // T, E // eb),
        in_specs=[pl.BlockSpec((D, T), lambda t, e: (0, t)),
                  pl.BlockSpec((eb, D), lambda t, e: (e, 0)),
                  pl.BlockSpec((eb, D), lambda t, e: (e, 0)),
                  row_spec, row_spec, full_spec, full_spec],
        out_specs=pl.BlockSpec((T, D), lambda t, e: (t, 0)),
        out_shape=jax.ShapeDtypeStruct((S, D), F32),
        scratch_shapes=[pltpu.VMEM((eb // MXU_DIM, MXU_DIM, T), F32)],
        compiler_params=_params("parallel", "arbitrary"),
        name="peer_experts",
    )(h1bt, u_b, v_b, l1, a, r2, b)


def _ple_kernel(h1_ref, peer_ref, p_ref, wg_ref, wp_ref, g_ref, b_ref, o_ref):
    r = DEEPNORM_ALPHA * h1_ref[...] + peer_ref[...]
    gate = jax.nn.sigmoid(jnp.dot(r.astype(BF16), wg_ref[...], preferred_element_type=F32))
    proj = jnp.dot(p_ref[...], wp_ref[...], preferred_element_type=F32)
    o_ref[...] = _layer_norm(r + gate * proj, g_ref[...], b_ref[...])


def _ple(h1, peer, p_b, wg, wp, ln_g, ln_b, tm):
    S, D = h1.shape
    P = p_b.shape[1]
    return pl.pallas_call(
        _ple_kernel,
        grid=(S // tm,),
        in_specs=[pl.BlockSpec((tm, D), lambda i: (i, 0)),
                  pl.BlockSpec((tm, D), lambda i: (i, 0)),
                  pl.BlockSpec((tm, P), lambda i: (i, 0)),
                  _resident((D, D), lambda i: (0, 0)),
                  _resident((P, D), lambda i: (0, 0)),
                  pl.BlockSpec((1, D), lambda i: (0, 0)),
                  pl.BlockSpec((1, D), lambda i: (0, 0))],
        out_specs=pl.BlockSpec((tm, D), lambda i: (i, 0)),
        out_shape=jax.ShapeDtypeStruct((S, D), F32),
        compiler_params=_params("parallel"),
        name="ple_ln2",
    )(h1, peer, p_b, wg, wp, ln_g.reshape(1, D), ln_b.reshape(1, D))


def _layer(x, p, ln_emb_g, ln_emb_b, rel_bias, w_in, gmlp_ln_g, gmlp_ln_b, gmlp_w_s, gmlp_b_s,
           w_br_attn, w_br_gmlp, w_mix_out, ln1_g, ln1_b, peer_w_q, peer_sub_keys, peer_u, peer_v,
           ple_w_proj, ple_w_gate, ln2_g, ln2_b):
    S, D = x.shape
    tm = min(512, S)
    o_k, o_v, o_qi = ATTN_WIDTH, 2 * ATTN_WIDTH, 3 * ATTN_WIDTH
    o_ki = o_qi + N_IDX_HEADS * IDX_HEAD_DIM
    o_gu = o_ki + IDX_HEAD_DIM + N_IDX_HEADS

    h, hb = _ln_embed(x, ln_emb_g, ln_emb_b, tm)
    wqt = jnp.concatenate([w_in[:, :o_k], w_in[:, o_qi:o_ki]], axis=1).T.astype(BF16)
    w1 = w_in[:, o_k:o_qi].astype(BF16)
    w2 = jnp.pad(w_in[:, o_ki:o_gu], ((0, 0), (0, LANES - (o_gu - o_ki)))).astype(BF16)
    w3 = w_in[:, o_gu:].astype(BF16)
    qt = _matmul_t(wqt, hb, BF16, tm, "in_proj_q")
    kv = _matmul(hb, w1, BF16, min(1024, S), 512, "in_proj_kv")
    kw = _matmul(hb, w2, F32, min(1024, S), LANES, "in_proj_index")
    zg = _matmul(hb, w3, F32, min(1024, S), 512, "in_proj_gmlp")

    attn = _dsa_attention(qt, kv, kw[:, :IDX_HEAD_DIM].astype(BF16),
                          kw[:, IDX_HEAD_DIM:IDX_HEAD_DIM + N_IDX_HEADS].T, rel_bias, bq=256)
    gm = _sgu(zg, gmlp_ln_g, gmlp_ln_b, gmlp_w_s, gmlp_b_s, tm)
    h1, h1b, h1bt = _merge(attn, gm, zg, h, w_br_attn.astype(BF16), w_br_gmlp.astype(BF16),
                           w_mix_out.astype(BF16), ln1_g, ln1_b, min(256, S))

    sk = peer_sub_keys.reshape(PEER_HEADS * 2, PEER_N_KEYS, -1).astype(BF16)
    l1, a, r2, b = _peer_route(h1b, peer_w_q.T.astype(BF16), sk, min(256, S))
    peer = _peer_experts(h1bt, peer_u.astype(BF16), peer_v.astype(BF16), l1, a, r2, b,
                         T=min(512, S), eb=1024)
    return _ple(h1, peer, p.astype(BF16), ple_w_gate.astype(BF16), ple_w_proj.astype(BF16),
                ln2_g, ln2_b, min(256, S))


def kernel(x, p, positions, ln_emb_g, ln_emb_b, rel_bias, w_in, gmlp_ln_g, gmlp_ln_b, gmlp_w_s, gmlp_b_s, w_br_attn, w_br_gmlp, w_mix_out, ln1_g, ln1_b, peer_w_q, peer_sub_keys, peer_u, peer_v, ple_w_proj, ple_w_gate, ln2_g, ln2_b):
    del positions
    assert x.shape[0] == 1 and w_in.shape[0] == DEPTH
    out = _layer(x[0], p[0, 0], ln_emb_g, ln_emb_b, rel_bias, w_in[0], gmlp_ln_g[0], gmlp_ln_b[0],
                 gmlp_w_s[0], gmlp_b_s[0], w_br_attn[0], w_br_gmlp[0], w_mix_out[0], ln1_g[0],
                 ln1_b[0], peer_w_q[0], peer_sub_keys[0], peer_u[0], peer_v[0], ple_w_proj[0],
                 ple_w_gate[0], ln2_g[0], ln2_b[0])
    return out[None]
```

```python
import functools
import math

import jax
import jax.numpy as jnp
from jax import lax
from jax.experimental import pallas as pl
from jax.experimental.pallas import tpu as pltpu

F32 = jnp.float32
BF16 = jnp.bfloat16
I32 = jnp.int32

LN_EPS = 1e-5
DEPTH = 1
DEEPNORM_ALPHA = (2.0 * DEPTH) ** 0.25

N_ATTN_HEADS = 8
ATTN_HEAD_DIM = 128
ATTN_WIDTH = N_ATTN_HEADS * ATTN_HEAD_DIM
N_IDX_HEADS = 16
IDX_HEAD_DIM = 64
TOPK_MAX = 256
GMLP_CHUNK = 128
GMLP_GROUPS = 8
GMLP_WIDTH = 1024
N_REL_BUCKETS = 32
REL_MAX_DISTANCE = 128
PEER_HEADS = 8
PEER_N_KEYS = 128
PEER_TOPK = 16

LANES = 128
SUBLANES = 8
PACKED_ROWS = 2 * SUBLANES
MXU_DIM = 256
NEG = -1e30
SEARCH_STEPS_MAX = 64
NT_DIMS = (((1,), (1,)), ((), ()))
TN_DIMS = (((0,), (0,)), ((), ()))
VMEM_LIMIT = 56 * 1024 * 1024


def _params(*sem):
    return pltpu.CompilerParams(dimension_semantics=sem, vmem_limit_bytes=VMEM_LIMIT)


def _resident(shape, index_map):
    return pl.BlockSpec(shape, index_map, pipeline_mode=pl.Buffered(1))


def _gelu(x):
    return 0.5 * x * (1.0 + jnp.tanh(0.7978845608028654 * (x + 0.044715 * (x * x * x))))


def _layer_norm(x, g, b):
    mu = jnp.mean(x, axis=-1, keepdims=True)
    xc = x - mu
    var = jnp.mean(xc * xc, axis=-1, keepdims=True)
    return xc * lax.rsqrt(var + LN_EPS) * g + b


def _ln_kernel(x_ref, g_ref, b_ref, h_ref, hb_ref):
    y = _layer_norm(x_ref[...], g_ref[...], b_ref[...])
    h_ref[...] = y
    hb_ref[...] = y.astype(BF16)


def _ln_embed(x, g, b, tm):
    S, D = x.shape
    return pl.pallas_call(
        _ln_kernel,
        grid=(S // tm,),
        in_specs=[pl.BlockSpec((tm, D), lambda i: (i, 0)),
                  pl.BlockSpec((1, D), lambda i: (0, 0)),
                  pl.BlockSpec((1, D), lambda i: (0, 0))],
        out_specs=[pl.BlockSpec((tm, D), lambda i: (i, 0)),
                   pl.BlockSpec((tm, D), lambda i: (i, 0))],
        out_shape=[jax.ShapeDtypeStruct((S, D), F32), jax.ShapeDtypeStruct((S, D), BF16)],
        compiler_params=_params("parallel"),
        name="ln_embed",
    )(x, g.reshape(1, D), b.reshape(1, D))


def _mm_kernel(a_ref, w_ref, o_ref):
    o_ref[...] = jnp.dot(a_ref[...], w_ref[...], preferred_element_type=F32).astype(o_ref.dtype)


def _matmul(a, w, out_dtype, tm, tn, name):
    M, K = a.shape
    N = w.shape[1]
    return pl.pallas_call(
        _mm_kernel,
        grid=(M // tm, N // tn),
        in_specs=[pl.BlockSpec((tm, K), lambda i, j: (i, 0)),
                  pl.BlockSpec((K, tn), lambda i, j: (0, j))],
        out_specs=pl.BlockSpec((tm, tn), lambda i, j: (i, j)),
        out_shape=jax.ShapeDtypeStruct((M, N), out_dtype),
        compiler_params=_params("parallel", "arbitrary"),
        name=name,
    )(a, w)


def _mm_t_kernel(wt_ref, a_ref, o_ref):
    o_ref[...] = lax.dot_general(wt_ref[...], a_ref[...], NT_DIMS,
                                 preferred_element_type=F32).astype(o_ref.dtype)


def _matmul_t(wt, a, out_dtype, tm, name):
    M, K = a.shape
    N = wt.shape[0]
    return pl.pallas_call(
        _mm_t_kernel,
        grid=(M // tm,),
        in_specs=[_resident((N, K), lambda i: (0, 0)),
                  pl.BlockSpec((tm, K), lambda i: (i, 0))],
        out_specs=pl.BlockSpec((N, tm), lambda i: (0, i)),
        out_shape=jax.ShapeDtypeStruct((N, M), out_dtype),
        compiler_params=_params("parallel"),
        name=name,
    )(wt, a)


def _reduce_row_groups(t, op):
    parts = [t[j * SUBLANES:(j + 1) * SUBLANES, :] for j in range(t.shape[0] // SUBLANES)]
    while len(parts) > 1:
        parts = [op(parts[j], parts[j + 1]) for j in range(0, len(parts) - 1, 2)] + (
            [parts[-1]] if len(parts) % 2 else [])
    return parts[0]


def _dsa_kernel(qit_ref, wt_ref, qt_ref, ki_ref, k_ref, v_ref, bias_ref, o_ref,
                sc_ref, m_ref, l_ref, acc_ref, *, bq, top_k):
    i = pl.program_id(0)
    w = wt_ref[...] * ((IDX_HEAD_DIM ** -0.5) * (N_IDX_HEADS ** -0.5))
    key_i = lax.broadcasted_iota(I32, (bq, bq), 0)
    qry_i = lax.broadcasted_iota(I32, (bq, bq), 1)
    causal = key_i <= qry_i

    def rows(c):
        return pl.ds(pl.multiple_of(c * bq, bq), bq)

    def scores(c):
        kic = ki_ref[rows(c), :]
        acc = jnp.zeros((bq, bq), F32)
        for h in range(N_IDX_HEADS):
            d = jnp.dot(kic, qit_ref[h * IDX_HEAD_DIM:(h + 1) * IDX_HEAD_DIM, :],
                        preferred_element_type=F32)
            acc = acc + w[h:h + 1, :] * jnp.maximum(d, 0.0)
        return acc

    def score_chunk(c, carry):
        sc_ref[c] = scores(c)
        return carry

    lax.fori_loop(0, i, score_chunk, 0)
    sc_ref[i] = jnp.where(causal, scores(i), -jnp.inf)

    def lowest_highest(c, lh):
        x = sc_ref[c]
        return (jnp.minimum(lh[0], _reduce_row_groups(x, jnp.minimum)),
                jnp.maximum(lh[1], _reduce_row_groups(x, jnp.maximum)))

    lo8, hi8 = lax.fori_loop(0, i, lowest_highest, (jnp.full((SUBLANES, bq), jnp.inf, F32),
                                                    jnp.full((SUBLANES, bq), -jnp.inf, F32)))
    diag = sc_ref[i]
    lo8 = jnp.minimum(lo8, _reduce_row_groups(jnp.where(causal, diag, jnp.inf), jnp.minimum))
    hi8 = jnp.maximum(hi8, _reduce_row_groups(diag, jnp.maximum))
    lo0 = jnp.min(lo8, axis=0, keepdims=True)
    hi0 = jnp.max(hi8, axis=0, keepdims=True)

    def count_ge(cand):
        def body(c, cnt):
            return cnt + _reduce_row_groups(jnp.where(sc_ref[c] >= cand, 1.0, 0.0), jnp.add)
        cnt = lax.fori_loop(0, i + 1, body, jnp.zeros((SUBLANES, bq), F32))
        return jnp.sum(cnt, axis=0, keepdims=True)

    n_keys = (i * bq + 1 + lax.broadcasted_iota(I32, (1, bq), 1)).astype(F32)
    few_keys = n_keys < float(top_k)

    def bisect(state):
        it, lo, hi, settled = state
        mid = lo + (hi - lo) * 0.5
        cnt = count_ge(mid)
        enough = cnt >= float(top_k)
        open_ = settled == 0.0
        lo_new = jnp.where(open_ & enough, mid, lo)
        hi_new = jnp.where(open_ & jnp.logical_not(enough), mid, hi)
        done = (cnt == float(top_k)) | (mid <= lo) | (mid >= hi)
        return it + 1, lo_new, hi_new, jnp.where(done, 1.0, settled)

    def unsettled(state):
        it, _, _, settled = state
        return (it < SEARCH_STEPS_MAX) & (jnp.min(settled) == 0.0)

    _, lo, _, _ = lax.while_loop(
        unsettled, bisect, (jnp.int32(0), lo0, hi0, jnp.where(few_keys, 1.0, 0.0)))
    thresh = jnp.where(few_keys, -jnp.inf, lo)

    def mask_chunk(c, carry):
        sc_ref[c] = jnp.where(sc_ref[c] >= thresh, 0.0, NEG)
        return carry

    lax.fori_loop(0, i, mask_chunk, 0)
    sc_ref[i] = jnp.where(causal, jnp.where(sc_ref[i] >= thresh, 0.0, NEG), NEG)

    scale = ATTN_HEAD_DIM ** -0.5
    m_ref[...] = jnp.full(m_ref.shape, NEG, F32)
    l_ref[...] = jnp.zeros(l_ref.shape, F32)
    acc_ref[...] = jnp.zeros(acc_ref.shape, F32)

    def attend(c, near):
        mbias = sc_ref[c]
        m_all, l_all = m_ref[...], l_ref[...]
        m_rows, l_rows = [], []

        def qk(h):
            cols = slice(h * ATTN_HEAD_DIM, (h + 1) * ATTN_HEAD_DIM)
            return jnp.dot(k_ref[rows(c), cols], qt_ref[cols, :], preferred_element_type=F32)

        ahead = N_ATTN_HEADS
        s_raw = [qk(h) for h in range(ahead)]
        for h in range(N_ATTN_HEADS):
            cols = slice(h * ATTN_HEAD_DIM, (h + 1) * ATTN_HEAD_DIM)
            s = s_raw[h] * scale + mbias
            if h + ahead < N_ATTN_HEADS:
                s_raw.append(qk(h + ahead))
            if near is not None:
                s = s + bias_ref[h, near * bq:(near + 1) * bq, :]
            m_old = m_all[h:h + 1, :]
            m_new = jnp.maximum(m_old, jnp.max(_reduce_row_groups(s, jnp.maximum), axis=0, keepdims=True))
            p = jnp.exp(s - m_new)
            a = jnp.exp(m_old - m_new)
            m_rows.append(m_new)
            l_rows.append(a * l_all[h:h + 1, :]
                          + jnp.sum(_reduce_row_groups(p, jnp.add), axis=0, keepdims=True))
            pv = lax.dot_general(v_ref[rows(c), cols], p.astype(BF16), TN_DIMS, preferred_element_type=F32)
            acc_ref[h] = a * acc_ref[h] + pv
        m_ref[...] = jnp.concatenate(m_rows, axis=0)
        l_ref[...] = jnp.concatenate(l_rows, axis=0)

    def far_chunk(c, carry):
        attend(c, None)
        return carry

    lax.fori_loop(0, jnp.maximum(i - 1, 0), far_chunk, 0)

    @pl.when(i > 0)
    def _():
        attend(i - 1, 0)

    attend(i, 1)
    for h in range(N_ATTN_HEADS):
        out_t = acc_ref[h] / l_ref[h:h + 1, :]
        o_ref[:, h * ATTN_HEAD_DIM:(h + 1) * ATTN_HEAD_DIM] = out_t.T.astype(o_ref.dtype)


def _rel_bucket(dist):
    n = jnp.maximum(dist, 0)
    max_exact = N_REL_BUCKETS // 2
    nf = jnp.maximum(n, 1).astype(F32)
    large = max_exact + (jnp.log(nf / max_exact) / math.log(REL_MAX_DISTANCE / max_exact)
                         * (N_REL_BUCKETS - max_exact)).astype(I32)
    large = jnp.minimum(large, N_REL_BUCKETS - 1)
    return jnp.where(n < max_exact, n, large)


def _near_bias_tiles(rel_bias, bq):
    assert bq >= REL_MAX_DISTANCE
    key = jnp.arange(2 * bq, dtype=I32)[:, None]
    qry = jnp.arange(bq, dtype=I32)[None, :]
    bucket = _rel_bucket(qry + bq - key)
    rel = (rel_bias - rel_bias[N_REL_BUCKETS - 1]).T
    tiles = jnp.broadcast_to(rel[:, 0][:, None, None], (rel.shape[0],) + bucket.shape)
    for b in range(1, N_REL_BUCKETS):
        tiles = jnp.where(bucket[None] >= b, rel[:, b][:, None, None], tiles)
    return tiles


def _dsa_attention(qt, kv, ki_b, wt, rel_bias, bq):
    S = kv.shape[0]
    top_k = min(TOPK_MAX, S // 4)
    n_blk = S // bq
    bias = _near_bias_tiles(rel_bias, bq)
    kernel = functools.partial(_dsa_kernel, bq=bq, top_k=top_k)
    return pl.pallas_call(
        kernel,
        grid=(n_blk,),
        in_specs=[pl.BlockSpec((N_IDX_HEADS * IDX_HEAD_DIM, bq), lambda i: (1, i)),
                  pl.BlockSpec((N_IDX_HEADS, bq), lambda i: (0, i)),
                  pl.BlockSpec((ATTN_WIDTH, bq), lambda i: (0, i)),
                  _resident((S, IDX_HEAD_DIM), lambda i: (0, 0)),
                  _resident((S, ATTN_WIDTH), lambda i: (0, 0)),
                  _resident((S, ATTN_WIDTH), lambda i: (0, 1)),
                  _resident((N_ATTN_HEADS, 2 * bq, bq), lambda i: (0, 0, 0))],
        out_specs=pl.BlockSpec((bq, ATTN_WIDTH), lambda i: (i, 0)),
        out_shape=jax.ShapeDtypeStruct((S, ATTN_WIDTH), BF16),
        scratch_shapes=[pltpu.VMEM((n_blk, bq, bq), F32),
                        pltpu.VMEM((N_ATTN_HEADS, bq), F32),
                        pltpu.VMEM((N_ATTN_HEADS, bq), F32),
                        pltpu.VMEM((N_ATTN_HEADS, ATTN_HEAD_DIM, bq), F32)],
        compiler_params=_params("arbitrary"),
        name="dsa_attention",
    )(qt, wt, qt, ki_b, kv, kv, bias)


def _sgu_kernel(gu_ref, gv_ref, g_ref, b_ref, ws_ref, bs_ref, o_ref, *, n_chunk):
    u = _gelu(gu_ref[...])
    v = _layer_norm(_gelu(gv_ref[...]), g_ref[...], b_ref[...]).astype(BF16)
    r = lax.broadcasted_iota(I32, (GMLP_CHUNK, GMLP_CHUNK), 0)
    c = lax.broadcasted_iota(I32, (GMLP_CHUNK, GMLP_CHUNK), 1)
    for g in range(GMLP_GROUPS):
        wg = jnp.where(c <= r, ws_ref[g], 0.0).astype(BF16)
        bcol = bs_ref[:, g:g + 1]
        cols = slice(g * LANES, (g + 1) * LANES)
        for n in range(n_chunk):
            rws = slice(n * GMLP_CHUNK, (n + 1) * GMLP_CHUNK)
            mixed = jnp.dot(wg, v[rws, cols], preferred_element_type=F32) + bcol
            o_ref[rws, cols] = (u[rws, cols] * mixed).astype(o_ref.dtype)


def _sgu(zg, ln_g, ln_b, w_s, b_s, tm):
    S = zg.shape[0]
    kernel = functools.partial(_sgu_kernel, n_chunk=tm // GMLP_CHUNK)
    return pl.pallas_call(
        kernel,
        grid=(S // tm,),
        in_specs=[pl.BlockSpec((tm, GMLP_WIDTH), lambda i: (i, 0)),
                  pl.BlockSpec((tm, GMLP_WIDTH), lambda i: (i, 1)),
                  pl.BlockSpec((1, GMLP_WIDTH), lambda i: (0, 0)),
                  pl.BlockSpec((1, GMLP_WIDTH), lambda i: (0, 0)),
                  pl.BlockSpec((GMLP_GROUPS, GMLP_CHUNK, GMLP_CHUNK), lambda i: (0, 0, 0)),
                  pl.BlockSpec((GMLP_CHUNK, GMLP_GROUPS), lambda i: (0, 0))],
        out_specs=pl.BlockSpec((tm, GMLP_WIDTH), lambda i: (i, 0)),
        out_shape=jax.ShapeDtypeStruct((S, GMLP_WIDTH), BF16),
        compiler_params=_params("parallel"),
        name="gmlp_sgu",
    )(zg, zg, ln_g.reshape(1, -1), ln_b.reshape(1, -1), w_s, b_s.T)


def _merge_kernel(at_ref, gm_ref, ga_ref, gg_ref, h_ref, wa_ref, wg_ref, wo_ref, g_ref, b_ref,
                  h1_ref, h1b_ref, h1bt_ref):
    a1 = jnp.dot(at_ref[...], wa_ref[...], preferred_element_type=F32)
    a2 = jnp.dot(gm_ref[...], wg_ref[...], preferred_element_type=F32)
    merged = jax.nn.sigmoid(ga_ref[...]) * a1 + jax.nn.sigmoid(gg_ref[...]) * a2
    y = DEEPNORM_ALPHA * h_ref[...] + jnp.dot(merged.astype(BF16), wo_ref[...], preferred_element_type=F32)
    h1 = _layer_norm(y, g_ref[...], b_ref[...])
    h1_ref[...] = h1
    h1b_ref[...] = h1.astype(BF16)
    h1bt_ref[...] = h1.T.astype(BF16)


def _merge(attn, gm, zg, h, wa, wg, wo, ln_g, ln_b, tm):
    S, D = h.shape
    W = attn.shape[1]
    return pl.pallas_call(
        _merge_kernel,
        grid=(S // tm,),
        in_specs=[pl.BlockSpec((tm, W), lambda i: (i, 0)),
                  pl.BlockSpec((tm, W), lambda i: (i, 0)),
                  pl.BlockSpec((tm, D), lambda i: (i, 1)),
                  pl.BlockSpec((tm, D), lambda i: (i, 2)),
                  pl.BlockSpec((tm, D), lambda i: (i, 0)),
                  _resident((W, D), lambda i: (0, 0)),
                  _resident((W, D), lambda i: (0, 0)),
                  _resident((D, D), lambda i: (0, 0)),
                  pl.BlockSpec((1, D), lambda i: (0, 0)),
                  pl.BlockSpec((1, D), lambda i: (0, 0))],
        out_specs=[pl.BlockSpec((tm, D), lambda i: (i, 0)),
                   pl.BlockSpec((tm, D), lambda i: (i, 0)),
                   pl.BlockSpec((D, tm), lambda i: (0, i))],
        out_shape=[jax.ShapeDtypeStruct((S, D), F32), jax.ShapeDtypeStruct((S, D), BF16),
                   jax.ShapeDtypeStruct((D, S), BF16)],
        compiler_params=_params("parallel"),
        name="merge_ln1",
    )(attn, gm, zg, zg, h, wa, wg, wo, ln_g.reshape(1, D), ln_b.reshape(1, D))


def _peer_pairs():
    pairs = [(a, b) for a in range(PEER_TOPK) for b in range(PEER_TOPK) if (a + 1) * (b + 1) <= PEER_TOPK]
    return sorted(pairs, key=lambda ab: ab[0] * PEER_TOPK + ab[1])


def _extract_top(work, n_rounds, on_pick, exact):
    n = float(work.shape[0])
    iota = lax.broadcasted_iota(I32, work.shape, 0).astype(F32)
    for r in range(n_rounds):
        mx = jnp.max(work, axis=0, keepdims=True)
        pick = work == mx
        if exact:
            first = jnp.min(jnp.where(pick, iota, n), axis=0, keepdims=True)
            pick = iota == first
        on_pick(r, mx, pick)
        work = jnp.where(pick, -jnp.inf, work)


def _route_kernel(hb_ref, wq_ref, sk_ref, l1_ref, a_ref, r2_ref, b_ref, *, T):
    qt = jnp.dot(hb_ref[...], wq_ref[...], preferred_element_type=F32).T
    pairs = _peer_pairs()

    def route(h, exact):
        miscount = jnp.zeros((1, T), F32)
        s, rank, vals = [], [], []
        for p in range(2):
            idx = 2 * h + p
            sp = jnp.dot(sk_ref[idx], qt[idx * LANES:(idx + 1) * LANES, :].astype(BF16),
                         preferred_element_type=F32)
            state = {"rank": jnp.full(sp.shape, float(PEER_TOPK), F32), "vals": []}

            def on_pick(r, mx, pick, state=state):
                state["rank"] = jnp.where(pick, float(r), state["rank"])
                state["vals"].append(mx)

            _extract_top(sp, PEER_TOPK, on_pick, exact)
            n_ranked = jnp.sum(jnp.where(state["rank"] < float(PEER_TOPK), 1.0, 0.0), axis=0, keepdims=True)
            miscount = jnp.maximum(miscount, jnp.abs(n_ranked - float(PEER_TOPK)))
            s.append(sp)
            rank.append(state["rank"])
            vals.append(state["vals"])
        cand = jnp.concatenate([vals[0][a] + vals[1][b] for a, b in pairs], axis=0)
        picked = {"mask": jnp.zeros(cand.shape, jnp.bool_)}

        def on_pick_c(r, mx, pick, picked=picked):
            picked["mask"] = picked["mask"] | pick

        _extract_top(cand, PEER_TOPK, on_pick_c, exact)
        top = vals[0][0] + vals[1][0]
        selw = jnp.where(picked["mask"], jnp.exp(cand - top), 0.0)
        z = jnp.sum(selw, axis=0, keepdims=True)
        self32 = jnp.where(picked["mask"], 1.0, 0.0)
        miscount = jnp.maximum(miscount, jnp.abs(jnp.sum(self32, axis=0, keepdims=True) - float(PEER_TOPK)))
        l1 = jnp.zeros(s[0].shape, F32)
        pos = 0
        for a in range(PEER_TOPK):
            cnt = sum(1 for ab in pairs if ab[0] == a)
            n_a = jnp.sum(self32[pos:pos + cnt, :], axis=0, keepdims=True)
            pos += cnt
            l1 = jnp.where(rank[0] == float(a), n_a, l1)
        l1_ref[h] = l1
        a_ref[h] = jnp.exp(s[0] - vals[0][0])
        r2_ref[h] = rank[1].astype(r2_ref.dtype)
        b_ref[h] = (jnp.exp(s[1] - vals[1][0]) / z).astype(b_ref.dtype)
        return miscount

    miscount = functools.reduce(jnp.maximum, [route(h, exact=False) for h in range(PEER_HEADS)])

    @pl.when(jnp.max(miscount) > 0.0)
    def _():
        for h in range(PEER_HEADS):
            route(h, exact=True)


def _peer_route(h1b, wq, sk, T):
    S, D = h1b.shape
    kernel = functools.partial(_route_kernel, T=T)
    shp = jax.ShapeDtypeStruct((PEER_HEADS, PEER_N_KEYS, S), F32)
    shp_b = jax.ShapeDtypeStruct((PEER_HEADS, PEER_N_KEYS, S), BF16)
    spec = pl.BlockSpec((PEER_HEADS, PEER_N_KEYS, T), lambda t: (0, 0, t))
    return pl.pallas_call(
        kernel,
        grid=(S // T,),
        in_specs=[pl.BlockSpec((T, D), lambda t: (t, 0)),
                  _resident(wq.shape, lambda t: (0, 0)),
                  _resident(sk.shape, lambda t: (0, 0, 0))],
        out_specs=[spec, spec, spec, spec],
        out_shape=[shp, shp, shp_b, shp_b],
        compiler_params=_params("parallel"),
        name="peer_route",
    )(h1b, wq, sk)


def _peer_kernel(ht_ref, u_ref, v_ref, l1_ref, a_ref, r2_ref, b_ref, o_ref, act_ref, *, n_i1, ce):
    @pl.when(pl.program_id(1) == 0)
    def _():
        o_ref[...] = jnp.zeros_like(o_ref)

    T = ht_ref.shape[1]
    n_slab = PEER_N_KEYS // PACKED_ROWS
    i1_per_chunk = ce // PEER_N_KEYS
    r2 = [r2_ref[h] for h in range(PEER_HEADS)]
    b = [b_ref[h] for h in range(PEER_HEADS)]

    def gates(c):
        tiles = []
        for j in range(c * i1_per_chunk, (c + 1) * i1_per_chunk):
            gate = [None] * n_slab
            for h in range(PEER_HEADS):
                l1 = jnp.broadcast_to(l1_ref[h, j:j + 1, :], (PACKED_ROWS, T)).astype(BF16)
                a = jnp.broadcast_to(a_ref[h, j:j + 1, :], (PACKED_ROWS, T)).astype(BF16)
                for s in range(n_slab):
                    slab = slice(s * PACKED_ROWS, (s + 1) * PACKED_ROWS)
                    val = jnp.where(r2[h][slab] < l1, b[h][slab], 0.0) * a
                    gate[s] = val if gate[s] is None else gate[s] + val
            tiles.extend(gate)
        return tiles

    n_chunk = n_i1 * PEER_N_KEYS // ce
    zero = jnp.minimum(pl.program_id(1), 0)
    for c in range(n_chunk):
        act_ref[c + zero] = jnp.dot(u_ref[c * ce:(c + 1) * ce, :], ht_ref[...],
                                    preferred_element_type=F32)
    tiles = []
    for c in range(n_chunk):
        gel = _gelu(act_ref[c + zero]).astype(BF16)
        g = gates(c)
        tiles.extend(gel[k * PACKED_ROWS:(k + 1) * PACKED_ROWS] * g[k] for k in range(len(g)))
    pt = jnp.concatenate(tiles, axis=0)
    o_ref[...] += lax.dot_general(pt, v_ref[...], TN_DIMS, preferred_element_type=F32)


def _peer_experts(h1bt, u_b, v_b, l1, a, r2, b, T, eb):
    D, S = h1bt.shape
    E = u_b.shape[0]
    n_i1 = eb // PEER_N_KEYS
    ce = MXU_DIM // 2
    kernel = functools.partial(_peer_kernel, n_i1=n_i1, ce=ce)
    row_spec = pl.BlockSpec((PEER_HEADS, n_i1, T), lambda t, e: (0, e, t))
    full_spec = pl.BlockSpec((PEER_HEADS, PEER_N_KEYS, T), lambda t, e: (0, 0, t))
    return pl.pallas_call(
        kernel,
        grid=(S // T, E // eb),
        in_specs=[pl.BlockSpec((D, T), lambda t, e: (0, t)),
                  pl.BlockSpec((eb, D), lambda t, e: (e, 0)),
                  pl.BlockSpec((eb, D), lambda t, e: (e, 0)),
                  row_spec, row_spec, full_spec, full_spec],
        out_specs=pl.BlockSpec((T, D), lambda t, e: (t, 0)),
        out_shape=jax.ShapeDtypeStruct((S, D), F32),
        scratch_shapes=[pltpu.VMEM((eb // ce, ce, T), F32)],
        compiler_params=_params("parallel", "arbitrary"),
        name="peer_experts",
    )(h1bt, u_b, v_b, l1, a, r2, b)


def _ple_kernel(h1_ref, peer_ref, p_ref, wg_ref, wp_ref, g_ref, b_ref, o_ref):
    r = DEEPNORM_ALPHA * h1_ref[...] + peer_ref[...]
    gate = jax.nn.sigmoid(jnp.dot(r.astype(BF16), wg_ref[...], preferred_element_type=F32))
    proj = jnp.dot(p_ref[...], wp_ref[...], preferred_element_type=F32)
    o_ref[...] = _layer_norm(r + gate * proj, g_ref[...], b_ref[...])


def _ple(h1, peer, p_b, wg, wp, ln_g, ln_b, tm):
    S, D = h1.shape
    P = p_b.shape[1]
    return pl.pallas_call(
        _ple_kernel,
        grid=(S // tm,),
        in_specs=[pl.BlockSpec((tm, D), lambda i: (i, 0)),
                  pl.BlockSpec((tm, D), lambda i: (i, 0)),
                  pl.BlockSpec((tm, P), lambda i: (i, 0)),
                  _resident((D, D), lambda i: (0, 0)),
                  _resident((P, D), lambda i: (0, 0)),
                  pl.BlockSpec((1, D), lambda i: (0, 0)),
                  pl.BlockSpec((1, D), lambda i: (0, 0))],
        out_specs=pl.BlockSpec((tm, D), lambda i: (i, 0)),
        out_shape=jax.ShapeDtypeStruct((S, D), F32),
        compiler_params=_params("parallel"),
        name="ple_ln2",
    )(h1, peer, p_b, wg, wp, ln_g.reshape(1, D), ln_b.reshape(1, D))


def _layer(x, p, ln_emb_g, ln_emb_b, rel_bias, w_in, gmlp_ln_g, gmlp_ln_b, gmlp_w_s, gmlp_b_s,
           w_br_attn, w_br_gmlp, w_mix_out, ln1_g, ln1_b, peer_w_q, peer_sub_keys, peer_u, peer_v,
           ple_w_proj, ple_w_gate, ln2_g, ln2_b):
    S, D = x.shape
    tm = min(512, S)
    o_k, o_v, o_qi = ATTN_WIDTH, 2 * ATTN_WIDTH, 3 * ATTN_WIDTH
    o_ki = o_qi + N_IDX_HEADS * IDX_HEAD_DIM
    o_gu = o_ki + IDX_HEAD_DIM + N_IDX_HEADS

    h, hb = _ln_embed(x, ln_emb_g, ln_emb_b, tm)
    wqt = jnp.concatenate([w_in[:, :o_k], w_in[:, o_qi:o_ki]], axis=1).T.astype(BF16)
    w1 = w_in[:, o_k:o_qi].astype(BF16)
    w2 = jnp.pad(w_in[:, o_ki:o_gu], ((0, 0), (0, LANES - (o_gu - o_ki)))).astype(BF16)
    w3 = w_in[:, o_gu:].astype(BF16)
    qt = _matmul_t(wqt, hb, BF16, tm, "in_proj_q")
    kv = _matmul(hb, w1, BF16, min(1024, S), 512, "in_proj_kv")
    kw = _matmul(hb, w2, F32, min(1024, S), LANES, "in_proj_index")
    zg = _matmul(hb, w3, F32, min(1024, S), 512, "in_proj_gmlp")

    attn = _dsa_attention(qt, kv, kw[:, :IDX_HEAD_DIM].astype(BF16),
                          kw[:, IDX_HEAD_DIM:IDX_HEAD_DIM + N_IDX_HEADS].T, rel_bias, bq=256)
    gm = _sgu(zg, gmlp_ln_g, gmlp_ln_b, gmlp_w_s, gmlp_b_s, tm)
    h1, h1b, h1bt = _merge(attn, gm, zg, h, w_br_attn.astype(BF16), w_br_gmlp.astype(BF16),
                           w_mix_out.astype(BF16), ln1_g, ln1_b, min(256, S))

    sk = peer_sub_keys.reshape(PEER_HEADS * 2, PEER_N_KEYS, -1).astype(BF16)
    l1, a, r2, b = _peer_route(h1b, peer_w_q.astype(BF16), sk, min(256, S))
    peer = _peer_experts(h1bt, peer_u.astype(BF16), peer_v.astype(BF16), l1, a, r2, b,
                         T=min(512, S), eb=1024)
    return _ple(h1, peer, p.astype(BF16), ple_w_gate.astype(BF16), ple_w_proj.astype(BF16),
                ln2_g, ln2_b, min(256, S))


def kernel(x, p, positions, ln_emb_g, ln_emb_b, rel_bias, w_in, gmlp_ln_g, gmlp_ln_b, gmlp_w_s, gmlp_b_s, w_br_attn, w_br_gmlp, w_mix_out, ln1_g, ln1_b, peer_w_q, peer_sub_keys, peer_u, peer_v, ple_w_proj, ple_w_gate, ln2_g, ln2_b):
    del positions
    assert x.shape[0] == 1 and w_in.shape[0] == DEPTH
    out = _layer(x[0], p[0, 0], ln_emb_g, ln_emb_b, rel_bias, w_in[0], gmlp_ln_g[0], gmlp_ln_b[0],
                 gmlp_w_s[0], gmlp_b_s[0], w_br_attn[0], w_br_gmlp[0], w_mix_out[0], ln1_g[0],
                 ln1_b[0], peer_w_q[0], peer_sub_keys[0], peer_u[0], peer_v[0], ple_w_proj[0],
                 ple_w_gate[0], ln2_g[0], ln2_b[0])
    return out[None]
```

```python
import functools
import math

import jax
import jax.numpy as jnp
from jax import lax
from jax.experimental import pallas as pl
from jax.experimental.pallas import tpu as pltpu

F32 = jnp.float32
BF16 = jnp.bfloat16
I32 = jnp.int32

LN_EPS = 1e-5
DEPTH = 1
DEEPNORM_ALPHA = (2.0 * DEPTH) ** 0.25

N_ATTN_HEADS = 8
ATTN_HEAD_DIM = 128
ATTN_WIDTH = N_ATTN_HEADS * ATTN_HEAD_DIM
N_IDX_HEADS = 16
IDX_HEAD_DIM = 64
TOPK_MAX = 256
GMLP_CHUNK = 128
GMLP_GROUPS = 8
GMLP_WIDTH = 1024
N_REL_BUCKETS = 32
REL_MAX_DISTANCE = 128
PEER_HEADS = 8
PEER_N_KEYS = 128
PEER_TOPK = 16

LANES = 128
SUBLANES = 8
PACKED_ROWS = 2 * SUBLANES
MXU_DIM = 256
NEG = -1e30
SEARCH_STEPS_MAX = 64
NT_DIMS = (((1,), (1,)), ((), ()))
TN_DIMS = (((0,), (0,)), ((), ()))
VMEM_LIMIT = 56 * 1024 * 1024


def _params(*sem):
    return pltpu.CompilerParams(dimension_semantics=sem, vmem_limit_bytes=VMEM_LIMIT)


def _resident(shape, index_map):
    return pl.BlockSpec(shape, index_map, pipeline_mode=pl.Buffered(1))


def _gelu(x):
    return 0.5 * x * (1.0 + jnp.tanh(0.7978845608028654 * (x + 0.044715 * (x * x * x))))


def _layer_norm(x, g, b):
    mu = jnp.mean(x, axis=-1, keepdims=True)
    xc = x - mu
    var = jnp.mean(xc * xc, axis=-1, keepdims=True)
    return xc * lax.rsqrt(var + LN_EPS) * g + b


def _ln_kernel(x_ref, g_ref, b_ref, h_ref, hb_ref):
    y = _layer_norm(x_ref[...], g_ref[...], b_ref[...])
    h_ref[...] = y
    hb_ref[...] = y.astype(BF16)


def _ln_embed(x, g, b, tm):
    S, D = x.shape
    return pl.pallas_call(
        _ln_kernel,
        grid=(S // tm,),
        in_specs=[pl.BlockSpec((tm, D), lambda i: (i, 0)),
                  pl.BlockSpec((1, D), lambda i: (0, 0)),
                  pl.BlockSpec((1, D), lambda i: (0, 0))],
        out_specs=[pl.BlockSpec((tm, D), lambda i: (i, 0)),
                   pl.BlockSpec((tm, D), lambda i: (i, 0))],
        out_shape=[jax.ShapeDtypeStruct((S, D), F32), jax.ShapeDtypeStruct((S, D), BF16)],
        compiler_params=_params("parallel"),
        name="ln_embed",
    )(x, g.reshape(1, D), b.reshape(1, D))


def _mm_kernel(a_ref, w_ref, o_ref):
    o_ref[...] = jnp.dot(a_ref[...], w_ref[...], preferred_element_type=F32).astype(o_ref.dtype)


def _matmul(a, w, out_dtype, tm, tn, name):
    M, K = a.shape
    N = w.shape[1]
    return pl.pallas_call(
        _mm_kernel,
        grid=(M // tm, N // tn),
        in_specs=[pl.BlockSpec((tm, K), lambda i, j: (i, 0)),
                  pl.BlockSpec((K, tn), lambda i, j: (0, j))],
        out_specs=pl.BlockSpec((tm, tn), lambda i, j: (i, j)),
        out_shape=jax.ShapeDtypeStruct((M, N), out_dtype),
        compiler_params=_params("parallel", "arbitrary"),
        name=name,
    )(a, w)


def _mm_t_kernel(wt_ref, a_ref, o_ref):
    o_ref[...] = lax.dot_general(wt_ref[...], a_ref[...], NT_DIMS,
                                 preferred_element_type=F32).astype(o_ref.dtype)


def _matmul_t(wt, a, out_dtype, tm, name):
    M, K = a.shape
    N = wt.shape[0]
    return pl.pallas_call(
        _mm_t_kernel,
        grid=(M // tm,),
        in_specs=[_resident((N, K), lambda i: (0, 0)),
                  pl.BlockSpec((tm, K), lambda i: (i, 0))],
        out_specs=pl.BlockSpec((N, tm), lambda i: (0, i)),
        out_shape=jax.ShapeDtypeStruct((N, M), out_dtype),
        compiler_params=_params("parallel"),
        name=name,
    )(wt, a)


def _reduce_row_groups(t, op):
    parts = [t[j * SUBLANES:(j + 1) * SUBLANES, :] for j in range(t.shape[0] // SUBLANES)]
    while len(parts) > 1:
        parts = [op(parts[j], parts[j + 1]) for j in range(0, len(parts) - 1, 2)] + (
            [parts[-1]] if len(parts) % 2 else [])
    return parts[0]


def _dsa_kernel(qit_ref, wt_ref, qt_ref, ki_ref, k_ref, v_ref, bias_ref, o_ref,
                sc_ref, m_ref, l_ref, acc_ref, *, bq, top_k):
    i = pl.program_id(0)
    w = wt_ref[...] * ((IDX_HEAD_DIM ** -0.5) * (N_IDX_HEADS ** -0.5))
    key_i = lax.broadcasted_iota(I32, (bq, bq), 0)
    qry_i = lax.broadcasted_iota(I32, (bq, bq), 1)
    causal = key_i <= qry_i

    def rows(c):
        return pl.ds(pl.multiple_of(c * bq, bq), bq)

    def scores(c):
        kic = ki_ref[rows(c), :]
        acc = jnp.zeros((bq, bq), F32)
        for h in range(N_IDX_HEADS):
            d = jnp.dot(kic, qit_ref[h * IDX_HEAD_DIM:(h + 1) * IDX_HEAD_DIM, :],
                        preferred_element_type=F32)
            acc = acc + w[h:h + 1, :] * jnp.maximum(d, 0.0)
        return acc

    def score_chunk(c, carry):
        sc_ref[c] = scores(c)
        return carry

    lax.fori_loop(0, i, score_chunk, 0)
    sc_ref[i] = jnp.where(causal, scores(i), -jnp.inf)

    def lowest_highest(c, lh):
        x = sc_ref[c]
        return (jnp.minimum(lh[0], _reduce_row_groups(x, jnp.minimum)),
                jnp.maximum(lh[1], _reduce_row_groups(x, jnp.maximum)))

    lo8, hi8 = lax.fori_loop(0, i, lowest_highest, (jnp.full((SUBLANES, bq), jnp.inf, F32),
                                                    jnp.full((SUBLANES, bq), -jnp.inf, F32)))
    diag = sc_ref[i]
    lo8 = jnp.minimum(lo8, _reduce_row_groups(jnp.where(causal, diag, jnp.inf), jnp.minimum))
    hi8 = jnp.maximum(hi8, _reduce_row_groups(diag, jnp.maximum))
    lo0 = jnp.min(lo8, axis=0, keepdims=True)
    hi0 = jnp.max(hi8, axis=0, keepdims=True)

    def count_ge(cand):
        def body(c, cnt):
            return cnt + _reduce_row_groups(jnp.where(sc_ref[c] >= cand, 1.0, 0.0), jnp.add)
        cnt = lax.fori_loop(0, i + 1, body, jnp.zeros((SUBLANES, bq), F32))
        return jnp.sum(cnt, axis=0, keepdims=True)

    n_keys = (i * bq + 1 + lax.broadcasted_iota(I32, (1, bq), 1)).astype(F32)
    few_keys = n_keys < float(top_k)

    def bisect(state):
        it, lo, hi, settled = state
        mid = lo + (hi - lo) * 0.5
        cnt = count_ge(mid)
        enough = cnt >= float(top_k)
        open_ = settled == 0.0
        lo_new = jnp.where(open_ & enough, mid, lo)
        hi_new = jnp.where(open_ & jnp.logical_not(enough), mid, hi)
        done = (cnt == float(top_k)) | (mid <= lo) | (mid >= hi)
        return it + 1, lo_new, hi_new, jnp.where(done, 1.0, settled)

    def unsettled(state):
        it, _, _, settled = state
        return (it < SEARCH_STEPS_MAX) & (jnp.min(settled) == 0.0)

    _, lo, _, _ = lax.while_loop(
        unsettled, bisect, (jnp.int32(0), lo0, hi0, jnp.where(few_keys, 1.0, 0.0)))
    thresh = jnp.where(few_keys, -jnp.inf, lo)

    def mask_chunk(c, carry):
        sc_ref[c] = jnp.where(sc_ref[c] >= thresh, 0.0, NEG)
        return carry

    lax.fori_loop(0, i, mask_chunk, 0)
    sc_ref[i] = jnp.where(causal, jnp.where(sc_ref[i] >= thresh, 0.0, NEG), NEG)

    scale = ATTN_HEAD_DIM ** -0.5
    m_ref[...] = jnp.full(m_ref.shape, NEG, F32)
    l_ref[...] = jnp.zeros(l_ref.shape, F32)
    acc_ref[...] = jnp.zeros(acc_ref.shape, F32)

    def attend(c, near):
        mbias = sc_ref[c]
        m_all, l_all = m_ref[...], l_ref[...]
        m_rows, l_rows = [], []

        def qk(h):
            cols = slice(h * ATTN_HEAD_DIM, (h + 1) * ATTN_HEAD_DIM)
            return jnp.dot(k_ref[rows(c), cols], qt_ref[cols, :], preferred_element_type=F32)

        ahead = N_ATTN_HEADS
        s_raw = [qk(h) for h in range(ahead)]
        for h in range(N_ATTN_HEADS):
            cols = slice(h * ATTN_HEAD_DIM, (h + 1) * ATTN_HEAD_DIM)
            s = s_raw[h] * scale + mbias
            if h + ahead < N_ATTN_HEADS:
                s_raw.append(qk(h + ahead))
            if near is not None:
                s = s + bias_ref[h, near * bq:(near + 1) * bq, :]
            m_old = m_all[h:h + 1, :]
            m_new = jnp.maximum(m_old, jnp.max(_reduce_row_groups(s, jnp.maximum), axis=0, keepdims=True))
            p = jnp.exp(s - m_new)
            a = jnp.exp(m_old - m_new)
            m_rows.append(m_new)
            l_rows.append(a * l_all[h:h + 1, :]
                          + jnp.sum(_reduce_row_groups(p, jnp.add), axis=0, keepdims=True))
            pv = lax.dot_general(v_ref[rows(c), cols], p.astype(BF16), TN_DIMS, preferred_element_type=F32)
            acc_ref[h] = a * acc_ref[h] + pv
        m_ref[...] = jnp.concatenate(m_rows, axis=0)
        l_ref[...] = jnp.concatenate(l_rows, axis=0)

    def far_chunk(c, carry):
        attend(c, None)
        return carry

    lax.fori_loop(0, jnp.maximum(i - 1, 0), far_chunk, 0)

    @pl.when(i > 0)
    def _():
        attend(i - 1, 0)

    attend(i, 1)
    for h in range(N_ATTN_HEADS):
        out_t = acc_ref[h] / l_ref[h:h + 1, :]
        o_ref[:, h * ATTN_HEAD_DIM:(h + 1) * ATTN_HEAD_DIM] = out_t.T.astype(o_ref.dtype)


def _rel_bucket(dist):
    n = jnp.maximum(dist, 0)
    max_exact = N_REL_BUCKETS // 2
    nf = jnp.maximum(n, 1).astype(F32)
    large = max_exact + (jnp.log(nf / max_exact) / math.log(REL_MAX_DISTANCE / max_exact)
                         * (N_REL_BUCKETS - max_exact)).astype(I32)
    large = jnp.minimum(large, N_REL_BUCKETS - 1)
    return jnp.where(n < max_exact, n, large)


def _near_bias_tiles(rel_bias, bq):
    assert bq >= REL_MAX_DISTANCE
    key = jnp.arange(2 * bq, dtype=I32)[:, None]
    qry = jnp.arange(bq, dtype=I32)[None, :]
    bucket = _rel_bucket(qry + bq - key)
    rel = (rel_bias - rel_bias[N_REL_BUCKETS - 1]).T
    tiles = jnp.broadcast_to(rel[:, 0][:, None, None], (rel.shape[0],) + bucket.shape)
    for b in range(1, N_REL_BUCKETS):
        tiles = jnp.where(bucket[None] >= b, rel[:, b][:, None, None], tiles)
    return tiles


def _dsa_attention(qt, kv, ki_b, wt, rel_bias, bq):
    S = kv.shape[0]
    top_k = min(TOPK_MAX, S // 4)
    n_blk = S // bq
    bias = _near_bias_tiles(rel_bias, bq)
    kernel = functools.partial(_dsa_kernel, bq=bq, top_k=top_k)
    return pl.pallas_call(
        kernel,
        grid=(n_blk,),
        in_specs=[pl.BlockSpec((N_IDX_HEADS * IDX_HEAD_DIM, bq), lambda i: (1, i)),
                  pl.BlockSpec((N_IDX_HEADS, bq), lambda i: (0, i)),
                  pl.BlockSpec((ATTN_WIDTH, bq), lambda i: (0, i)),
                  _resident((S, IDX_HEAD_DIM), lambda i: (0, 0)),
                  _resident((S, ATTN_WIDTH), lambda i: (0, 0)),
                  _resident((S, ATTN_WIDTH), lambda i: (0, 1)),
                  _resident((N_ATTN_HEADS, 2 * bq, bq), lambda i: (0, 0, 0))],
        out_specs=pl.BlockSpec((bq, ATTN_WIDTH), lambda i: (i, 0)),
        out_shape=jax.ShapeDtypeStruct((S, ATTN_WIDTH), BF16),
        scratch_shapes=[pltpu.VMEM((n_blk, bq, bq), F32),
                        pltpu.VMEM((N_ATTN_HEADS, bq), F32),
                        pltpu.VMEM((N_ATTN_HEADS, bq), F32),
                        pltpu.VMEM((N_ATTN_HEADS, ATTN_HEAD_DIM, bq), F32)],
        compiler_params=_params("arbitrary"),
        name="dsa_attention",
    )(qt, wt, qt, ki_b, kv, kv, bias)


def _sgu_kernel(gu_ref, gv_ref, g_ref, b_ref, ws_ref, bs_ref, o_ref, *, n_chunk):
    u = _gelu(gu_ref[...])
    v = _layer_norm(_gelu(gv_ref[...]), g_ref[...], b_ref[...]).astype(BF16)
    r = lax.broadcasted_iota(I32, (GMLP_CHUNK, GMLP_CHUNK), 0)
    c = lax.broadcasted_iota(I32, (GMLP_CHUNK, GMLP_CHUNK), 1)
    for g in range(GMLP_GROUPS):
        wg = jnp.where(c <= r, ws_ref[g], 0.0).astype(BF16)
        bcol = bs_ref[:, g:g + 1]
        cols = slice(g * LANES, (g + 1) * LANES)
        for n in range(n_chunk):
            rws = slice(n * GMLP_CHUNK, (n + 1) * GMLP_CHUNK)
            mixed = jnp.dot(wg, v[rws, cols], preferred_element_type=F32) + bcol
            o_ref[rws, cols] = (u[rws, cols] * mixed).astype(o_ref.dtype)


def _sgu(zg, ln_g, ln_b, w_s, b_s, tm):
    S = zg.shape[0]
    kernel = functools.partial(_sgu_kernel, n_chunk=tm // GMLP_CHUNK)
    return pl.pallas_call(
        kernel,
        grid=(S // tm,),
        in_specs=[pl.BlockSpec((tm, GMLP_WIDTH), lambda i: (i, 0)),
                  pl.BlockSpec((tm, GMLP_WIDTH), lambda i: (i, 1)),
                  pl.BlockSpec((1, GMLP_WIDTH), lambda i: (0, 0)),
                  pl.BlockSpec((1, GMLP_WIDTH), lambda i: (0, 0)),
                  pl.BlockSpec((GMLP_GROUPS, GMLP_CHUNK, GMLP_CHUNK), lambda i: (0, 0, 0)),
                  pl.BlockSpec((GMLP_CHUNK, GMLP_GROUPS), lambda i: (0, 0))],
        out_specs=pl.BlockSpec((tm, GMLP_WIDTH), lambda i: (i, 0)),
        out_shape=jax.ShapeDtypeStruct((S, GMLP_WIDTH), BF16),
        compiler_params=_params("parallel"),
        name="gmlp_sgu",
    )(zg, zg, ln_g.reshape(1, -1), ln_b.reshape(1, -1), w_s, b_s.T)


def _merge_kernel(at_ref, gm_ref, ga_ref, gg_ref, h_ref, wa_ref, wg_ref, wo_ref, g_ref, b_ref,
                  h1_ref, h1b_ref, h1bt_ref):
    a1 = jnp.dot(at_ref[...], wa_ref[...], preferred_element_type=F32)
    a2 = jnp.dot(gm_ref[...], wg_ref[...], preferred_element_type=F32)
    merged = jax.nn.sigmoid(ga_ref[...]) * a1 + jax.nn.sigmoid(gg_ref[...]) * a2
    y = DEEPNORM_ALPHA * h_ref[...] + jnp.dot(merged.astype(BF16), wo_ref[...], preferred_element_type=F32)
    h1 = _layer_norm(y, g_ref[...], b_ref[...])
    h1_ref[...] = h1
    h1b_ref[...] = h1.astype(BF16)
    h1bt_ref[...] = h1.T.astype(BF16)


def _merge(attn, gm, zg, h, wa, wg, wo, ln_g, ln_b, tm):
    S, D = h.shape
    W = attn.shape[1]
    return pl.pallas_call(
        _merge_kernel,
        grid=(S // tm,),
        in_specs=[pl.BlockSpec((tm, W), lambda i: (i, 0)),
                  pl.BlockSpec((tm, W), lambda i: (i, 0)),
                  pl.BlockSpec((tm, D), lambda i: (i, 1)),
                  pl.BlockSpec((tm, D), lambda i: (i, 2)),
                  pl.BlockSpec((tm, D), lambda i: (i, 0)),
                  _resident((W, D), lambda i: (0, 0)),
                  _resident((W, D), lambda i: (0, 0)),
                  _resident((D, D), lambda i: (0, 0)),
                  pl.BlockSpec((1, D), lambda i: (0, 0)),
                  pl.BlockSpec((1, D), lambda i: (0, 0))],
        out_specs=[pl.BlockSpec((tm, D), lambda i: (i, 0)),
                   pl.BlockSpec((tm, D), lambda i: (i, 0)),
                   pl.BlockSpec((D, tm), lambda i: (0, i))],
        out_shape=[jax.ShapeDtypeStruct((S, D), F32), jax.ShapeDtypeStruct((S, D), BF16),
                   jax.ShapeDtypeStruct((D, S), BF16)],
        compiler_params=_params("parallel"),
        name="merge_ln1",
    )(attn, gm, zg, zg, h, wa, wg, wo, ln_g.reshape(1, D), ln_b.reshape(1, D))


def _peer_pairs():
    pairs = [(a, b) for a in range(PEER_TOPK) for b in range(PEER_TOPK) if (a + 1) * (b + 1) <= PEER_TOPK]
    return sorted(pairs, key=lambda ab: ab[0] * PEER_TOPK + ab[1])


def _extract_top(work, n_rounds, on_pick, exact):
    n = float(work.shape[0])
    iota = lax.broadcasted_iota(I32, work.shape, 0).astype(F32)
    for r in range(n_rounds):
        mx = jnp.max(work, axis=0, keepdims=True)
        pick = work == mx
        if exact:
            first = jnp.min(jnp.where(pick, iota, n), axis=0, keepdims=True)
            pick = iota == first
        on_pick(r, mx, pick)
        work = jnp.where(pick, -jnp.inf, work)


def _route_kernel(hb_ref, wq_ref, sk_ref, l1_ref, a_ref, r2_ref, b_ref, *, T):
    qt = jnp.dot(hb_ref[...], wq_ref[...], preferred_element_type=F32).T
    pairs = _peer_pairs()

    def route(h, exact):
        miscount = jnp.zeros((1, T), F32)
        s, rank, vals = [], [], []
        for p in range(2):
            idx = 2 * h + p
            sp = jnp.dot(sk_ref[idx], qt[idx * LANES:(idx + 1) * LANES, :].astype(BF16),
                         preferred_element_type=F32)
            state = {"rank": jnp.full(sp.shape, float(PEER_TOPK), F32), "vals": []}

            def on_pick(r, mx, pick, state=state):
                state["rank"] = jnp.where(pick, float(r), state["rank"])
                state["vals"].append(mx)

            _extract_top(sp, PEER_TOPK, on_pick, exact)
            n_ranked = jnp.sum(jnp.where(state["rank"] < float(PEER_TOPK), 1.0, 0.0), axis=0, keepdims=True)
            miscount = jnp.maximum(miscount, jnp.abs(n_ranked - float(PEER_TOPK)))
            s.append(sp)
            rank.append(state["rank"])
            vals.append(state["vals"])
        cand = jnp.concatenate([vals[0][a] + vals[1][b] for a, b in pairs], axis=0)
        picked = {"mask": jnp.zeros(cand.shape, jnp.bool_)}

        def on_pick_c(r, mx, pick, picked=picked):
            picked["mask"] = picked["mask"] | pick

        _extract_top(cand, PEER_TOPK, on_pick_c, exact)
        top = vals[0][0] + vals[1][0]
        selw = jnp.where(picked["mask"], jnp.exp(cand - top), 0.0)
        z = jnp.sum(selw, axis=0, keepdims=True)
        self32 = jnp.where(picked["mask"], 1.0, 0.0)
        miscount = jnp.maximum(miscount, jnp.abs(jnp.sum(self32, axis=0, keepdims=True) - float(PEER_TOPK)))
        l1 = jnp.zeros(s[0].shape, F32)
        pos = 0
        for a in range(PEER_TOPK):
            cnt = sum(1 for ab in pairs if ab[0] == a)
            n_a = jnp.sum(self32[pos:pos + cnt, :], axis=0, keepdims=True)
            pos += cnt
            l1 = jnp.where(rank[0] == float(a), n_a, l1)
        l1_ref[h] = l1
        a_ref[h] = jnp.exp(s[0] - vals[0][0])
        r2_ref[h] = rank[1].astype(r2_ref.dtype)
        b_ref[h] = (jnp.exp(s[1] - vals[1][0]) / z).astype(b_ref.dtype)
        return miscount

    miscount = functools.reduce(jnp.maximum, [route(h, exact=False) for h in range(PEER_HEADS)])

    @pl.when(jnp.max(miscount) > 0.0)
    def _():
        for h in range(PEER_HEADS):
            route(h, exact=True)


def _peer_route(h1b, wq, sk, T):
    S, D = h1b.shape
    kernel = functools.partial(_route_kernel, T=T)
    shp = jax.ShapeDtypeStruct((PEER_HEADS, PEER_N_KEYS, S), F32)
    shp_b = jax.ShapeDtypeStruct((PEER_HEADS, PEER_N_KEYS, S), BF16)
    spec = pl.BlockSpec((PEER_HEADS, PEER_N_KEYS, T), lambda t: (0, 0, t))
    return pl.pallas_call(
        kernel,
        grid=(S // T,),
        in_specs=[pl.BlockSpec((T, D), lambda t: (t, 0)),
                  _resident(wq.shape, lambda t: (0, 0)),
                  _resident(sk.shape, lambda t: (0, 0, 0))],
        out_specs=[spec, spec, spec, spec],
        out_shape=[shp, shp, shp_b, shp_b],
        compiler_params=_params("parallel"),
        name="peer_route",
    )(h1b, wq, sk)


def _peer_kernel(ht_ref, u_ref, v_ref, l1_ref, a_ref, r2_ref, b_ref, o_ref, p0_ref, p1_ref,
                 *, n_i1, ce, n_blk):
    e = pl.program_id(1)

    @pl.when(e == 0)
    def _():
        o_ref[...] = jnp.zeros_like(o_ref)
        p1_ref[...] = jnp.zeros_like(p1_ref)

    T = ht_ref.shape[1]
    n_slab = PEER_N_KEYS // PACKED_ROWS
    i1_per_chunk = ce // PEER_N_KEYS
    r2 = [r2_ref[h] for h in range(PEER_HEADS)]
    b = [b_ref[h] for h in range(PEER_HEADS)]

    def gates(c):
        tiles = []
        for j in range(c * i1_per_chunk, (c + 1) * i1_per_chunk):
            gate = [None] * n_slab
            for h in range(PEER_HEADS):
                l1 = jnp.broadcast_to(l1_ref[h, j:j + 1, :], (PACKED_ROWS, T)).astype(BF16)
                a = jnp.broadcast_to(a_ref[h, j:j + 1, :], (PACKED_ROWS, T)).astype(BF16)
                for s in range(n_slab):
                    slab = slice(s * PACKED_ROWS, (s + 1) * PACKED_ROWS)
                    val = jnp.where(r2[h][slab] < l1, b[h][slab], 0.0) * a
                    gate[s] = val if gate[s] is None else gate[s] + val
            tiles.extend(gate)
        return tiles

    n_chunk = n_i1 * PEER_N_KEYS // ce
    d_out = o_ref.shape[1] // n_chunk

    def output_part(p_ref, c):
        cols = slice(c * d_out, (c + 1) * d_out)
        o_ref[:, cols] += jnp.dot(p_ref[...], v_ref[:, cols], preferred_element_type=F32)

    def step(p_write, p_read):
        for c in range(n_chunk):
            act = jnp.dot(u_ref[c * ce:(c + 1) * ce, :], ht_ref[...], preferred_element_type=F32)
            output_part(p_read, c)
            gel = _gelu(act).astype(BF16)
            g = gates(c)
            pc = jnp.concatenate([gel[k * PACKED_ROWS:(k + 1) * PACKED_ROWS] * g[k]
                                  for k in range(len(g))], axis=0)
            p_write[:, c * ce:(c + 1) * ce] = pc.T

    def tail(p_read):
        for c in range(n_chunk):
            output_part(p_read, c)

    pl.when((e < n_blk) & (e % 2 == 0))(lambda: step(p0_ref, p1_ref))
    pl.when((e < n_blk) & (e % 2 == 1))(lambda: step(p1_ref, p0_ref))
    pl.when(e == n_blk)(lambda: tail(p1_ref if (n_blk - 1) % 2 else p0_ref))


def _peer_experts(h1bt, u_b, v_b, l1, a, r2, b, T, eb):
    D, S = h1bt.shape
    E = u_b.shape[0]
    n_i1 = eb // PEER_N_KEYS
    n_blk = E // eb
    last = n_blk - 1
    ce = MXU_DIM // 2
    kernel = functools.partial(_peer_kernel, n_i1=n_i1, ce=ce, n_blk=n_blk)
    row_spec = pl.BlockSpec((PEER_HEADS, n_i1, T), lambda t, e: (0, jnp.minimum(e, last), t))
    full_spec = pl.BlockSpec((PEER_HEADS, PEER_N_KEYS, T), lambda t, e: (0, 0, t))
    return pl.pallas_call(
        kernel,
        grid=(S // T, n_blk + 1),
        in_specs=[pl.BlockSpec((D, T), lambda t, e: (0, t)),
                  pl.BlockSpec((eb, D), lambda t, e: (jnp.minimum(e, last), 0)),
                  pl.BlockSpec((eb, D), lambda t, e: (jnp.maximum(e - 1, 0), 0)),
                  row_spec, row_spec, full_spec, full_spec],
        out_specs=pl.BlockSpec((T, D), lambda t, e: (t, 0)),
        out_shape=jax.ShapeDtypeStruct((S, D), F32),
        scratch_shapes=[pltpu.VMEM((T, eb), BF16), pltpu.VMEM((T, eb), BF16)],
        compiler_params=_params("parallel", "arbitrary"),
        name="peer_experts",
    )(h1bt, u_b, v_b, l1, a, r2, b)


def _ple_kernel(h1_ref, peer_ref, p_ref, wg_ref, wp_ref, g_ref, b_ref, o_ref):
    r = DEEPNORM_ALPHA * h1_ref[...] + peer_ref[...]
    gate = jax.nn.sigmoid(jnp.dot(r.astype(BF16), wg_ref[...], preferred_element_type=F32))
    proj = jnp.dot(p_ref[...], wp_ref[...], preferred_element_type=F32)
    o_ref[...] = _layer_norm(r + gate * proj, g_ref[...], b_ref[...])


def _ple(h1, peer, p_b, wg, wp, ln_g, ln_b, tm):
    S, D = h1.shape
    P = p_b.shape[1]
    return pl.pallas_call(
        _ple_kernel,
        grid=(S // tm,),
        in_specs=[pl.BlockSpec((tm, D), lambda i: (i, 0)),
                  pl.BlockSpec((tm, D), lambda i: (i, 0)),
                  pl.BlockSpec((tm, P), lambda i: (i, 0)),
                  _resident((D, D), lambda i: (0, 0)),
                  _resident((P, D), lambda i: (0, 0)),
                  pl.BlockSpec((1, D), lambda i: (0, 0)),
                  pl.BlockSpec((1, D), lambda i: (0, 0))],
        out_specs=pl.BlockSpec((tm, D), lambda i: (i, 0)),
        out_shape=jax.ShapeDtypeStruct((S, D), F32),
        compiler_params=_params("parallel"),
        name="ple_ln2",
    )(h1, peer, p_b, wg, wp, ln_g.reshape(1, D), ln_b.reshape(1, D))


def _layer(x, p, ln_emb_g, ln_emb_b, rel_bias, w_in, gmlp_ln_g, gmlp_ln_b, gmlp_w_s, gmlp_b_s,
           w_br_attn, w_br_gmlp, w_mix_out, ln1_g, ln1_b, peer_w_q, peer_sub_keys, peer_u, peer_v,
           ple_w_proj, ple_w_gate, ln2_g, ln2_b):
    S, D = x.shape
    tm = min(512, S)
    o_k, o_v, o_qi = ATTN_WIDTH, 2 * ATTN_WIDTH, 3 * ATTN_WIDTH
    o_ki = o_qi + N_IDX_HEADS * IDX_HEAD_DIM
    o_gu = o_ki + IDX_HEAD_DIM + N_IDX_HEADS

    h, hb = _ln_embed(x, ln_emb_g, ln_emb_b, tm)
    wqt = jnp.concatenate([w_in[:, :o_k], w_in[:, o_qi:o_ki]], axis=1).T.astype(BF16)
    w1 = w_in[:, o_k:o_qi].astype(BF16)
    w2 = jnp.pad(w_in[:, o_ki:o_gu], ((0, 0), (0, LANES - (o_gu - o_ki)))).astype(BF16)
    w3 = w_in[:, o_gu:].astype(BF16)
    qt = _matmul_t(wqt, hb, BF16, tm, "in_proj_q")
    kv = _matmul(hb, w1, BF16, min(1024, S), 512, "in_proj_kv")
    kw = _matmul(hb, w2, F32, min(1024, S), LANES, "in_proj_index")
    zg = _matmul(hb, w3, F32, min(1024, S), 512, "in_proj_gmlp")

    attn = _dsa_attention(qt, kv, kw[:, :IDX_HEAD_DIM].astype(BF16),
                          kw[:, IDX_HEAD_DIM:IDX_HEAD_DIM + N_IDX_HEADS].T, rel_bias, bq=256)
    gm = _sgu(zg, gmlp_ln_g, gmlp_ln_b, gmlp_w_s, gmlp_b_s, tm)
    h1, h1b, h1bt = _merge(attn, gm, zg, h, w_br_attn.astype(BF16), w_br_gmlp.astype(BF16),
                           w_mix_out.astype(BF16), ln1_g, ln1_b, min(256, S))

    sk = peer_sub_keys.reshape(PEER_HEADS * 2, PEER_N_KEYS, -1).astype(BF16)
    l1, a, r2, b = _peer_route(h1b, peer_w_q.astype(BF16), sk, min(256, S))
    peer = _peer_experts(h1bt, peer_u.astype(BF16), peer_v.astype(BF16), l1, a, r2, b,
                         T=min(512, S), eb=1024)
    return _ple(h1, peer, p.astype(BF16), ple_w_gate.astype(BF16), ple_w_proj.astype(BF16),
                ln2_g, ln2_b, min(256, S))


def kernel(x, p, positions, ln_emb_g, ln_emb_b, rel_bias, w_in, gmlp_ln_g, gmlp_ln_b, gmlp_w_s, gmlp_b_s, w_br_attn, w_br_gmlp, w_mix_out, ln1_g, ln1_b, peer_w_q, peer_sub_keys, peer_u, peer_v, ple_w_proj, ple_w_gate, ln2_g, ln2_b):
    del positions
    assert x.shape[0] == 1 and w_in.shape[0] == DEPTH
    out = _layer(x[0], p[0, 0], ln_emb_g, ln_emb_b, rel_bias, w_in[0], gmlp_ln_g[0], gmlp_ln_b[0],
                 gmlp_w_s[0], gmlp_b_s[0], w_br_attn[0], w_br_gmlp[0], w_mix_out[0], ln1_g[0],
                 ln1_b[0], peer_w_q[0], peer_sub_keys[0], peer_u[0], peer_v[0], ple_w_proj[0],
                 ple_w_gate[0], ln2_g[0], ln2_b[0])
    return out[None]
```

```python
import functools
import math

import jax
import jax.numpy as jnp
from jax import lax
from jax.experimental import pallas as pl
from jax.experimental.pallas import tpu as pltpu

F32 = jnp.float32
BF16 = jnp.bfloat16
I32 = jnp.int32

LN_EPS = 1e-5
DEPTH = 1
DEEPNORM_ALPHA = (2.0 * DEPTH) ** 0.25

N_ATTN_HEADS = 8
ATTN_HEAD_DIM = 128
ATTN_WIDTH = N_ATTN_HEADS * ATTN_HEAD_DIM
N_IDX_HEADS = 16
IDX_HEAD_DIM = 64
TOPK_MAX = 256
GMLP_CHUNK = 128
GMLP_GROUPS = 8
GMLP_WIDTH = 1024
N_REL_BUCKETS = 32
REL_MAX_DISTANCE = 128
PEER_HEADS = 8
PEER_N_KEYS = 128
PEER_TOPK = 16

LANES = 128
SUBLANES = 8
PACKED_ROWS = 2 * SUBLANES
MXU_DIM = 256
NEG = -1e30
SEARCH_STEPS_MAX = 64
NT_DIMS = (((1,), (1,)), ((), ()))
TN_DIMS = (((0,), (0,)), ((), ()))
VMEM_LIMIT = 56 * 1024 * 1024


def _params(*sem):
    return pltpu.CompilerParams(dimension_semantics=sem, vmem_limit_bytes=VMEM_LIMIT)


def _resident(shape, index_map):
    return pl.BlockSpec(shape, index_map, pipeline_mode=pl.Buffered(1))


def _gelu(x):
    return 0.5 * x * (1.0 + jnp.tanh(0.7978845608028654 * (x + 0.044715 * (x * x * x))))


def _layer_norm(x, g, b):
    mu = jnp.mean(x, axis=-1, keepdims=True)
    xc = x - mu
    var = jnp.mean(xc * xc, axis=-1, keepdims=True)
    return xc * lax.rsqrt(var + LN_EPS) * g + b


def _ln_kernel(x_ref, g_ref, b_ref, h_ref, hb_ref):
    y = _layer_norm(x_ref[...], g_ref[...], b_ref[...])
    h_ref[...] = y
    hb_ref[...] = y.astype(BF16)


def _ln_embed(x, g, b, tm):
    S, D = x.shape
    return pl.pallas_call(
        _ln_kernel,
        grid=(S // tm,),
        in_specs=[pl.BlockSpec((tm, D), lambda i: (i, 0)),
                  pl.BlockSpec((1, D), lambda i: (0, 0)),
                  pl.BlockSpec((1, D), lambda i: (0, 0))],
        out_specs=[pl.BlockSpec((tm, D), lambda i: (i, 0)),
                   pl.BlockSpec((tm, D), lambda i: (i, 0))],
        out_shape=[jax.ShapeDtypeStruct((S, D), F32), jax.ShapeDtypeStruct((S, D), BF16)],
        compiler_params=_params("parallel"),
        name="ln_embed",
    )(x, g.reshape(1, D), b.reshape(1, D))


def _mm_kernel(a_ref, w_ref, o_ref):
    o_ref[...] = jnp.dot(a_ref[...], w_ref[...], preferred_element_type=F32).astype(o_ref.dtype)


def _matmul(a, w, out_dtype, tm, tn, name):
    M, K = a.shape
    N = w.shape[1]
    return pl.pallas_call(
        _mm_kernel,
        grid=(M // tm, N // tn),
        in_specs=[pl.BlockSpec((tm, K), lambda i, j: (i, 0)),
                  pl.BlockSpec((K, tn), lambda i, j: (0, j))],
        out_specs=pl.BlockSpec((tm, tn), lambda i, j: (i, j)),
        out_shape=jax.ShapeDtypeStruct((M, N), out_dtype),
        compiler_params=_params("parallel", "arbitrary"),
        name=name,
    )(a, w)


def _mm_t_kernel(wt_ref, a_ref, o_ref):
    o_ref[...] = lax.dot_general(wt_ref[...], a_ref[...], NT_DIMS,
                                 preferred_element_type=F32).astype(o_ref.dtype)


def _matmul_t(wt, a, out_dtype, tm, name):
    M, K = a.shape
    N = wt.shape[0]
    return pl.pallas_call(
        _mm_t_kernel,
        grid=(M // tm,),
        in_specs=[_resident((N, K), lambda i: (0, 0)),
                  pl.BlockSpec((tm, K), lambda i: (i, 0))],
        out_specs=pl.BlockSpec((N, tm), lambda i: (0, i)),
        out_shape=jax.ShapeDtypeStruct((N, M), out_dtype),
        compiler_params=_params("parallel"),
        name=name,
    )(wt, a)


def _reduce_row_groups(t, op):
    parts = [t[j * SUBLANES:(j + 1) * SUBLANES, :] for j in range(t.shape[0] // SUBLANES)]
    while len(parts) > 1:
        parts = [op(parts[j], parts[j + 1]) for j in range(0, len(parts) - 1, 2)] + (
            [parts[-1]] if len(parts) % 2 else [])
    return parts[0]


def _dsa_kernel(qit_ref, wt_ref, qt_ref, ki_ref, k_ref, v_ref, bias_ref, o_ref,
                sc_ref, m_ref, l_ref, acc_ref, *, bq, top_k):
    i = pl.program_id(0)
    w = wt_ref[...] * ((IDX_HEAD_DIM ** -0.5) * (N_IDX_HEADS ** -0.5))
    key_i = lax.broadcasted_iota(I32, (bq, bq), 0)
    qry_i = lax.broadcasted_iota(I32, (bq, bq), 1)
    causal = key_i <= qry_i

    def rows(c):
        return pl.ds(pl.multiple_of(c * bq, bq), bq)

    def scores(c):
        kic = ki_ref[rows(c), :]
        acc = jnp.zeros((bq, bq), F32)
        for h in range(N_IDX_HEADS):
            d = jnp.dot(kic, qit_ref[h * IDX_HEAD_DIM:(h + 1) * IDX_HEAD_DIM, :],
                        preferred_element_type=F32)
            acc = acc + w[h:h + 1, :] * jnp.maximum(d, 0.0)
        return acc

    def score_chunk(c, carry):
        sc_ref[c] = scores(c)
        return carry

    lax.fori_loop(0, i, score_chunk, 0)
    sc_ref[i] = jnp.where(causal, scores(i), -jnp.inf)

    def lowest_highest(c, lh):
        x = sc_ref[c]
        return (jnp.minimum(lh[0], _reduce_row_groups(x, jnp.minimum)),
                jnp.maximum(lh[1], _reduce_row_groups(x, jnp.maximum)))

    lo8, hi8 = lax.fori_loop(0, i, lowest_highest, (jnp.full((SUBLANES, bq), jnp.inf, F32),
                                                    jnp.full((SUBLANES, bq), -jnp.inf, F32)))
    diag = sc_ref[i]
    lo8 = jnp.minimum(lo8, _reduce_row_groups(jnp.where(causal, diag, jnp.inf), jnp.minimum))
    hi8 = jnp.maximum(hi8, _reduce_row_groups(diag, jnp.maximum))
    lo0 = jnp.min(lo8, axis=0, keepdims=True)
    hi0 = jnp.max(hi8, axis=0, keepdims=True)

    def count_ge(cand):
        def body(c, cnt):
            return cnt + _reduce_row_groups(jnp.where(sc_ref[c] >= cand, 1.0, 0.0), jnp.add)
        cnt = lax.fori_loop(0, i + 1, body, jnp.zeros((SUBLANES, bq), F32))
        return jnp.sum(cnt, axis=0, keepdims=True)

    n_keys = (i * bq + 1 + lax.broadcasted_iota(I32, (1, bq), 1)).astype(F32)
    few_keys = n_keys < float(top_k)

    def bisect(state):
        it, lo, hi, settled = state
        mid = lo + (hi - lo) * 0.5
        cnt = count_ge(mid)
        enough = cnt >= float(top_k)
        open_ = settled == 0.0
        lo_new = jnp.where(open_ & enough, mid, lo)
        hi_new = jnp.where(open_ & jnp.logical_not(enough), mid, hi)
        done = (cnt == float(top_k)) | (mid <= lo) | (mid >= hi)
        return it + 1, lo_new, hi_new, jnp.where(done, 1.0, settled)

    def unsettled(state):
        it, _, _, settled = state
        return (it < SEARCH_STEPS_MAX) & (jnp.min(settled) == 0.0)

    _, lo, _, _ = lax.while_loop(
        unsettled, bisect, (jnp.int32(0), lo0, hi0, jnp.where(few_keys, 1.0, 0.0)))
    thresh = jnp.where(few_keys, -jnp.inf, lo)

    def mask_chunk(c, carry):
        sc_ref[c] = jnp.where(sc_ref[c] >= thresh, 0.0, NEG)
        return carry

    lax.fori_loop(0, i, mask_chunk, 0)
    sc_ref[i] = jnp.where(causal, jnp.where(sc_ref[i] >= thresh, 0.0, NEG), NEG)

    scale = ATTN_HEAD_DIM ** -0.5
    m_ref[...] = jnp.full(m_ref.shape, NEG, F32)
    l_ref[...] = jnp.zeros(l_ref.shape, F32)
    acc_ref[...] = jnp.zeros(acc_ref.shape, F32)

    def attend(c, near):
        mbias = sc_ref[c]
        m_all, l_all = m_ref[...], l_ref[...]
        m_rows, l_rows = [], []

        def qk(h):
            cols = slice(h * ATTN_HEAD_DIM, (h + 1) * ATTN_HEAD_DIM)
            return jnp.dot(k_ref[rows(c), cols], qt_ref[cols, :], preferred_element_type=F32)

        ahead = N_ATTN_HEADS
        s_raw = [qk(h) for h in range(ahead)]
        for h in range(N_ATTN_HEADS):
            cols = slice(h * ATTN_HEAD_DIM, (h + 1) * ATTN_HEAD_DIM)
            s = s_raw[h] * scale + mbias
            if h + ahead < N_ATTN_HEADS:
                s_raw.append(qk(h + ahead))
            if near is not None:
                s = s + bias_ref[h, near * bq:(near + 1) * bq, :]
            m_old = m_all[h:h + 1, :]
            m_new = jnp.maximum(m_old, jnp.max(_reduce_row_groups(s, jnp.maximum), axis=0, keepdims=True))
            p = jnp.exp(s - m_new)
            a = jnp.exp(m_old - m_new)
            m_rows.append(m_new)
            l_rows.append(a * l_all[h:h + 1, :]
                          + jnp.sum(_reduce_row_groups(p, jnp.add), axis=0, keepdims=True))
            pv = lax.dot_general(v_ref[rows(c), cols], p.astype(BF16), TN_DIMS, preferred_element_type=F32)
            acc_ref[h] = a * acc_ref[h] + pv
        m_ref[...] = jnp.concatenate(m_rows, axis=0)
        l_ref[...] = jnp.concatenate(l_rows, axis=0)

    def far_chunk(c, carry):
        attend(c, None)
        return carry

    lax.fori_loop(0, jnp.maximum(i - 1, 0), far_chunk, 0)

    @pl.when(i > 0)
    def _():
        attend(i - 1, 0)

    attend(i, 1)
    for h in range(N_ATTN_HEADS):
        out_t = acc_ref[h] / l_ref[h:h + 1, :]
        o_ref[:, h * ATTN_HEAD_DIM:(h + 1) * ATTN_HEAD_DIM] = out_t.T.astype(o_ref.dtype)


def _rel_bucket(dist):
    n = jnp.maximum(dist, 0)
    max_exact = N_REL_BUCKETS // 2
    nf = jnp.maximum(n, 1).astype(F32)
    large = max_exact + (jnp.log(nf / max_exact) / math.log(REL_MAX_DISTANCE / max_exact)
                         * (N_REL_BUCKETS - max_exact)).astype(I32)
    large = jnp.minimum(large, N_REL_BUCKETS - 1)
    return jnp.where(n < max_exact, n, large)


def _near_bias_tiles(rel_bias, bq):
    assert bq >= REL_MAX_DISTANCE
    key = jnp.arange(2 * bq, dtype=I32)[:, None]
    qry = jnp.arange(bq, dtype=I32)[None, :]
    bucket = _rel_bucket(qry + bq - key)
    rel = (rel_bias - rel_bias[N_REL_BUCKETS - 1]).T
    tiles = jnp.broadcast_to(rel[:, 0][:, None, None], (rel.shape[0],) + bucket.shape)
    for b in range(1, N_REL_BUCKETS):
        tiles = jnp.where(bucket[None] >= b, rel[:, b][:, None, None], tiles)
    return tiles


def _dsa_attention(qt, kv, ki_b, wt, rel_bias, bq):
    S = kv.shape[0]
    top_k = min(TOPK_MAX, S // 4)
    n_blk = S // bq
    bias = _near_bias_tiles(rel_bias, bq)
    kernel = functools.partial(_dsa_kernel, bq=bq, top_k=top_k)
    return pl.pallas_call(
        kernel,
        grid=(n_blk,),
        in_specs=[pl.BlockSpec((N_IDX_HEADS * IDX_HEAD_DIM, bq), lambda i: (1, i)),
                  pl.BlockSpec((N_IDX_HEADS, bq), lambda i: (0, i)),
                  pl.BlockSpec((ATTN_WIDTH, bq), lambda i: (0, i)),
                  _resident((S, IDX_HEAD_DIM), lambda i: (0, 0)),
                  _resident((S, ATTN_WIDTH), lambda i: (0, 0)),
                  _resident((S, ATTN_WIDTH), lambda i: (0, 1)),
                  _resident((N_ATTN_HEADS, 2 * bq, bq), lambda i: (0, 0, 0))],
        out_specs=pl.BlockSpec((bq, ATTN_WIDTH), lambda i: (i, 0)),
        out_shape=jax.ShapeDtypeStruct((S, ATTN_WIDTH), BF16),
        scratch_shapes=[pltpu.VMEM((n_blk, bq, bq), F32),
                        pltpu.VMEM((N_ATTN_HEADS, bq), F32),
                        pltpu.VMEM((N_ATTN_HEADS, bq), F32),
                        pltpu.VMEM((N_ATTN_HEADS, ATTN_HEAD_DIM, bq), F32)],
        compiler_params=_params("arbitrary"),
        name="dsa_attention",
    )(qt, wt, qt, ki_b, kv, kv, bias)


def _sgu_kernel(gu_ref, gv_ref, g_ref, b_ref, ws_ref, bs_ref, o_ref, *, n_chunk):
    u = _gelu(gu_ref[...])
    v = _layer_norm(_gelu(gv_ref[...]), g_ref[...], b_ref[...]).astype(BF16)
    r = lax.broadcasted_iota(I32, (GMLP_CHUNK, GMLP_CHUNK), 0)
    c = lax.broadcasted_iota(I32, (GMLP_CHUNK, GMLP_CHUNK), 1)
    for g in range(GMLP_GROUPS):
        wg = jnp.where(c <= r, ws_ref[g], 0.0).astype(BF16)
        bcol = bs_ref[:, g:g + 1]
        cols = slice(g * LANES, (g + 1) * LANES)
        for n in range(n_chunk):
            rws = slice(n * GMLP_CHUNK, (n + 1) * GMLP_CHUNK)
            mixed = jnp.dot(wg, v[rws, cols], preferred_element_type=F32) + bcol
            o_ref[rws, cols] = (u[rws, cols] * mixed).astype(o_ref.dtype)


def _sgu(zg, ln_g, ln_b, w_s, b_s, tm):
    S = zg.shape[0]
    kernel = functools.partial(_sgu_kernel, n_chunk=tm // GMLP_CHUNK)
    return pl.pallas_call(
        kernel,
        grid=(S // tm,),
        in_specs=[pl.BlockSpec((tm, GMLP_WIDTH), lambda i: (i, 0)),
                  pl.BlockSpec((tm, GMLP_WIDTH), lambda i: (i, 1)),
                  pl.BlockSpec((1, GMLP_WIDTH), lambda i: (0, 0)),
                  pl.BlockSpec((1, GMLP_WIDTH), lambda i: (0, 0)),
                  pl.BlockSpec((GMLP_GROUPS, GMLP_CHUNK, GMLP_CHUNK), lambda i: (0, 0, 0)),
                  pl.BlockSpec((GMLP_CHUNK, GMLP_GROUPS), lambda i: (0, 0))],
        out_specs=pl.BlockSpec((tm, GMLP_WIDTH), lambda i: (i, 0)),
        out_shape=jax.ShapeDtypeStruct((S, GMLP_WIDTH), BF16),
        compiler_params=_params("parallel"),
        name="gmlp_sgu",
    )(zg, zg, ln_g.reshape(1, -1), ln_b.reshape(1, -1), w_s, b_s.T)


def _merge_kernel(at_ref, gm_ref, ga_ref, gg_ref, h_ref, wa_ref, wg_ref, wo_ref, g_ref, b_ref,
                  h1_ref, h1b_ref, h1bt_ref):
    a1 = jnp.dot(at_ref[...], wa_ref[...], preferred_element_type=F32)
    a2 = jnp.dot(gm_ref[...], wg_ref[...], preferred_element_type=F32)
    merged = jax.nn.sigmoid(ga_ref[...]) * a1 + jax.nn.sigmoid(gg_ref[...]) * a2
    y = DEEPNORM_ALPHA * h_ref[...] + jnp.dot(merged.astype(BF16), wo_ref[...], preferred_element_type=F32)
    h1 = _layer_norm(y, g_ref[...], b_ref[...])
    h1_ref[...] = h1
    h1b_ref[...] = h1.astype(BF16)
    h1bt_ref[...] = h1.T.astype(BF16)


def _merge(attn, gm, zg, h, wa, wg, wo, ln_g, ln_b, tm):
    S, D = h.shape
    W = attn.shape[1]
    return pl.pallas_call(
        _merge_kernel,
        grid=(S // tm,),
        in_specs=[pl.BlockSpec((tm, W), lambda i: (i, 0)),
                  pl.BlockSpec((tm, W), lambda i: (i, 0)),
                  pl.BlockSpec((tm, D), lambda i: (i, 1)),
                  pl.BlockSpec((tm, D), lambda i: (i, 2)),
                  pl.BlockSpec((tm, D), lambda i: (i, 0)),
                  _resident((W, D), lambda i: (0, 0)),
                  _resident((W, D), lambda i: (0, 0)),
                  _resident((D, D), lambda i: (0, 0)),
                  pl.BlockSpec((1, D), lambda i: (0, 0)),
                  pl.BlockSpec((1, D), lambda i: (0, 0))],
        out_specs=[pl.BlockSpec((tm, D), lambda i: (i, 0)),
                   pl.BlockSpec((tm, D), lambda i: (i, 0)),
                   pl.BlockSpec((D, tm), lambda i: (0, i))],
        out_shape=[jax.ShapeDtypeStruct((S, D), F32), jax.ShapeDtypeStruct((S, D), BF16),
                   jax.ShapeDtypeStruct((D, S), BF16)],
        compiler_params=_params("parallel"),
        name="merge_ln1",
    )(attn, gm, zg, zg, h, wa, wg, wo, ln_g.reshape(1, D), ln_b.reshape(1, D))


def _peer_pairs():
    pairs = [(a, b) for a in range(PEER_TOPK) for b in range(PEER_TOPK) if (a + 1) * (b + 1) <= PEER_TOPK]
    return sorted(pairs, key=lambda ab: ab[0] * PEER_TOPK + ab[1])


def _extract_top(work, n_rounds, on_pick, exact):
    n = float(work.shape[0])
    iota = lax.broadcasted_iota(I32, work.shape, 0).astype(F32)
    for r in range(n_rounds):
        mx = jnp.max(work, axis=0, keepdims=True)
        pick = work == mx
        if exact:
            first = jnp.min(jnp.where(pick, iota, n), axis=0, keepdims=True)
            pick = iota == first
        on_pick(r, mx, pick)
        work = jnp.where(pick, -jnp.inf, work)


def _route_kernel(hb_ref, wq_ref, sk_ref, u_ref, v_ref, l1_ref, a_ref, r2_ref, b_ref, ub_ref, vb_ref, *, T):
    ub_ref[...] = u_ref[...].astype(BF16)
    vb_ref[...] = v_ref[...].astype(BF16)
    qt = jnp.dot(hb_ref[...], wq_ref[...], preferred_element_type=F32).T
    pairs = _peer_pairs()

    def route(h, exact):
        miscount = jnp.zeros((1, T), F32)
        s, rank, vals = [], [], []
        for p in range(2):
            idx = 2 * h + p
            sp = jnp.dot(sk_ref[idx], qt[idx * LANES:(idx + 1) * LANES, :].astype(BF16),
                         preferred_element_type=F32)
            state = {"rank": jnp.full(sp.shape, float(PEER_TOPK), F32), "vals": []}

            def on_pick(r, mx, pick, state=state):
                state["rank"] = jnp.where(pick, float(r), state["rank"])
                state["vals"].append(mx)

            _extract_top(sp, PEER_TOPK, on_pick, exact)
            n_ranked = jnp.sum(jnp.where(state["rank"] < float(PEER_TOPK), 1.0, 0.0), axis=0, keepdims=True)
            miscount = jnp.maximum(miscount, jnp.abs(n_ranked - float(PEER_TOPK)))
            s.append(sp)
            rank.append(state["rank"])
            vals.append(state["vals"])
        cand = jnp.concatenate([vals[0][a] + vals[1][b] for a, b in pairs], axis=0)
        picked = {"mask": jnp.zeros(cand.shape, jnp.bool_)}

        def on_pick_c(r, mx, pick, picked=picked):
            picked["mask"] = picked["mask"] | pick

        _extract_top(cand, PEER_TOPK, on_pick_c, exact)
        top = vals[0][0] + vals[1][0]
        selw = jnp.where(picked["mask"], jnp.exp(cand - top), 0.0)
        z = jnp.sum(selw, axis=0, keepdims=True)
        self32 = jnp.where(picked["mask"], 1.0, 0.0)
        miscount = jnp.maximum(miscount, jnp.abs(jnp.sum(self32, axis=0, keepdims=True) - float(PEER_TOPK)))
        l1 = jnp.zeros(s[0].shape, F32)
        pos = 0
        for a in range(PEER_TOPK):
            cnt = sum(1 for ab in pairs if ab[0] == a)
            n_a = jnp.sum(self32[pos:pos + cnt, :], axis=0, keepdims=True)
            pos += cnt
            l1 = jnp.where(rank[0] == float(a), n_a, l1)
        l1_ref[h] = l1
        a_ref[h] = jnp.exp(s[0] - vals[0][0])
        r2_ref[h] = rank[1].astype(r2_ref.dtype)
        b_ref[h] = (jnp.exp(s[1] - vals[1][0]) / z).astype(b_ref.dtype)
        return miscount

    miscount = functools.reduce(jnp.maximum, [route(h, exact=False) for h in range(PEER_HEADS)])

    @pl.when(jnp.max(miscount) > 0.0)
    def _():
        for h in range(PEER_HEADS):
            route(h, exact=True)


def _peer_route(h1b, wq, sk, expert_u, expert_v, T):
    S, D = h1b.shape
    n_steps = S // T
    E = expert_u.shape[0]
    kernel = functools.partial(_route_kernel, T=T)
    shp = jax.ShapeDtypeStruct((PEER_HEADS, PEER_N_KEYS, S), F32)
    shp_b = jax.ShapeDtypeStruct((PEER_HEADS, PEER_N_KEYS, S), BF16)
    spec = pl.BlockSpec((PEER_HEADS, PEER_N_KEYS, T), lambda t: (0, 0, t))
    table = pl.BlockSpec((E // n_steps, D), lambda t: (t, 0))
    return pl.pallas_call(
        kernel,
        grid=(n_steps,),
        in_specs=[pl.BlockSpec((T, D), lambda t: (t, 0)),
                  _resident(wq.shape, lambda t: (0, 0)),
                  _resident(sk.shape, lambda t: (0, 0, 0)),
                  table, table],
        out_specs=[spec, spec, spec, spec, table, table],
        out_shape=[shp, shp, shp_b, shp_b,
                   jax.ShapeDtypeStruct(expert_u.shape, BF16), jax.ShapeDtypeStruct(expert_v.shape, BF16)],
        compiler_params=_params("parallel"),
        name="peer_route",
    )(h1b, wq, sk, expert_u, expert_v)


def _peer_kernel(ht_ref, u_ref, v_ref, l1_ref, a_ref, r2_ref, b_ref, o_ref, act_ref, *, n_i1, ce):
    @pl.when(pl.program_id(1) == 0)
    def _():
        o_ref[...] = jnp.zeros_like(o_ref)

    T = ht_ref.shape[1]
    n_slab = PEER_N_KEYS // PACKED_ROWS
    i1_per_chunk = ce // PEER_N_KEYS
    r2 = [r2_ref[h] for h in range(PEER_HEADS)]
    b = [b_ref[h] for h in range(PEER_HEADS)]

    def gates(c):
        tiles = []
        for j in range(c * i1_per_chunk, (c + 1) * i1_per_chunk):
            gate = [None] * n_slab
            for h in range(PEER_HEADS):
                l1 = jnp.broadcast_to(l1_ref[h, j:j + 1, :], (PACKED_ROWS, T)).astype(BF16)
                a = jnp.broadcast_to(a_ref[h, j:j + 1, :], (PACKED_ROWS, T)).astype(BF16)
                for s in range(n_slab):
                    slab = slice(s * PACKED_ROWS, (s + 1) * PACKED_ROWS)
                    val = jnp.where(r2[h][slab] < l1, b[h][slab], 0.0) * a
                    gate[s] = val if gate[s] is None else gate[s] + val
            tiles.extend(gate)
        return tiles

    n_chunk = n_i1 * PEER_N_KEYS // ce
    zero = jnp.minimum(pl.program_id(1), 0)
    for c in range(n_chunk):
        act_ref[c + zero] = jnp.dot(u_ref[c * ce:(c + 1) * ce, :], ht_ref[...],
                                    preferred_element_type=F32)
    tiles = []
    for c in range(n_chunk):
        gel = _gelu(act_ref[c + zero]).astype(BF16)
        g = gates(c)
        tiles.extend(gel[k * PACKED_ROWS:(k + 1) * PACKED_ROWS] * g[k] for k in range(len(g)))
    pt = jnp.concatenate(tiles, axis=0)
    o_ref[...] += lax.dot_general(pt, v_ref[...], TN_DIMS, preferred_element_type=F32)


def _peer_experts(h1bt, u_b, v_b, l1, a, r2, b, T, eb):
    D, S = h1bt.shape
    E = u_b.shape[0]
    n_i1 = eb // PEER_N_KEYS
    ce = MXU_DIM // 2
    kernel = functools.partial(_peer_kernel, n_i1=n_i1, ce=ce)
    row_spec = pl.BlockSpec((PEER_HEADS, n_i1, T), lambda t, e: (0, e, t))
    full_spec = pl.BlockSpec((PEER_HEADS, PEER_N_KEYS, T), lambda t, e: (0, 0, t))
    return pl.pallas_call(
        kernel,
        grid=(S // T, E // eb),
        in_specs=[pl.BlockSpec((D, T), lambda t, e: (0, t)),
                  pl.BlockSpec((eb, D), lambda t, e: (e, 0)),
                  pl.BlockSpec((eb, D), lambda t, e: (e, 0)),
                  row_spec, row_spec, full_spec, full_spec],
        out_specs=pl.BlockSpec((T, D), lambda t, e: (t, 0)),
        out_shape=jax.ShapeDtypeStruct((S, D), F32),
        scratch_shapes=[pltpu.VMEM((eb // ce, ce, T), F32)],
        compiler_params=_params("parallel", "arbitrary"),
        name="peer_experts",
    )(h1bt, u_b, v_b, l1, a, r2, b)


def _ple_kernel(h1_ref, peer_ref, p_ref, wg_ref, wp_ref, g_ref, b_ref, o_ref):
    r = DEEPNORM_ALPHA * h1_ref[...] + peer_ref[...]
    gate = jax.nn.sigmoid(jnp.dot(r.astype(BF16), wg_ref[...], preferred_element_type=F32))
    proj = jnp.dot(p_ref[...], wp_ref[...], preferred_element_type=F32)
    o_ref[...] = _layer_norm(r + gate * proj, g_ref[...], b_ref[...])


def _ple(h1, peer, p_b, wg, wp, ln_g, ln_b, tm):
    S, D = h1.shape
    P = p_b.shape[1]
    return pl.pallas_call(
        _ple_kernel,
        grid=(S // tm,),
        in_specs=[pl.BlockSpec((tm, D), lambda i: (i, 0)),
                  pl.BlockSpec((tm, D), lambda i: (i, 0)),
                  pl.BlockSpec((tm, P), lambda i: (i, 0)),
                  _resident((D, D), lambda i: (0, 0)),
                  _resident((P, D), lambda i: (0, 0)),
                  pl.BlockSpec((1, D), lambda i: (0, 0)),
                  pl.BlockSpec((1, D), lambda i: (0, 0))],
        out_specs=pl.BlockSpec((tm, D), lambda i: (i, 0)),
        out_shape=jax.ShapeDtypeStruct((S, D), F32),
        compiler_params=_params("parallel"),
        name="ple_ln2",
    )(h1, peer, p_b, wg, wp, ln_g.reshape(1, D), ln_b.reshape(1, D))


def _layer(x, p, ln_emb_g, ln_emb_b, rel_bias, w_in, gmlp_ln_g, gmlp_ln_b, gmlp_w_s, gmlp_b_s,
           w_br_attn, w_br_gmlp, w_mix_out, ln1_g, ln1_b, peer_w_q, peer_sub_keys, peer_u, peer_v,
           ple_w_proj, ple_w_gate, ln2_g, ln2_b):
    S, D = x.shape
    tm = min(512, S)
    o_k, o_v, o_qi = ATTN_WIDTH, 2 * ATTN_WIDTH, 3 * ATTN_WIDTH
    o_ki = o_qi + N_IDX_HEADS * IDX_HEAD_DIM
    o_gu = o_ki + IDX_HEAD_DIM + N_IDX_HEADS

    h, hb = _ln_embed(x, ln_emb_g, ln_emb_b, tm)
    wqt = jnp.concatenate([w_in[:, :o_k], w_in[:, o_qi:o_ki]], axis=1).T.astype(BF16)
    w1 = w_in[:, o_k:o_qi].astype(BF16)
    w2 = jnp.pad(w_in[:, o_ki:o_gu], ((0, 0), (0, LANES - (o_gu - o_ki)))).astype(BF16)
    w3 = w_in[:, o_gu:].astype(BF16)
    qt = _matmul_t(wqt, hb, BF16, tm, "in_proj_q")
    kv = _matmul(hb, w1, BF16, min(1024, S), 512, "in_proj_kv")
    kw = _matmul(hb, w2, F32, min(1024, S), LANES, "in_proj_index")
    zg = _matmul(hb, w3, F32, min(1024, S), 512, "in_proj_gmlp")

    attn = _dsa_attention(qt, kv, kw[:, :IDX_HEAD_DIM].astype(BF16),
                          kw[:, IDX_HEAD_DIM:IDX_HEAD_DIM + N_IDX_HEADS].T, rel_bias, bq=256)
    gm = _sgu(zg, gmlp_ln_g, gmlp_ln_b, gmlp_w_s, gmlp_b_s, tm)
    h1, h1b, h1bt = _merge(attn, gm, zg, h, w_br_attn.astype(BF16), w_br_gmlp.astype(BF16),
                           w_mix_out.astype(BF16), ln1_g, ln1_b, min(256, S))

    sk = peer_sub_keys.reshape(PEER_HEADS * 2, PEER_N_KEYS, -1).astype(BF16)
    l1, a, r2, b, u_b, v_b = _peer_route(h1b, peer_w_q.astype(BF16), sk, peer_u, peer_v, min(256, S))
    peer = _peer_experts(h1bt, u_b, v_b, l1, a, r2, b, T=min(512, S), eb=1024)
    return _ple(h1, peer, p.astype(BF16), ple_w_gate.astype(BF16), ple_w_proj.astype(BF16),
                ln2_g, ln2_b, min(256, S))


def kernel(x, p, positions, ln_emb_g, ln_emb_b, rel_bias, w_in, gmlp_ln_g, gmlp_ln_b, gmlp_w_s, gmlp_b_s, w_br_attn, w_br_gmlp, w_mix_out, ln1_g, ln1_b, peer_w_q, peer_sub_keys, peer_u, peer_v, ple_w_proj, ple_w_gate, ln2_g, ln2_b):
    del positions
    assert x.shape[0] == 1 and w_in.shape[0] == DEPTH
    out = _layer(x[0], p[0, 0], ln_emb_g, ln_emb_b, rel_bias, w_in[0], gmlp_ln_g[0], gmlp_ln_b[0],
                 gmlp_w_s[0], gmlp_b_s[0], w_br_attn[0], w_br_gmlp[0], w_mix_out[0], ln1_g[0],
                 ln1_b[0], peer_w_q[0], peer_sub_keys[0], peer_u[0], peer_v[0], ple_w_proj[0],
                 ple_w_gate[0], ln2_g[0], ln2_b[0])
    return out[None]
```

```python
import functools
import math

import jax
import jax.numpy as jnp
from jax import lax
from jax.experimental import pallas as pl
from jax.experimental.pallas import tpu as pltpu

F32 = jnp.float32
BF16 = jnp.bfloat16
I32 = jnp.int32

LN_EPS = 1e-5
DEPTH = 1
DEEPNORM_ALPHA = (2.0 * DEPTH) ** 0.25

N_ATTN_HEADS = 8
ATTN_HEAD_DIM = 128
ATTN_WIDTH = N_ATTN_HEADS * ATTN_HEAD_DIM
N_IDX_HEADS = 16
IDX_HEAD_DIM = 64
TOPK_MAX = 256
GMLP_CHUNK = 128
GMLP_GROUPS = 8
GMLP_WIDTH = 1024
N_REL_BUCKETS = 32
REL_MAX_DISTANCE = 128
PEER_HEADS = 8
PEER_N_KEYS = 128
PEER_TOPK = 16

LANES = 128
SUBLANES = 8
PACKED_ROWS = 2 * SUBLANES
MXU_DIM = 256
NEG = -1e30
SEARCH_STEPS_MAX = 64
NT_DIMS = (((1,), (1,)), ((), ()))
TN_DIMS = (((0,), (0,)), ((), ()))
VMEM_LIMIT = 56 * 1024 * 1024


def _params(*sem):
    return pltpu.CompilerParams(dimension_semantics=sem, vmem_limit_bytes=VMEM_LIMIT)


def _resident(shape, index_map):
    return pl.BlockSpec(shape, index_map, pipeline_mode=pl.Buffered(1))


def _gelu(x):
    return 0.5 * x * (1.0 + jnp.tanh(0.7978845608028654 * (x + 0.044715 * (x * x * x))))


def _layer_norm(x, g, b):
    mu = jnp.mean(x, axis=-1, keepdims=True)
    xc = x - mu
    var = jnp.mean(xc * xc, axis=-1, keepdims=True)
    return xc * lax.rsqrt(var + LN_EPS) * g + b


def _ln_kernel(x_ref, g_ref, b_ref, h_ref, hb_ref):
    y = _layer_norm(x_ref[...], g_ref[...], b_ref[...])
    h_ref[...] = y
    hb_ref[...] = y.astype(BF16)


def _ln_embed(x, g, b, tm):
    S, D = x.shape
    return pl.pallas_call(
        _ln_kernel,
        grid=(S // tm,),
        in_specs=[pl.BlockSpec((tm, D), lambda i: (i, 0)),
                  pl.BlockSpec((1, D), lambda i: (0, 0)),
                  pl.BlockSpec((1, D), lambda i: (0, 0))],
        out_specs=[pl.BlockSpec((tm, D), lambda i: (i, 0)),
                   pl.BlockSpec((tm, D), lambda i: (i, 0))],
        out_shape=[jax.ShapeDtypeStruct((S, D), F32), jax.ShapeDtypeStruct((S, D), BF16)],
        compiler_params=_params("parallel"),
        name="ln_embed",
    )(x, g.reshape(1, D), b.reshape(1, D))


def _mm_kernel(a_ref, w_ref, o_ref):
    w = w_ref[...].astype(BF16)
    o_ref[...] = jnp.dot(a_ref[...], w, preferred_element_type=F32).astype(o_ref.dtype)


def _matmul(a, w, out_dtype, tm, tn, name, col0=0, n_cols=None):
    M, K = a.shape
    N = w.shape[1] if n_cols is None else n_cols
    j0 = col0 // tn
    return pl.pallas_call(
        _mm_kernel,
        grid=(M // tm, N // tn),
        in_specs=[pl.BlockSpec((tm, K), lambda i, j: (i, 0)),
                  pl.BlockSpec((K, tn), lambda i, j: (0, j + j0))],
        out_specs=pl.BlockSpec((tm, tn), lambda i, j: (i, j)),
        out_shape=jax.ShapeDtypeStruct((M, N), out_dtype),
        compiler_params=_params("parallel", "arbitrary"),
        name=name,
    )(a, w)


def _mm_t_kernel(a_ref, *refs):
    w_refs, o_ref = refs[:-1], refs[-1]
    row = 0
    for w_ref in w_refs:
        y = jnp.dot(a_ref[...], w_ref[...].astype(BF16), preferred_element_type=F32)
        o_ref[row:row + y.shape[1], :] = y.T.astype(o_ref.dtype)
        row += y.shape[1]


def _matmul_t(a, w, col_blocks, n, out_dtype, tm, name):
    M, K = a.shape
    w_specs = [_resident((K, n), lambda i, c=c: (0, c)) for c in col_blocks]
    N = n * len(col_blocks)
    return pl.pallas_call(
        _mm_t_kernel,
        grid=(M // tm,),
        in_specs=[pl.BlockSpec((tm, K), lambda i: (i, 0))] + w_specs,
        out_specs=pl.BlockSpec((N, tm), lambda i: (0, i)),
        out_shape=jax.ShapeDtypeStruct((N, M), out_dtype),
        compiler_params=_params("parallel"),
        name=name,
    )(a, *([w] * len(col_blocks)))


def _reduce_row_groups(t, op):
    parts = [t[j * SUBLANES:(j + 1) * SUBLANES, :] for j in range(t.shape[0] // SUBLANES)]
    while len(parts) > 1:
        parts = [op(parts[j], parts[j + 1]) for j in range(0, len(parts) - 1, 2)] + (
            [parts[-1]] if len(parts) % 2 else [])
    return parts[0]


def _dsa_kernel(qit_ref, wt_ref, qt_ref, ki_ref, k_ref, v_ref, bias_ref, o_ref,
                sc_ref, m_ref, l_ref, acc_ref, *, bq, top_k):
    i = pl.program_id(0)
    w = wt_ref[...] * ((IDX_HEAD_DIM ** -0.5) * (N_IDX_HEADS ** -0.5))
    key_i = lax.broadcasted_iota(I32, (bq, bq), 0)
    qry_i = lax.broadcasted_iota(I32, (bq, bq), 1)
    causal = key_i <= qry_i

    def rows(c):
        return pl.ds(pl.multiple_of(c * bq, bq), bq)

    def scores(c):
        kic = ki_ref[rows(c), :]
        acc = jnp.zeros((bq, bq), F32)
        for h in range(N_IDX_HEADS):
            d = jnp.dot(kic, qit_ref[h * IDX_HEAD_DIM:(h + 1) * IDX_HEAD_DIM, :],
                        preferred_element_type=F32)
            acc = acc + w[h:h + 1, :] * jnp.maximum(d, 0.0)
        return acc

    def score_chunk(c, carry):
        sc_ref[c] = scores(c)
        return carry

    lax.fori_loop(0, i, score_chunk, 0)
    sc_ref[i] = jnp.where(causal, scores(i), -jnp.inf)

    def lowest_highest(c, lh):
        x = sc_ref[c]
        return (jnp.minimum(lh[0], _reduce_row_groups(x, jnp.minimum)),
                jnp.maximum(lh[1], _reduce_row_groups(x, jnp.maximum)))

    lo8, hi8 = lax.fori_loop(0, i, lowest_highest, (jnp.full((SUBLANES, bq), jnp.inf, F32),
                                                    jnp.full((SUBLANES, bq), -jnp.inf, F32)))
    diag = sc_ref[i]
    lo8 = jnp.minimum(lo8, _reduce_row_groups(jnp.where(causal, diag, jnp.inf), jnp.minimum))
    hi8 = jnp.maximum(hi8, _reduce_row_groups(diag, jnp.maximum))
    lo0 = jnp.min(lo8, axis=0, keepdims=True)
    hi0 = jnp.max(hi8, axis=0, keepdims=True)

    def count_ge(cand):
        def body(c, cnt):
            return cnt + _reduce_row_groups(jnp.where(sc_ref[c] >= cand, 1.0, 0.0), jnp.add)
        cnt = lax.fori_loop(0, i + 1, body, jnp.zeros((SUBLANES, bq), F32))
        return jnp.sum(cnt, axis=0, keepdims=True)

    n_keys = (i * bq + 1 + lax.broadcasted_iota(I32, (1, bq), 1)).astype(F32)
    few_keys = n_keys < float(top_k)

    def bisect(state):
        it, lo, hi, settled = state
        mid = lo + (hi - lo) * 0.5
        cnt = count_ge(mid)
        enough = cnt >= float(top_k)
        open_ = settled == 0.0
        lo_new = jnp.where(open_ & enough, mid, lo)
        hi_new = jnp.where(open_ & jnp.logical_not(enough), mid, hi)
        done = (cnt == float(top_k)) | (mid <= lo) | (mid >= hi)
        return it + 1, lo_new, hi_new, jnp.where(done, 1.0, settled)

    def unsettled(state):
        it, _, _, settled = state
        return (it < SEARCH_STEPS_MAX) & (jnp.min(settled) == 0.0)

    _, lo, _, _ = lax.while_loop(
        unsettled, bisect, (jnp.int32(0), lo0, hi0, jnp.where(few_keys, 1.0, 0.0)))
    thresh = jnp.where(few_keys, -jnp.inf, lo)

    def mask_chunk(c, carry):
        sc_ref[c] = jnp.where(sc_ref[c] >= thresh, 0.0, NEG)
        return carry

    lax.fori_loop(0, i, mask_chunk, 0)
    sc_ref[i] = jnp.where(causal, jnp.where(sc_ref[i] >= thresh, 0.0, NEG), NEG)

    scale = ATTN_HEAD_DIM ** -0.5
    m_ref[...] = jnp.full(m_ref.shape, NEG, F32)
    l_ref[...] = jnp.zeros(l_ref.shape, F32)
    acc_ref[...] = jnp.zeros(acc_ref.shape, F32)

    def attend(c, near):
        mbias = sc_ref[c]
        m_all, l_all = m_ref[...], l_ref[...]
        m_rows, l_rows = [], []

        def qk(h):
            cols = slice(h * ATTN_HEAD_DIM, (h + 1) * ATTN_HEAD_DIM)
            return jnp.dot(k_ref[rows(c), cols], qt_ref[cols, :], preferred_element_type=F32)

        ahead = N_ATTN_HEADS
        s_raw = [qk(h) for h in range(ahead)]
        for h in range(N_ATTN_HEADS):
            cols = slice(h * ATTN_HEAD_DIM, (h + 1) * ATTN_HEAD_DIM)
            s = s_raw[h] * scale + mbias
            if h + ahead < N_ATTN_HEADS:
                s_raw.append(qk(h + ahead))
            if near is not None:
                s = s + bias_ref[h, near * bq:(near + 1) * bq, :]
            m_old = m_all[h:h + 1, :]
            m_new = jnp.maximum(m_old, jnp.max(_reduce_row_groups(s, jnp.maximum), axis=0, keepdims=True))
            p = jnp.exp(s - m_new)
            a = jnp.exp(m_old - m_new)
            m_rows.append(m_new)
            l_rows.append(a * l_all[h:h + 1, :]
                          + jnp.sum(_reduce_row_groups(p, jnp.add), axis=0, keepdims=True))
            pv = lax.dot_general(v_ref[rows(c), cols], p.astype(BF16), TN_DIMS, preferred_element_type=F32)
            acc_ref[h] = a * acc_ref[h] + pv
        m_ref[...] = jnp.concatenate(m_rows, axis=0)
        l_ref[...] = jnp.concatenate(l_rows, axis=0)

    def far_chunk(c, carry):
        attend(c, None)
        return carry

    lax.fori_loop(0, jnp.maximum(i - 1, 0), far_chunk, 0)

    @pl.when(i > 0)
    def _():
        attend(i - 1, 0)

    attend(i, 1)
    for h in range(N_ATTN_HEADS):
        out_t = acc_ref[h] / l_ref[h:h + 1, :]
        o_ref[:, h * ATTN_HEAD_DIM:(h + 1) * ATTN_HEAD_DIM] = out_t.T.astype(o_ref.dtype)


def _rel_bucket(dist):
    n = jnp.maximum(dist, 0)
    max_exact = N_REL_BUCKETS // 2
    nf = jnp.maximum(n, 1).astype(F32)
    large = max_exact + (jnp.log(nf / max_exact) / math.log(REL_MAX_DISTANCE / max_exact)
                         * (N_REL_BUCKETS - max_exact)).astype(I32)
    large = jnp.minimum(large, N_REL_BUCKETS - 1)
    return jnp.where(n < max_exact, n, large)


def _near_bias_tiles(rel_bias, bq):
    assert bq >= REL_MAX_DISTANCE
    key = jnp.arange(2 * bq, dtype=I32)[:, None]
    qry = jnp.arange(bq, dtype=I32)[None, :]
    bucket = _rel_bucket(qry + bq - key)
    rel = (rel_bias - rel_bias[N_REL_BUCKETS - 1]).T
    tiles = jnp.broadcast_to(rel[:, 0][:, None, None], (rel.shape[0],) + bucket.shape)
    for b in range(1, N_REL_BUCKETS):
        tiles = jnp.where(bucket[None] >= b, rel[:, b][:, None, None], tiles)
    return tiles


def _dsa_attention(qt, kv, ki_b, wt, rel_bias, bq):
    S = kv.shape[0]
    top_k = min(TOPK_MAX, S // 4)
    n_blk = S // bq
    bias = _near_bias_tiles(rel_bias, bq)
    kernel = functools.partial(_dsa_kernel, bq=bq, top_k=top_k)
    return pl.pallas_call(
        kernel,
        grid=(n_blk,),
        in_specs=[pl.BlockSpec((N_IDX_HEADS * IDX_HEAD_DIM, bq), lambda i: (1, i)),
                  pl.BlockSpec((N_IDX_HEADS, bq), lambda i: (0, i)),
                  pl.BlockSpec((ATTN_WIDTH, bq), lambda i: (0, i)),
                  _resident((S, IDX_HEAD_DIM), lambda i: (0, 0)),
                  _resident((S, ATTN_WIDTH), lambda i: (0, 0)),
                  _resident((S, ATTN_WIDTH), lambda i: (0, 1)),
                  _resident((N_ATTN_HEADS, 2 * bq, bq), lambda i: (0, 0, 0))],
        out_specs=pl.BlockSpec((bq, ATTN_WIDTH), lambda i: (i, 0)),
        out_shape=jax.ShapeDtypeStruct((S, ATTN_WIDTH), BF16),
        scratch_shapes=[pltpu.VMEM((n_blk, bq, bq), F32),
                        pltpu.VMEM((N_ATTN_HEADS, bq), F32),
                        pltpu.VMEM((N_ATTN_HEADS, bq), F32),
                        pltpu.VMEM((N_ATTN_HEADS, ATTN_HEAD_DIM, bq), F32)],
        compiler_params=_params("arbitrary"),
        name="dsa_attention",
    )(qt, wt, qt, ki_b, kv, kv, bias)


def _sgu_kernel(gu_ref, gv_ref, g_ref, b_ref, ws_ref, bs_ref, o_ref, *, n_chunk):
    u = _gelu(gu_ref[...])
    v = _layer_norm(_gelu(gv_ref[...]), g_ref[...], b_ref[...]).astype(BF16)
    r = lax.broadcasted_iota(I32, (GMLP_CHUNK, GMLP_CHUNK), 0)
    c = lax.broadcasted_iota(I32, (GMLP_CHUNK, GMLP_CHUNK), 1)
    for g in range(GMLP_GROUPS):
        wg = jnp.where(c <= r, ws_ref[g], 0.0).astype(BF16)
        bcol = bs_ref[:, g:g + 1]
        cols = slice(g * LANES, (g + 1) * LANES)
        for n in range(n_chunk):
            rws = slice(n * GMLP_CHUNK, (n + 1) * GMLP_CHUNK)
            mixed = jnp.dot(wg, v[rws, cols], preferred_element_type=F32) + bcol
            o_ref[rws, cols] = (u[rws, cols] * mixed).astype(o_ref.dtype)


def _sgu(zg, ln_g, ln_b, w_s, b_s, tm):
    S = zg.shape[0]
    kernel = functools.partial(_sgu_kernel, n_chunk=tm // GMLP_CHUNK)
    return pl.pallas_call(
        kernel,
        grid=(S // tm,),
        in_specs=[pl.BlockSpec((tm, GMLP_WIDTH), lambda i: (i, 0)),
                  pl.BlockSpec((tm, GMLP_WIDTH), lambda i: (i, 1)),
                  pl.BlockSpec((1, GMLP_WIDTH), lambda i: (0, 0)),
                  pl.BlockSpec((1, GMLP_WIDTH), lambda i: (0, 0)),
                  pl.BlockSpec((GMLP_GROUPS, GMLP_CHUNK, GMLP_CHUNK), lambda i: (0, 0, 0)),
                  pl.BlockSpec((GMLP_CHUNK, GMLP_GROUPS), lambda i: (0, 0))],
        out_specs=pl.BlockSpec((tm, GMLP_WIDTH), lambda i: (i, 0)),
        out_shape=jax.ShapeDtypeStruct((S, GMLP_WIDTH), BF16),
        compiler_params=_params("parallel"),
        name="gmlp_sgu",
    )(zg, zg, ln_g.reshape(1, -1), ln_b.reshape(1, -1), w_s, b_s.T)


def _merge_kernel(at_ref, gm_ref, ga_ref, gg_ref, h_ref, wa_ref, wg_ref, wo_ref, g_ref, b_ref,
                  h1_ref, h1b_ref, h1bt_ref):
    a1 = jnp.dot(at_ref[...], wa_ref[...], preferred_element_type=F32)
    a2 = jnp.dot(gm_ref[...], wg_ref[...], preferred_element_type=F32)
    merged = jax.nn.sigmoid(ga_ref[...]) * a1 + jax.nn.sigmoid(gg_ref[...]) * a2
    y = DEEPNORM_ALPHA * h_ref[...] + jnp.dot(merged.astype(BF16), wo_ref[...], preferred_element_type=F32)
    h1 = _layer_norm(y, g_ref[...], b_ref[...])
    h1_ref[...] = h1
    h1b_ref[...] = h1.astype(BF16)
    h1bt_ref[...] = h1.T.astype(BF16)


def _merge(attn, gm, zg, h, wa, wg, wo, ln_g, ln_b, tm):
    S, D = h.shape
    W = attn.shape[1]
    return pl.pallas_call(
        _merge_kernel,
        grid=(S // tm,),
        in_specs=[pl.BlockSpec((tm, W), lambda i: (i, 0)),
                  pl.BlockSpec((tm, W), lambda i: (i, 0)),
                  pl.BlockSpec((tm, D), lambda i: (i, 1)),
                  pl.BlockSpec((tm, D), lambda i: (i, 2)),
                  pl.BlockSpec((tm, D), lambda i: (i, 0)),
                  _resident((W, D), lambda i: (0, 0)),
                  _resident((W, D), lambda i: (0, 0)),
                  _resident((D, D), lambda i: (0, 0)),
                  pl.BlockSpec((1, D), lambda i: (0, 0)),
                  pl.BlockSpec((1, D), lambda i: (0, 0))],
        out_specs=[pl.BlockSpec((tm, D), lambda i: (i, 0)),
                   pl.BlockSpec((tm, D), lambda i: (i, 0)),
                   pl.BlockSpec((D, tm), lambda i: (0, i))],
        out_shape=[jax.ShapeDtypeStruct((S, D), F32), jax.ShapeDtypeStruct((S, D), BF16),
                   jax.ShapeDtypeStruct((D, S), BF16)],
        compiler_params=_params("parallel"),
        name="merge_ln1",
    )(attn, gm, zg, zg, h, wa, wg, wo, ln_g.reshape(1, D), ln_b.reshape(1, D))


def _peer_pairs():
    pairs = [(a, b) for a in range(PEER_TOPK) for b in range(PEER_TOPK) if (a + 1) * (b + 1) <= PEER_TOPK]
    return sorted(pairs, key=lambda ab: ab[0] * PEER_TOPK + ab[1])


def _extract_top(work, n_rounds, on_pick, exact):
    n = float(work.shape[0])
    iota = lax.broadcasted_iota(I32, work.shape, 0).astype(F32)
    for r in range(n_rounds):
        mx = jnp.max(work, axis=0, keepdims=True)
        pick = work == mx
        if exact:
            first = jnp.min(jnp.where(pick, iota, n), axis=0, keepdims=True)
            pick = iota == first
        on_pick(r, mx, pick)
        work = jnp.where(pick, -jnp.inf, work)


def _route_kernel(hb_ref, wq_ref, sk_ref, u_ref, v_ref, l1_ref, a_ref, r2_ref, b_ref, ub_ref, vb_ref, *, T):
    ub_ref[...] = u_ref[...].astype(BF16)
    vb_ref[...] = v_ref[...].astype(BF16)
    qt = jnp.dot(hb_ref[...], wq_ref[...], preferred_element_type=F32).T
    pairs = _peer_pairs()

    def route(h, exact):
        miscount = jnp.zeros((1, T), F32)
        s, rank, vals = [], [], []
        for p in range(2):
            idx = 2 * h + p
            sp = jnp.dot(sk_ref[idx], qt[idx * LANES:(idx + 1) * LANES, :].astype(BF16),
                         preferred_element_type=F32)
            state = {"rank": jnp.full(sp.shape, float(PEER_TOPK), F32), "vals": []}

            def on_pick(r, mx, pick, state=state):
                state["rank"] = jnp.where(pick, float(r), state["rank"])
                state["vals"].append(mx)

            _extract_top(sp, PEER_TOPK, on_pick, exact)
            n_ranked = jnp.sum(jnp.where(state["rank"] < float(PEER_TOPK), 1.0, 0.0), axis=0, keepdims=True)
            miscount = jnp.maximum(miscount, jnp.abs(n_ranked - float(PEER_TOPK)))
            s.append(sp)
            rank.append(state["rank"])
            vals.append(state["vals"])
        cand = jnp.concatenate([vals[0][a] + vals[1][b] for a, b in pairs], axis=0)
        picked = {"mask": jnp.zeros(cand.shape, jnp.bool_)}

        def on_pick_c(r, mx, pick, picked=picked):
            picked["mask"] = picked["mask"] | pick

        _extract_top(cand, PEER_TOPK, on_pick_c, exact)
        top = vals[0][0] + vals[1][0]
        selw = jnp.where(picked["mask"], jnp.exp(cand - top), 0.0)
        z = jnp.sum(selw, axis=0, keepdims=True)
        self32 = jnp.where(picked["mask"], 1.0, 0.0)
        miscount = jnp.maximum(miscount, jnp.abs(jnp.sum(self32, axis=0, keepdims=True) - float(PEER_TOPK)))
        l1 = jnp.zeros(s[0].shape, F32)
        pos = 0
        for a in range(PEER_TOPK):
            cnt = sum(1 for ab in pairs if ab[0] == a)
            n_a = jnp.sum(self32[pos:pos + cnt, :], axis=0, keepdims=True)
            pos += cnt
            l1 = jnp.where(rank[0] == float(a), n_a, l1)
        l1_ref[h] = l1
        a_ref[h] = jnp.exp(s[0] - vals[0][0])
        r2_ref[h] = rank[1].astype(r2_ref.dtype)
        b_ref[h] = (jnp.exp(s[1] - vals[1][0]) / z).astype(b_ref.dtype)
        return miscount

    miscount = functools.reduce(jnp.maximum, [route(h, exact=False) for h in range(PEER_HEADS)])

    @pl.when(jnp.max(miscount) > 0.0)
    def _():
        for h in range(PEER_HEADS):
            route(h, exact=True)


def _peer_route(h1b, wq, sk, expert_u, expert_v, T):
    S, D = h1b.shape
    n_steps = S // T
    E = expert_u.shape[0]
    kernel = functools.partial(_route_kernel, T=T)
    shp = jax.ShapeDtypeStruct((PEER_HEADS, PEER_N_KEYS, S), F32)
    shp_b = jax.ShapeDtypeStruct((PEER_HEADS, PEER_N_KEYS, S), BF16)
    spec = pl.BlockSpec((PEER_HEADS, PEER_N_KEYS, T), lambda t: (0, 0, t))
    table = pl.BlockSpec((E // n_steps, D), lambda t: (t, 0))
    return pl.pallas_call(
        kernel,
        grid=(n_steps,),
        in_specs=[pl.BlockSpec((T, D), lambda t: (t, 0)),
                  _resident(wq.shape, lambda t: (0, 0)),
                  _resident(sk.shape, lambda t: (0, 0, 0)),
                  table, table],
        out_specs=[spec, spec, spec, spec, table, table],
        out_shape=[shp, shp, shp_b, shp_b,
                   jax.ShapeDtypeStruct(expert_u.shape, BF16), jax.ShapeDtypeStruct(expert_v.shape, BF16)],
        compiler_params=_params("parallel"),
        name="peer_route",
    )(h1b, wq, sk, expert_u, expert_v)


def _peer_kernel(ht_ref, u_ref, v_ref, l1_ref, a_ref, r2_ref, b_ref, o_ref, act_ref, *, n_i1, ce):
    @pl.when(pl.program_id(1) == 0)
    def _():
        o_ref[...] = jnp.zeros_like(o_ref)

    T = ht_ref.shape[1]
    n_slab = PEER_N_KEYS // PACKED_ROWS
    i1_per_chunk = ce // PEER_N_KEYS
    r2 = [r2_ref[h] for h in range(PEER_HEADS)]
    b = [b_ref[h] for h in range(PEER_HEADS)]

    def gates(c):
        tiles = []
        for j in range(c * i1_per_chunk, (c + 1) * i1_per_chunk):
            gate = [None] * n_slab
            for h in range(PEER_HEADS):
                l1 = jnp.broadcast_to(l1_ref[h, j:j + 1, :], (PACKED_ROWS, T)).astype(BF16)
                a = jnp.broadcast_to(a_ref[h, j:j + 1, :], (PACKED_ROWS, T)).astype(BF16)
                for s in range(n_slab):
                    slab = slice(s * PACKED_ROWS, (s + 1) * PACKED_ROWS)
                    val = jnp.where(r2[h][slab] < l1, b[h][slab], 0.0) * a
                    gate[s] = val if gate[s] is None else gate[s] + val
            tiles.extend(gate)
        return tiles

    n_chunk = n_i1 * PEER_N_KEYS // ce
    zero = jnp.minimum(pl.program_id(1), 0)
    for c in range(n_chunk):
        act_ref[c + zero] = jnp.dot(u_ref[c * ce:(c + 1) * ce, :], ht_ref[...],
                                    preferred_element_type=F32)
    tiles = []
    for c in range(n_chunk):
        gel = _gelu(act_ref[c + zero]).astype(BF16)
        g = gates(c)
        tiles.extend(gel[k * PACKED_ROWS:(k + 1) * PACKED_ROWS] * g[k] for k in range(len(g)))
    pt = jnp.concatenate(tiles, axis=0)
    o_ref[...] += lax.dot_general(pt, v_ref[...], TN_DIMS, preferred_element_type=F32)


def _peer_experts(h1bt, u_b, v_b, l1, a, r2, b, T, eb):
    D, S = h1bt.shape
    E = u_b.shape[0]
    n_i1 = eb // PEER_N_KEYS
    ce = MXU_DIM // 2
    kernel = functools.partial(_peer_kernel, n_i1=n_i1, ce=ce)
    row_spec = pl.BlockSpec((PEER_HEADS, n_i1, T), lambda t, e: (0, e, t))
    full_spec = pl.BlockSpec((PEER_HEADS, PEER_N_KEYS, T), lambda t, e: (0, 0, t))
    return pl.pallas_call(
        kernel,
        grid=(S // T, E // eb),
        in_specs=[pl.BlockSpec((D, T), lambda t, e: (0, t)),
                  pl.BlockSpec((eb, D), lambda t, e: (e, 0)),
                  pl.BlockSpec((eb, D), lambda t, e: (e, 0)),
                  row_spec, row_spec, full_spec, full_spec],
        out_specs=pl.BlockSpec((T, D), lambda t, e: (t, 0)),
        out_shape=jax.ShapeDtypeStruct((S, D), F32),
        scratch_shapes=[pltpu.VMEM((eb // ce, ce, T), F32)],
        compiler_params=_params("parallel", "arbitrary"),
        name="peer_experts",
    )(h1bt, u_b, v_b, l1, a, r2, b)


def _ple_kernel(h1_ref, peer_ref, p_ref, wg_ref, wp_ref, g_ref, b_ref, o_ref):
    r = DEEPNORM_ALPHA * h1_ref[...] + peer_ref[...]
    gate = jax.nn.sigmoid(jnp.dot(r.astype(BF16), wg_ref[...], preferred_element_type=F32))
    proj = jnp.dot(p_ref[...], wp_ref[...], preferred_element_type=F32)
    o_ref[...] = _layer_norm(r + gate * proj, g_ref[...], b_ref[...])


def _ple(h1, peer, p_b, wg, wp, ln_g, ln_b, tm):
    S, D = h1.shape
    P = p_b.shape[1]
    return pl.pallas_call(
        _ple_kernel,
        grid=(S // tm,),
        in_specs=[pl.BlockSpec((tm, D), lambda i: (i, 0)),
                  pl.BlockSpec((tm, D), lambda i: (i, 0)),
                  pl.BlockSpec((tm, P), lambda i: (i, 0)),
                  _resident((D, D), lambda i: (0, 0)),
                  _resident((P, D), lambda i: (0, 0)),
                  pl.BlockSpec((1, D), lambda i: (0, 0)),
                  pl.BlockSpec((1, D), lambda i: (0, 0))],
        out_specs=pl.BlockSpec((tm, D), lambda i: (i, 0)),
        out_shape=jax.ShapeDtypeStruct((S, D), F32),
        compiler_params=_params("parallel"),
        name="ple_ln2",
    )(h1, peer, p_b, wg, wp, ln_g.reshape(1, D), ln_b.reshape(1, D))


def _layer(x, p, ln_emb_g, ln_emb_b, rel_bias, w_in, gmlp_ln_g, gmlp_ln_b, gmlp_w_s, gmlp_b_s,
           w_br_attn, w_br_gmlp, w_mix_out, ln1_g, ln1_b, peer_w_q, peer_sub_keys, peer_u, peer_v,
           ple_w_proj, ple_w_gate, ln2_g, ln2_b):
    S, D = x.shape
    tm = min(512, S)
    o_k, o_v, o_qi = ATTN_WIDTH, 2 * ATTN_WIDTH, 3 * ATTN_WIDTH
    o_ki = o_qi + N_IDX_HEADS * IDX_HEAD_DIM
    o_gu = o_ki + IDX_HEAD_DIM + N_IDX_HEADS

    h, hb = _ln_embed(x, ln_emb_g, ln_emb_b, tm)
    w2 = jnp.pad(w_in[:, o_ki:o_gu], ((0, 0), (0, LANES - (o_gu - o_ki))))
    w3 = w_in[:, o_gu:]
    qt = _matmul_t(hb, w_in, (0, o_qi // ATTN_WIDTH), ATTN_WIDTH, BF16, tm,
                   "in_proj_q")
    kv = _matmul(hb, w_in, BF16, min(1024, S), 512, "in_proj_kv", col0=o_k, n_cols=o_qi - o_k)
    kw = _matmul(hb, w2, F32, min(1024, S), LANES, "in_proj_index")
    zg = _matmul(hb, w3, F32, min(1024, S), 512, "in_proj_gmlp")

    attn = _dsa_attention(qt, kv, kw[:, :IDX_HEAD_DIM].astype(BF16),
                          kw[:, IDX_HEAD_DIM:IDX_HEAD_DIM + N_IDX_HEADS].T, rel_bias, bq=256)
    gm = _sgu(zg, gmlp_ln_g, gmlp_ln_b, gmlp_w_s, gmlp_b_s, tm)
    h1, h1b, h1bt = _merge(attn, gm, zg, h, w_br_attn.astype(BF16), w_br_gmlp.astype(BF16),
                           w_mix_out.astype(BF16), ln1_g, ln1_b, min(256, S))

    sk = peer_sub_keys.reshape(PEER_HEADS * 2, PEER_N_KEYS, -1).astype(BF16)
    l1, a, r2, b, u_b, v_b = _peer_route(h1b, peer_w_q.astype(BF16), sk, peer_u, peer_v, min(256, S))
    peer = _peer_experts(h1bt, u_b, v_b, l1, a, r2, b, T=min(512, S), eb=1024)
    return _ple(h1, peer, p.astype(BF16), ple_w_gate.astype(BF16), ple_w_proj.astype(BF16),
                ln2_g, ln2_b, min(256, S))


def kernel(x, p, positions, ln_emb_g, ln_emb_b, rel_bias, w_in, gmlp_ln_g, gmlp_ln_b, gmlp_w_s, gmlp_b_s, w_br_attn, w_br_gmlp, w_mix_out, ln1_g, ln1_b, peer_w_q, peer_sub_keys, peer_u, peer_v, ple_w_proj, ple_w_gate, ln2_g, ln2_b):
    del positions
    assert x.shape[0] == 1 and w_in.shape[0] == DEPTH
    out = _layer(x[0], p[0, 0], ln_emb_g, ln_emb_b, rel_bias, w_in[0], gmlp_ln_g[0], gmlp_ln_b[0],
                 gmlp_w_s[0], gmlp_b_s[0], w_br_attn[0], w_br_gmlp[0], w_mix_out[0], ln1_g[0],
                 ln1_b[0], peer_w_q[0], peer_sub_keys[0], peer_u[0], peer_v[0], ple_w_proj[0],
                 ple_w_gate[0], ln2_g[0], ln2_b[0])
    return out[None]
```

```python
import functools
import math

import jax
import jax.numpy as jnp
from jax import lax
from jax.experimental import pallas as pl
from jax.experimental.pallas import tpu as pltpu

F32 = jnp.float32
BF16 = jnp.bfloat16
I32 = jnp.int32

LN_EPS = 1e-5
DEPTH = 1
DEEPNORM_ALPHA = (2.0 * DEPTH) ** 0.25

N_ATTN_HEADS = 8
ATTN_HEAD_DIM = 128
ATTN_WIDTH = N_ATTN_HEADS * ATTN_HEAD_DIM
N_IDX_HEADS = 16
IDX_HEAD_DIM = 64
TOPK_MAX = 256
GMLP_CHUNK = 128
GMLP_GROUPS = 8
GMLP_WIDTH = 1024
N_REL_BUCKETS = 32
REL_MAX_DISTANCE = 128
PEER_HEADS = 8
PEER_N_KEYS = 128
PEER_TOPK = 16

LANES = 128
SUBLANES = 8
PACKED_ROWS = 2 * SUBLANES
MXU_DIM = 256
NEG = -1e30
SEARCH_STEPS_MAX = 64
NT_DIMS = (((1,), (1,)), ((), ()))
TN_DIMS = (((0,), (0,)), ((), ()))
VMEM_LIMIT = 56 * 1024 * 1024


def _params(*sem):
    return pltpu.CompilerParams(dimension_semantics=sem, vmem_limit_bytes=VMEM_LIMIT)


def _resident(shape, index_map):
    return pl.BlockSpec(shape, index_map, pipeline_mode=pl.Buffered(1))


def _gelu(x):
    return 0.5 * x * (1.0 + jnp.tanh(0.7978845608028654 * (x + 0.044715 * (x * x * x))))


def _layer_norm(x, g, b):
    mu = jnp.mean(x, axis=-1, keepdims=True)
    xc = x - mu
    var = jnp.mean(xc * xc, axis=-1, keepdims=True)
    return xc * lax.rsqrt(var + LN_EPS) * g + b


def _ln_kernel(x_ref, g_ref, b_ref, h_ref, hb_ref):
    y = _layer_norm(x_ref[...], g_ref[...], b_ref[...])
    h_ref[...] = y
    hb_ref[...] = y.astype(BF16)


def _ln_embed(x, g, b, tm):
    S, D = x.shape
    return pl.pallas_call(
        _ln_kernel,
        grid=(S // tm,),
        in_specs=[pl.BlockSpec((tm, D), lambda i: (i, 0)),
                  pl.BlockSpec((1, D), lambda i: (0, 0)),
                  pl.BlockSpec((1, D), lambda i: (0, 0))],
        out_specs=[pl.BlockSpec((tm, D), lambda i: (i, 0)),
                   pl.BlockSpec((tm, D), lambda i: (i, 0))],
        out_shape=[jax.ShapeDtypeStruct((S, D), F32), jax.ShapeDtypeStruct((S, D), BF16)],
        compiler_params=_params("parallel"),
        name="ln_embed",
    )(x, g.reshape(1, D), b.reshape(1, D))


def _mm_kernel(a_ref, w_ref, o_ref):
    o_ref[...] = jnp.dot(a_ref[...], w_ref[...], preferred_element_type=F32).astype(o_ref.dtype)


def _matmul(a, w, out_dtype, tm, tn, name):
    M, K = a.shape
    N = w.shape[1]
    return pl.pallas_call(
        _mm_kernel,
        grid=(M // tm, N // tn),
        in_specs=[pl.BlockSpec((tm, K), lambda i, j: (i, 0)),
                  pl.BlockSpec((K, tn), lambda i, j: (0, j))],
        out_specs=pl.BlockSpec((tm, tn), lambda i, j: (i, j)),
        out_shape=jax.ShapeDtypeStruct((M, N), out_dtype),
        compiler_params=_params("parallel", "arbitrary"),
        name=name,
    )(a, w)


def _mm_t_kernel(wt_ref, a_ref, o_ref):
    o_ref[...] = lax.dot_general(wt_ref[...], a_ref[...], NT_DIMS,
                                 preferred_element_type=F32).astype(o_ref.dtype)


def _matmul_t(wt, a, out_dtype, tm, name):
    M, K = a.shape
    N = wt.shape[0]
    return pl.pallas_call(
        _mm_t_kernel,
        grid=(M // tm,),
        in_specs=[_resident((N, K), lambda i: (0, 0)),
                  pl.BlockSpec((tm, K), lambda i: (i, 0))],
        out_specs=pl.BlockSpec((N, tm), lambda i: (0, i)),
        out_shape=jax.ShapeDtypeStruct((N, M), out_dtype),
        compiler_params=_params("parallel"),
        name=name,
    )(wt, a)


def _normal_quantile(p):
    x = 2.0 * p - 1.0
    ln = jnp.log(1.0 - x * x)
    t = 2.0 / (math.pi * 0.147) + 0.5 * ln
    return math.sqrt(2.0) * jnp.sign(x) * jnp.sqrt(jnp.sqrt(t * t - ln / 0.147) - t)


def _reduce_row_groups(t, op):
    parts = [t[j * SUBLANES:(j + 1) * SUBLANES, :] for j in range(t.shape[0] // SUBLANES)]
    while len(parts) > 1:
        parts = [op(parts[j], parts[j + 1]) for j in range(0, len(parts) - 1, 2)] + (
            [parts[-1]] if len(parts) % 2 else [])
    return parts[0]


def _dsa_kernel(qit_ref, wt_ref, qt_ref, ki_ref, k_ref, v_ref, bias_ref, o_ref,
                sc_ref, m_ref, l_ref, acc_ref, *, bq, top_k):
    i = pl.program_id(0)
    w = wt_ref[...] * ((IDX_HEAD_DIM ** -0.5) * (N_IDX_HEADS ** -0.5))
    key_i = lax.broadcasted_iota(I32, (bq, bq), 0)
    qry_i = lax.broadcasted_iota(I32, (bq, bq), 1)
    causal = key_i <= qry_i

    def rows(c):
        return pl.ds(pl.multiple_of(c * bq, bq), bq)

    def scores(c):
        kic = ki_ref[rows(c), :]
        acc = jnp.zeros((bq, bq), F32)
        for h in range(N_IDX_HEADS):
            d = jnp.dot(kic, qit_ref[h * IDX_HEAD_DIM:(h + 1) * IDX_HEAD_DIM, :],
                        preferred_element_type=F32)
            acc = acc + w[h:h + 1, :] * jnp.maximum(d, 0.0)
        return acc

    def score_chunk(c, carry):
        sc_ref[c] = scores(c)
        return carry

    lax.fori_loop(0, i, score_chunk, 0)
    sc_ref[i] = jnp.where(causal, scores(i), -jnp.inf)

    def score_stats(x, valid, st):
        xs = x if valid is None else jnp.where(valid, x, 0.0)
        lo_x = x if valid is None else jnp.where(valid, x, jnp.inf)
        return (jnp.minimum(st[0], _reduce_row_groups(lo_x, jnp.minimum)),
                jnp.maximum(st[1], _reduce_row_groups(x, jnp.maximum)),
                st[2] + _reduce_row_groups(xs, jnp.add),
                st[3] + _reduce_row_groups(xs * xs, jnp.add))

    st = lax.fori_loop(0, i, lambda c, st: score_stats(sc_ref[c], None, st),
                       (jnp.full((SUBLANES, bq), jnp.inf, F32), jnp.full((SUBLANES, bq), -jnp.inf, F32),
                        jnp.zeros((SUBLANES, bq), F32), jnp.zeros((SUBLANES, bq), F32)))
    st = score_stats(sc_ref[i], causal, st)
    lowest = jnp.min(st[0], axis=0, keepdims=True)
    highest = jnp.max(st[1], axis=0, keepdims=True)
    n_keys = (i * bq + 1 + lax.broadcasted_iota(I32, (1, bq), 1)).astype(F32)
    mean = jnp.sum(st[2], axis=0, keepdims=True) / n_keys
    spread = jnp.sqrt(jnp.maximum(jnp.sum(st[3], axis=0, keepdims=True) / n_keys - mean * mean, 0.0))

    def count_ge(cand):
        def body(c, cnt):
            return cnt + _reduce_row_groups(jnp.where(sc_ref[c] >= cand, 1.0, 0.0), jnp.add)
        cnt = lax.fori_loop(0, i + 1, body, jnp.zeros((SUBLANES, bq), F32))
        return jnp.sum(cnt, axis=0, keepdims=True)

    few_keys = n_keys < float(top_k)

    tail = jnp.clip(float(top_k) / n_keys, 1e-6, 1.0 - 1e-6)
    guess = mean + spread * _normal_quantile(1.0 - tail)
    g_lo, g_hi = guess - 0.5 * spread, guess + 0.5 * spread
    cnt_lo, cnt_hi = count_ge(g_lo), count_ge(g_hi)
    below = cnt_lo < float(top_k)
    above = cnt_hi >= float(top_k)
    lo0 = jnp.where(below, lowest, jnp.where(above, g_hi, g_lo))
    hi0 = jnp.where(below, g_lo, jnp.where(above, highest, g_hi))
    hit = (cnt_lo == float(top_k)) | (cnt_hi == float(top_k))

    def bisect(state):
        it, lo, hi, settled = state
        mid = lo + (hi - lo) * 0.5
        cnt = count_ge(mid)
        enough = cnt >= float(top_k)
        open_ = settled == 0.0
        lo_new = jnp.where(open_ & enough, mid, lo)
        hi_new = jnp.where(open_ & jnp.logical_not(enough), mid, hi)
        done = (cnt == float(top_k)) | (mid <= lo) | (mid >= hi)
        return it + 1, lo_new, hi_new, jnp.where(done, 1.0, settled)

    def unsettled(state):
        it, _, _, settled = state
        return (it < SEARCH_STEPS_MAX) & (jnp.min(settled) == 0.0)

    _, lo, _, _ = lax.while_loop(
        unsettled, bisect, (jnp.int32(0), lo0, hi0, jnp.where(few_keys | hit, 1.0, 0.0)))
    thresh = jnp.where(few_keys, -jnp.inf, lo)

    def mask_chunk(c, carry):
        sc_ref[c] = jnp.where(sc_ref[c] >= thresh, 0.0, NEG)
        return carry

    lax.fori_loop(0, i, mask_chunk, 0)
    sc_ref[i] = jnp.where(causal, jnp.where(sc_ref[i] >= thresh, 0.0, NEG), NEG)

    scale = ATTN_HEAD_DIM ** -0.5
    m_ref[...] = jnp.full(m_ref.shape, NEG, F32)
    l_ref[...] = jnp.zeros(l_ref.shape, F32)
    acc_ref[...] = jnp.zeros(acc_ref.shape, F32)

    def attend(c, near):
        mbias = sc_ref[c]
        m_all, l_all = m_ref[...], l_ref[...]
        m_rows, l_rows = [], []

        def qk(h):
            cols = slice(h * ATTN_HEAD_DIM, (h + 1) * ATTN_HEAD_DIM)
            return jnp.dot(k_ref[rows(c), cols], qt_ref[cols, :], preferred_element_type=F32)

        ahead = N_ATTN_HEADS
        s_raw = [qk(h) for h in range(ahead)]
        for h in range(N_ATTN_HEADS):
            cols = slice(h * ATTN_HEAD_DIM, (h + 1) * ATTN_HEAD_DIM)
            s = s_raw[h] * scale + mbias
            if h + ahead < N_ATTN_HEADS:
                s_raw.append(qk(h + ahead))
            if near is not None:
                s = s + bias_ref[h, near * bq:(near + 1) * bq, :]
            m_old = m_all[h:h + 1, :]
            m_new = jnp.maximum(m_old, jnp.max(_reduce_row_groups(s, jnp.maximum), axis=0, keepdims=True))
            p = jnp.exp(s - m_new)
            a = jnp.exp(m_old - m_new)
            m_rows.append(m_new)
            l_rows.append(a * l_all[h:h + 1, :]
                          + jnp.sum(_reduce_row_groups(p, jnp.add), axis=0, keepdims=True))
            pv = lax.dot_general(v_ref[rows(c), cols], p.astype(BF16), TN_DIMS, preferred_element_type=F32)
            acc_ref[h] = a * acc_ref[h] + pv
        m_ref[...] = jnp.concatenate(m_rows, axis=0)
        l_ref[...] = jnp.concatenate(l_rows, axis=0)

    def far_chunk(c, carry):
        attend(c, None)
        return carry

    lax.fori_loop(0, jnp.maximum(i - 1, 0), far_chunk, 0)

    @pl.when(i > 0)
    def _():
        attend(i - 1, 0)

    attend(i, 1)
    for h in range(N_ATTN_HEADS):
        out_t = acc_ref[h] / l_ref[h:h + 1, :]
        o_ref[:, h * ATTN_HEAD_DIM:(h + 1) * ATTN_HEAD_DIM] = out_t.T.astype(o_ref.dtype)


def _rel_bucket(dist):
    n = jnp.maximum(dist, 0)
    max_exact = N_REL_BUCKETS // 2
    nf = jnp.maximum(n, 1).astype(F32)
    large = max_exact + (jnp.log(nf / max_exact) / math.log(REL_MAX_DISTANCE / max_exact)
                         * (N_REL_BUCKETS - max_exact)).astype(I32)
    large = jnp.minimum(large, N_REL_BUCKETS - 1)
    return jnp.where(n < max_exact, n, large)


def _near_bias_tiles(rel_bias, bq):
    assert bq >= REL_MAX_DISTANCE
    key = jnp.arange(2 * bq, dtype=I32)[:, None]
    qry = jnp.arange(bq, dtype=I32)[None, :]
    bucket = _rel_bucket(qry + bq - key)
    rel = (rel_bias - rel_bias[N_REL_BUCKETS - 1]).T
    tiles = jnp.broadcast_to(rel[:, 0][:, None, None], (rel.shape[0],) + bucket.shape)
    for b in range(1, N_REL_BUCKETS):
        tiles = jnp.where(bucket[None] >= b, rel[:, b][:, None, None], tiles)
    return tiles


def _dsa_attention(qt, kv, ki_b, wt, rel_bias, bq):
    S = kv.shape[0]
    top_k = min(TOPK_MAX, S // 4)
    n_blk = S // bq
    bias = _near_bias_tiles(rel_bias, bq)
    kernel = functools.partial(_dsa_kernel, bq=bq, top_k=top_k)
    return pl.pallas_call(
        kernel,
        grid=(n_blk,),
        in_specs=[pl.BlockSpec((N_IDX_HEADS * IDX_HEAD_DIM, bq), lambda i: (1, i)),
                  pl.BlockSpec((N_IDX_HEADS, bq), lambda i: (0, i)),
                  pl.BlockSpec((ATTN_WIDTH, bq), lambda i: (0, i)),
                  _resident((S, IDX_HEAD_DIM), lambda i: (0, 0)),
                  _resident((S, ATTN_WIDTH), lambda i: (0, 0)),
                  _resident((S, ATTN_WIDTH), lambda i: (0, 1)),
                  _resident((N_ATTN_HEADS, 2 * bq, bq), lambda i: (0, 0, 0))],
        out_specs=pl.BlockSpec((bq, ATTN_WIDTH), lambda i: (i, 0)),
        out_shape=jax.ShapeDtypeStruct((S, ATTN_WIDTH), BF16),
        scratch_shapes=[pltpu.VMEM((n_blk, bq, bq), F32),
                        pltpu.VMEM((N_ATTN_HEADS, bq), F32),
                        pltpu.VMEM((N_ATTN_HEADS, bq), F32),
                        pltpu.VMEM((N_ATTN_HEADS, ATTN_HEAD_DIM, bq), F32)],
        compiler_params=_params("arbitrary"),
        name="dsa_attention",
    )(qt, wt, qt, ki_b, kv, kv, bias)


def _sgu_kernel(gu_ref, gv_ref, g_ref, b_ref, ws_ref, bs_ref, o_ref, *, n_chunk):
    u = _gelu(gu_ref[...])
    v = _layer_norm(_gelu(gv_ref[...]), g_ref[...], b_ref[...]).astype(BF16)
    r = lax.broadcasted_iota(I32, (GMLP_CHUNK, GMLP_CHUNK), 0)
    c = lax.broadcasted_iota(I32, (GMLP_CHUNK, GMLP_CHUNK), 1)
    for g in range(GMLP_GROUPS):
        wg = jnp.where(c <= r, ws_ref[g], 0.0).astype(BF16)
        bcol = bs_ref[:, g:g + 1]
        cols = slice(g * LANES, (g + 1) * LANES)
        for n in range(n_chunk):
            rws = slice(n * GMLP_CHUNK, (n + 1) * GMLP_CHUNK)
            mixed = jnp.dot(wg, v[rws, cols], preferred_element_type=F32) + bcol
            o_ref[rws, cols] = (u[rws, cols] * mixed).astype(o_ref.dtype)


def _sgu(zg, ln_g, ln_b, w_s, b_s, tm):
    S = zg.shape[0]
    kernel = functools.partial(_sgu_kernel, n_chunk=tm // GMLP_CHUNK)
    return pl.pallas_call(
        kernel,
        grid=(S // tm,),
        in_specs=[pl.BlockSpec((tm, GMLP_WIDTH), lambda i: (i, 0)),
                  pl.BlockSpec((tm, GMLP_WIDTH), lambda i: (i, 1)),
                  pl.BlockSpec((1, GMLP_WIDTH), lambda i: (0, 0)),
                  pl.BlockSpec((1, GMLP_WIDTH), lambda i: (0, 0)),
                  pl.BlockSpec((GMLP_GROUPS, GMLP_CHUNK, GMLP_CHUNK), lambda i: (0, 0, 0)),
                  pl.BlockSpec((GMLP_CHUNK, GMLP_GROUPS), lambda i: (0, 0))],
        out_specs=pl.BlockSpec((tm, GMLP_WIDTH), lambda i: (i, 0)),
        out_shape=jax.ShapeDtypeStruct((S, GMLP_WIDTH), BF16),
        compiler_params=_params("parallel"),
        name="gmlp_sgu",
    )(zg, zg, ln_g.reshape(1, -1), ln_b.reshape(1, -1), w_s, b_s.T)


def _merge_kernel(at_ref, gm_ref, ga_ref, gg_ref, h_ref, wa_ref, wg_ref, wo_ref, g_ref, b_ref,
                  h1_ref, h1b_ref, h1bt_ref):
    a1 = jnp.dot(at_ref[...], wa_ref[...], preferred_element_type=F32)
    a2 = jnp.dot(gm_ref[...], wg_ref[...], preferred_element_type=F32)
    merged = jax.nn.sigmoid(ga_ref[...]) * a1 + jax.nn.sigmoid(gg_ref[...]) * a2
    y = DEEPNORM_ALPHA * h_ref[...] + jnp.dot(merged.astype(BF16), wo_ref[...], preferred_element_type=F32)
    h1 = _layer_norm(y, g_ref[...], b_ref[...])
    h1_ref[...] = h1
    h1b_ref[...] = h1.astype(BF16)
    h1bt_ref[...] = h1.T.astype(BF16)


def _merge(attn, gm, zg, h, wa, wg, wo, ln_g, ln_b, tm):
    S, D = h.shape
    W = attn.shape[1]
    return pl.pallas_call(
        _merge_kernel,
        grid=(S // tm,),
        in_specs=[pl.BlockSpec((tm, W), lambda i: (i, 0)),
                  pl.BlockSpec((tm, W), lambda i: (i, 0)),
                  pl.BlockSpec((tm, D), lambda i: (i, 1)),
                  pl.BlockSpec((tm, D), lambda i: (i, 2)),
                  pl.BlockSpec((tm, D), lambda i: (i, 0)),
                  _resident((W, D), lambda i: (0, 0)),
                  _resident((W, D), lambda i: (0, 0)),
                  _resident((D, D), lambda i: (0, 0)),
                  pl.BlockSpec((1, D), lambda i: (0, 0)),
                  pl.BlockSpec((1, D), lambda i: (0, 0))],
        out_specs=[pl.BlockSpec((tm, D), lambda i: (i, 0)),
                   pl.BlockSpec((tm, D), lambda i: (i, 0)),
                   pl.BlockSpec((D, tm), lambda i: (0, i))],
        out_shape=[jax.ShapeDtypeStruct((S, D), F32), jax.ShapeDtypeStruct((S, D), BF16),
                   jax.ShapeDtypeStruct((D, S), BF16)],
        compiler_params=_params("parallel"),
        name="merge_ln1",
    )(attn, gm, zg, zg, h, wa, wg, wo, ln_g.reshape(1, D), ln_b.reshape(1, D))


def _peer_pairs():
    pairs = [(a, b) for a in range(PEER_TOPK) for b in range(PEER_TOPK) if (a + 1) * (b + 1) <= PEER_TOPK]
    return sorted(pairs, key=lambda ab: ab[0] * PEER_TOPK + ab[1])


def _extract_top(work, n_rounds, on_pick, exact):
    n = float(work.shape[0])
    iota = lax.broadcasted_iota(I32, work.shape, 0).astype(F32)
    for r in range(n_rounds):
        mx = jnp.max(work, axis=0, keepdims=True)
        pick = work == mx
        if exact:
            first = jnp.min(jnp.where(pick, iota, n), axis=0, keepdims=True)
            pick = iota == first
        on_pick(r, mx, pick)
        work = jnp.where(pick, -jnp.inf, work)


def _route_kernel(hb_ref, wq_ref, sk_ref, u_ref, v_ref, l1_ref, a_ref, r2_ref, b_ref, ub_ref, vb_ref, *, T):
    ub_ref[...] = u_ref[...].astype(BF16)
    vb_ref[...] = v_ref[...].astype(BF16)
    qt = jnp.dot(hb_ref[...], wq_ref[...], preferred_element_type=F32).T
    pairs = _peer_pairs()

    def route(h, exact):
        miscount = jnp.zeros((1, T), F32)
        s, rank, vals = [], [], []
        for p in range(2):
            idx = 2 * h + p
            sp = jnp.dot(sk_ref[idx], qt[idx * LANES:(idx + 1) * LANES, :].astype(BF16),
                         preferred_element_type=F32)
            state = {"rank": jnp.full(sp.shape, float(PEER_TOPK), F32), "vals": []}

            def on_pick(r, mx, pick, state=state):
                state["rank"] = jnp.where(pick, float(r), state["rank"])
                state["vals"].append(mx)

            _extract_top(sp, PEER_TOPK, on_pick, exact)
            n_ranked = jnp.sum(jnp.where(state["rank"] < float(PEER_TOPK), 1.0, 0.0), axis=0, keepdims=True)
            miscount = jnp.maximum(miscount, jnp.abs(n_ranked - float(PEER_TOPK)))
            s.append(sp)
            rank.append(state["rank"])
            vals.append(state["vals"])
        cand = jnp.concatenate([vals[0][a] + vals[1][b] for a, b in pairs], axis=0)
        picked = {"mask": jnp.zeros(cand.shape, jnp.bool_)}

        def on_pick_c(r, mx, pick, picked=picked):
            picked["mask"] = picked["mask"] | pick

        _extract_top(cand, PEER_TOPK, on_pick_c, exact)
        top = vals[0][0] + vals[1][0]
        selw = jnp.where(picked["mask"], jnp.exp(cand - top), 0.0)
        z = jnp.sum(selw, axis=0, keepdims=True)
        self32 = jnp.where(picked["mask"], 1.0, 0.0)
        miscount = jnp.maximum(miscount, jnp.abs(jnp.sum(self32, axis=0, keepdims=True) - float(PEER_TOPK)))
        l1 = jnp.zeros(s[0].shape, F32)
        pos = 0
        for a in range(PEER_TOPK):
            cnt = sum(1 for ab in pairs if ab[0] == a)
            n_a = jnp.sum(self32[pos:pos + cnt, :], axis=0, keepdims=True)
            pos += cnt
            l1 = jnp.where(rank[0] == float(a), n_a, l1)
        l1_ref[h] = l1
        a_ref[h] = jnp.exp(s[0] - vals[0][0])
        r2_ref[h] = rank[1].astype(r2_ref.dtype)
        b_ref[h] = (jnp.exp(s[1] - vals[1][0]) / z).astype(b_ref.dtype)
        return miscount

    miscount = functools.reduce(jnp.maximum, [route(h, exact=False) for h in range(PEER_HEADS)])

    @pl.when(jnp.max(miscount) > 0.0)
    def _():
        for h in range(PEER_HEADS):
            route(h, exact=True)


def _peer_route(h1b, wq, sk, expert_u, expert_v, T):
    S, D = h1b.shape
    n_steps = S // T
    E = expert_u.shape[0]
    kernel = functools.partial(_route_kernel, T=T)
    shp = jax.ShapeDtypeStruct((PEER_HEADS, PEER_N_KEYS, S), F32)
    shp_b = jax.ShapeDtypeStruct((PEER_HEADS, PEER_N_KEYS, S), BF16)
    spec = pl.BlockSpec((PEER_HEADS, PEER_N_KEYS, T), lambda t: (0, 0, t))
    table = pl.BlockSpec((E // n_steps, D), lambda t: (t, 0))
    return pl.pallas_call(
        kernel,
        grid=(n_steps,),
        in_specs=[pl.BlockSpec((T, D), lambda t: (t, 0)),
                  _resident(wq.shape, lambda t: (0, 0)),
                  _resident(sk.shape, lambda t: (0, 0, 0)),
                  table, table],
        out_specs=[spec, spec, spec, spec, table, table],
        out_shape=[shp, shp, shp_b, shp_b,
                   jax.ShapeDtypeStruct(expert_u.shape, BF16), jax.ShapeDtypeStruct(expert_v.shape, BF16)],
        compiler_params=_params("parallel"),
        name="peer_route",
    )(h1b, wq, sk, expert_u, expert_v)


def _peer_kernel(ht_ref, u_ref, v_ref, l1_ref, a_ref, r2_ref, b_ref, o_ref, act_ref, *, n_i1, ce):
    @pl.when(pl.program_id(1) == 0)
    def _():
        o_ref[...] = jnp.zeros_like(o_ref)

    T = ht_ref.shape[1]
    n_slab = PEER_N_KEYS // PACKED_ROWS
    i1_per_chunk = ce // PEER_N_KEYS
    r2 = [r2_ref[h] for h in range(PEER_HEADS)]
    b = [b_ref[h] for h in range(PEER_HEADS)]

    def gates(c):
        tiles = []
        for j in range(c * i1_per_chunk, (c + 1) * i1_per_chunk):
            gate = [None] * n_slab
            for h in range(PEER_HEADS):
                l1 = jnp.broadcast_to(l1_ref[h, j:j + 1, :], (PACKED_ROWS, T)).astype(BF16)
                a = jnp.broadcast_to(a_ref[h, j:j + 1, :], (PACKED_ROWS, T)).astype(BF16)
                for s in range(n_slab):
                    slab = slice(s * PACKED_ROWS, (s + 1) * PACKED_ROWS)
                    val = jnp.where(r2[h][slab] < l1, b[h][slab], 0.0) * a
                    gate[s] = val if gate[s] is None else gate[s] + val
            tiles.extend(gate)
        return tiles

    n_chunk = n_i1 * PEER_N_KEYS // ce
    zero = jnp.minimum(pl.program_id(1), 0)
    for c in range(n_chunk):
        act_ref[c + zero] = jnp.dot(u_ref[c * ce:(c + 1) * ce, :], ht_ref[...],
                                    preferred_element_type=F32)
    tiles = []
    for c in range(n_chunk):
        gel = _gelu(act_ref[c + zero]).astype(BF16)
        g = gates(c)
        tiles.extend(gel[k * PACKED_ROWS:(k + 1) * PACKED_ROWS] * g[k] for k in range(len(g)))
    pt = jnp.concatenate(tiles, axis=0)
    o_ref[...] += lax.dot_general(pt, v_ref[...], TN_DIMS, preferred_element_type=F32)


def _peer_experts(h1bt, u_b, v_b, l1, a, r2, b, T, eb):
    D, S = h1bt.shape
    E = u_b.shape[0]
    n_i1 = eb // PEER_N_KEYS
    ce = MXU_DIM // 2
    kernel = functools.partial(_peer_kernel, n_i1=n_i1, ce=ce)
    row_spec = pl.BlockSpec((PEER_HEADS, n_i1, T), lambda t, e: (0, e, t))
    full_spec = pl.BlockSpec((PEER_HEADS, PEER_N_KEYS, T), lambda t, e: (0, 0, t))
    return pl.pallas_call(
        kernel,
        grid=(S // T, E // eb),
        in_specs=[pl.BlockSpec((D, T), lambda t, e: (0, t)),
                  pl.BlockSpec((eb, D), lambda t, e: (e, 0)),
                  pl.BlockSpec((eb, D), lambda t, e: (e, 0)),
                  row_spec, row_spec, full_spec, full_spec],
        out_specs=pl.BlockSpec((T, D), lambda t, e: (t, 0)),
        out_shape=jax.ShapeDtypeStruct((S, D), F32),
        scratch_shapes=[pltpu.VMEM((eb // ce, ce, T), F32)],
        compiler_params=_params("parallel", "arbitrary"),
        name="peer_experts",
    )(h1bt, u_b, v_b, l1, a, r2, b)


def _ple_kernel(h1_ref, peer_ref, p_ref, wg_ref, wp_ref, g_ref, b_ref, o_ref):
    r = DEEPNORM_ALPHA * h1_ref[...] + peer_ref[...]
    gate = jax.nn.sigmoid(jnp.dot(r.astype(BF16), wg_ref[...], preferred_element_type=F32))
    proj = jnp.dot(p_ref[...], wp_ref[...], preferred_element_type=F32)
    o_ref[...] = _layer_norm(r + gate * proj, g_ref[...], b_ref[...])


def _ple(h1, peer, p_b, wg, wp, ln_g, ln_b, tm):
    S, D = h1.shape
    P = p_b.shape[1]
    return pl.pallas_call(
        _ple_kernel,
        grid=(S // tm,),
        in_specs=[pl.BlockSpec((tm, D), lambda i: (i, 0)),
                  pl.BlockSpec((tm, D), lambda i: (i, 0)),
                  pl.BlockSpec((tm, P), lambda i: (i, 0)),
                  _resident((D, D), lambda i: (0, 0)),
                  _resident((P, D), lambda i: (0, 0)),
                  pl.BlockSpec((1, D), lambda i: (0, 0)),
                  pl.BlockSpec((1, D), lambda i: (0, 0))],
        out_specs=pl.BlockSpec((tm, D), lambda i: (i, 0)),
        out_shape=jax.ShapeDtypeStruct((S, D), F32),
        compiler_params=_params("parallel"),
        name="ple_ln2",
    )(h1, peer, p_b, wg, wp, ln_g.reshape(1, D), ln_b.reshape(1, D))


def _layer(x, p, ln_emb_g, ln_emb_b, rel_bias, w_in, gmlp_ln_g, gmlp_ln_b, gmlp_w_s, gmlp_b_s,
           w_br_attn, w_br_gmlp, w_mix_out, ln1_g, ln1_b, peer_w_q, peer_sub_keys, peer_u, peer_v,
           ple_w_proj, ple_w_gate, ln2_g, ln2_b):
    S, D = x.shape
    tm = min(512, S)
    o_k, o_v, o_qi = ATTN_WIDTH, 2 * ATTN_WIDTH, 3 * ATTN_WIDTH
    o_ki = o_qi + N_IDX_HEADS * IDX_HEAD_DIM
    o_gu = o_ki + IDX_HEAD_DIM + N_IDX_HEADS

    h, hb = _ln_embed(x, ln_emb_g, ln_emb_b, tm)
    wqt = jnp.concatenate([w_in[:, :o_k], w_in[:, o_qi:o_ki]], axis=1).T.astype(BF16)
    w1 = w_in[:, o_k:o_qi].astype(BF16)
    w2 = jnp.pad(w_in[:, o_ki:o_gu], ((0, 0), (0, LANES - (o_gu - o_ki)))).astype(BF16)
    w3 = w_in[:, o_gu:].astype(BF16)
    qt = _matmul_t(wqt, hb, BF16, tm, "in_proj_q")
    kv = _matmul(hb, w1, BF16, min(1024, S), 512, "in_proj_kv")
    kw = _matmul(hb, w2, F32, min(1024, S), LANES, "in_proj_index")
    zg = _matmul(hb, w3, F32, min(1024, S), 512, "in_proj_gmlp")

    attn = _dsa_attention(qt, kv, kw[:, :IDX_HEAD_DIM].astype(BF16),
                          kw[:, IDX_HEAD_DIM:IDX_HEAD_DIM + N_IDX_HEADS].T, rel_bias, bq=256)
    gm = _sgu(zg, gmlp_ln_g, gmlp_ln_b, gmlp_w_s, gmlp_b_s, tm)
    h1, h1b, h1bt = _merge(attn, gm, zg, h, w_br_attn.astype(BF16), w_br_gmlp.astype(BF16),
                           w_mix_out.astype(BF16), ln1_g, ln1_b, min(256, S))

    sk = peer_sub_keys.reshape(PEER_HEADS * 2, PEER_N_KEYS, -1).astype(BF16)
    l1, a, r2, b, u_b, v_b = _peer_route(h1b, peer_w_q.astype(BF16), sk, peer_u, peer_v, min(256, S))
    peer = _peer_experts(h1bt, u_b, v_b, l1, a, r2, b, T=min(512, S), eb=1024)
    return _ple(h1, peer, p.astype(BF16), ple_w_gate.astype(BF16), ple_w_proj.astype(BF16),
                ln2_g, ln2_b, min(256, S))


def kernel(x, p, positions, ln_emb_g, ln_emb_b, rel_bias, w_in, gmlp_ln_g, gmlp_ln_b, gmlp_w_s, gmlp_b_s, w_br_attn, w_br_gmlp, w_mix_out, ln1_g, ln1_b, peer_w_q, peer_sub_keys, peer_u, peer_v, ple_w_proj, ple_w_gate, ln2_g, ln2_b):
    del positions
    assert x.shape[0] == 1 and w_in.shape[0] == DEPTH
    out = _layer(x[0], p[0, 0], ln_emb_g, ln_emb_b, rel_bias, w_in[0], gmlp_ln_g[0], gmlp_ln_b[0],
                 gmlp_w_s[0], gmlp_b_s[0], w_br_attn[0], w_br_gmlp[0], w_mix_out[0], ln1_g[0],
                 ln1_b[0], peer_w_q[0], peer_sub_keys[0], peer_u[0], peer_v[0], ple_w_proj[0],
                 ple_w_gate[0], ln2_g[0], ln2_b[0])
    return out[None]
```

```python
import functools
import math

import jax
import jax.numpy as jnp
from jax import lax
from jax.experimental import pallas as pl
from jax.experimental.pallas import tpu as pltpu

F32 = jnp.float32
BF16 = jnp.bfloat16
I32 = jnp.int32

LN_EPS = 1e-5
DEPTH = 1
DEEPNORM_ALPHA = (2.0 * DEPTH) ** 0.25

N_ATTN_HEADS = 8
ATTN_HEAD_DIM = 128
ATTN_WIDTH = N_ATTN_HEADS * ATTN_HEAD_DIM
N_IDX_HEADS = 16
IDX_HEAD_DIM = 64
TOPK_MAX = 256
GMLP_CHUNK = 128
GMLP_GROUPS = 8
GMLP_WIDTH = 1024
N_REL_BUCKETS = 32
REL_MAX_DISTANCE = 128
PEER_HEADS = 8
PEER_N_KEYS = 128
PEER_TOPK = 16

LANES = 128
SUBLANES = 8
PACKED_ROWS = 2 * SUBLANES
MXU_DIM = 256
NEG = -1e30
SEARCH_STEPS_MAX = 64
NT_DIMS = (((1,), (1,)), ((), ()))
TN_DIMS = (((0,), (0,)), ((), ()))
VMEM_LIMIT = 56 * 1024 * 1024


def _params(*sem):
    return pltpu.CompilerParams(dimension_semantics=sem, vmem_limit_bytes=VMEM_LIMIT)


def _resident(shape, index_map):
    return pl.BlockSpec(shape, index_map, pipeline_mode=pl.Buffered(1))


def _gelu(x):
    return 0.5 * x * (1.0 + jnp.tanh(0.7978845608028654 * (x + 0.044715 * (x * x * x))))


def _layer_norm(x, g, b):
    mu = jnp.mean(x, axis=-1, keepdims=True)
    xc = x - mu
    var = jnp.mean(xc * xc, axis=-1, keepdims=True)
    return xc * lax.rsqrt(var + LN_EPS) * g + b


def _ln_kernel(x_ref, g_ref, b_ref, h_ref, hb_ref):
    y = _layer_norm(x_ref[...], g_ref[...], b_ref[...])
    h_ref[...] = y
    hb_ref[...] = y.astype(BF16)


def _ln_embed(x, g, b, tm):
    S, D = x.shape
    return pl.pallas_call(
        _ln_kernel,
        grid=(S // tm,),
        in_specs=[pl.BlockSpec((tm, D), lambda i: (i, 0)),
                  pl.BlockSpec((1, D), lambda i: (0, 0)),
                  pl.BlockSpec((1, D), lambda i: (0, 0))],
        out_specs=[pl.BlockSpec((tm, D), lambda i: (i, 0)),
                   pl.BlockSpec((tm, D), lambda i: (i, 0))],
        out_shape=[jax.ShapeDtypeStruct((S, D), F32), jax.ShapeDtypeStruct((S, D), BF16)],
        compiler_params=_params("parallel"),
        name="ln_embed",
    )(x, g.reshape(1, D), b.reshape(1, D))


def _mm_kernel(a_ref, w_ref, o_ref):
    o_ref[...] = jnp.dot(a_ref[...], w_ref[...], preferred_element_type=F32).astype(o_ref.dtype)


def _matmul(a, w, out_dtype, tm, tn, name):
    M, K = a.shape
    N = w.shape[1]
    return pl.pallas_call(
        _mm_kernel,
        grid=(M // tm, N // tn),
        in_specs=[pl.BlockSpec((tm, K), lambda i, j: (i, 0)),
                  pl.BlockSpec((K, tn), lambda i, j: (0, j))],
        out_specs=pl.BlockSpec((tm, tn), lambda i, j: (i, j)),
        out_shape=jax.ShapeDtypeStruct((M, N), out_dtype),
        compiler_params=_params("parallel", "arbitrary"),
        name=name,
    )(a, w)


def _mm_t_kernel(wt_ref, a_ref, o_ref):
    o_ref[...] = lax.dot_general(wt_ref[...], a_ref[...], NT_DIMS,
                                 preferred_element_type=F32).astype(o_ref.dtype)


def _matmul_t(wt, a, out_dtype, tm, name):
    M, K = a.shape
    N = wt.shape[0]
    return pl.pallas_call(
        _mm_t_kernel,
        grid=(M // tm,),
        in_specs=[_resident((N, K), lambda i: (0, 0)),
                  pl.BlockSpec((tm, K), lambda i: (i, 0))],
        out_specs=pl.BlockSpec((N, tm), lambda i: (0, i)),
        out_shape=jax.ShapeDtypeStruct((N, M), out_dtype),
        compiler_params=_params("parallel"),
        name=name,
    )(wt, a)


def _reduce_row_groups(t, op):
    parts = [t[j * SUBLANES:(j + 1) * SUBLANES, :] for j in range(t.shape[0] // SUBLANES)]
    while len(parts) > 1:
        parts = [op(parts[j], parts[j + 1]) for j in range(0, len(parts) - 1, 2)] + (
            [parts[-1]] if len(parts) % 2 else [])
    return parts[0]


def _dsa_kernel(qit_ref, wt_ref, qt_ref, ki_ref, k_ref, v_ref, bias_ref, o_ref,
                sc_ref, m_ref, l_ref, acc_ref, *, bq, top_k):
    i = pl.program_id(0)
    w = wt_ref[...] * ((IDX_HEAD_DIM ** -0.5) * (N_IDX_HEADS ** -0.5))
    key_i = lax.broadcasted_iota(I32, (bq, bq), 0)
    qry_i = lax.broadcasted_iota(I32, (bq, bq), 1)
    causal = key_i <= qry_i

    def rows(c):
        return pl.ds(pl.multiple_of(c * bq, bq), bq)

    def scores(c):
        kic = ki_ref[rows(c), :]
        acc = jnp.zeros((bq, bq), F32)
        for h in range(N_IDX_HEADS):
            d = jnp.dot(kic, qit_ref[h * IDX_HEAD_DIM:(h + 1) * IDX_HEAD_DIM, :],
                        preferred_element_type=F32)
            acc = acc + w[h:h + 1, :] * jnp.maximum(d, 0.0)
        return acc

    def score_chunk(c, carry):
        sc_ref[c] = scores(c)
        return carry

    lax.fori_loop(0, i, score_chunk, 0)
    sc_ref[i] = jnp.where(causal, scores(i), -jnp.inf)

    def lowest_highest(c, lh):
        x = sc_ref[c]
        return (jnp.minimum(lh[0], _reduce_row_groups(x, jnp.minimum)),
                jnp.maximum(lh[1], _reduce_row_groups(x, jnp.maximum)))

    lo8, hi8 = lax.fori_loop(0, i, lowest_highest, (jnp.full((SUBLANES, bq), jnp.inf, F32),
                                                    jnp.full((SUBLANES, bq), -jnp.inf, F32)))
    diag = sc_ref[i]
    lo8 = jnp.minimum(lo8, _reduce_row_groups(jnp.where(causal, diag, jnp.inf), jnp.minimum))
    hi8 = jnp.maximum(hi8, _reduce_row_groups(diag, jnp.maximum))
    lo0 = jnp.min(lo8, axis=0, keepdims=True)
    hi0 = jnp.max(hi8, axis=0, keepdims=True)

    def count_ge(cand):
        def body(c, cnt):
            return cnt + _reduce_row_groups(jnp.where(sc_ref[c] >= cand, 1.0, 0.0), jnp.add)
        cnt = lax.fori_loop(0, i + 1, body, jnp.zeros((SUBLANES, bq), F32))
        return jnp.sum(cnt, axis=0, keepdims=True)

    n_keys = (i * bq + 1 + lax.broadcasted_iota(I32, (1, bq), 1)).astype(F32)
    few_keys = n_keys < float(top_k)

    def bisect(state):
        it, lo, hi, settled = state
        mid = lo + (hi - lo) * 0.5
        cnt = count_ge(mid)
        enough = cnt >= float(top_k)
        open_ = settled == 0.0
        lo_new = jnp.where(open_ & enough, mid, lo)
        hi_new = jnp.where(open_ & jnp.logical_not(enough), mid, hi)
        done = (cnt == float(top_k)) | (mid <= lo) | (mid >= hi)
        return it + 1, lo_new, hi_new, jnp.where(done, 1.0, settled)

    def unsettled(state):
        it, _, _, settled = state
        return (it < SEARCH_STEPS_MAX) & (jnp.min(settled) == 0.0)

    _, lo, _, _ = lax.while_loop(
        unsettled, bisect, (jnp.int32(0), lo0, hi0, jnp.where(few_keys, 1.0, 0.0)))
    thresh = jnp.where(few_keys, -jnp.inf, lo)

    def mask_chunk(c, carry):
        sc_ref[c] = jnp.where(sc_ref[c] >= thresh, 0.0, NEG)
        return carry

    lax.fori_loop(0, i, mask_chunk, 0)
    sc_ref[i] = jnp.where(causal, jnp.where(sc_ref[i] >= thresh, 0.0, NEG), NEG)

    scale = ATTN_HEAD_DIM ** -0.5
    m_ref[...] = jnp.full(m_ref.shape, NEG, F32)
    l_ref[...] = jnp.zeros(l_ref.shape, F32)
    acc_ref[...] = jnp.zeros(acc_ref.shape, F32)

    def attend(c, near):
        mbias = sc_ref[c]
        m_all, l_all = m_ref[...], l_ref[...]
        m_rows, l_rows = [], []

        def qk(h):
            cols = slice(h * ATTN_HEAD_DIM, (h + 1) * ATTN_HEAD_DIM)
            return jnp.dot(k_ref[rows(c), cols], qt_ref[cols, :], preferred_element_type=F32)

        ahead = N_ATTN_HEADS
        s_raw = [qk(h) for h in range(ahead)]
        for h in range(N_ATTN_HEADS):
            cols = slice(h * ATTN_HEAD_DIM, (h + 1) * ATTN_HEAD_DIM)
            s = s_raw[h] * scale + mbias
            if h + ahead < N_ATTN_HEADS:
                s_raw.append(qk(h + ahead))
            if near is not None:
                s = s + bias_ref[h, near * bq:(near + 1) * bq, :]
            m_old = m_all[h:h + 1, :]
            m_new = jnp.maximum(m_old, jnp.max(_reduce_row_groups(s, jnp.maximum), axis=0, keepdims=True))
            p = jnp.exp(s - m_new)
            a = jnp.exp(m_old - m_new)
            m_rows.append(m_new)
            l_rows.append(a * l_all[h:h + 1, :]
                          + jnp.sum(_reduce_row_groups(p, jnp.add), axis=0, keepdims=True))
            pv = lax.dot_general(v_ref[rows(c), cols], p.astype(BF16), TN_DIMS, preferred_element_type=F32)
            acc_ref[h] = a * acc_ref[h] + pv
        m_ref[...] = jnp.concatenate(m_rows, axis=0)
        l_ref[...] = jnp.concatenate(l_rows, axis=0)

    def far_chunk(c, carry):
        attend(c, None)
        return carry

    lax.fori_loop(0, jnp.maximum(i - 1, 0), far_chunk, 0)

    @pl.when(i > 0)
    def _():
        attend(i - 1, 0)

    attend(i, 1)
    for h in range(N_ATTN_HEADS):
        out_t = acc_ref[h] / l_ref[h:h + 1, :]
        o_ref[:, h * ATTN_HEAD_DIM:(h + 1) * ATTN_HEAD_DIM] = out_t.T.astype(o_ref.dtype)


def _rel_bucket(dist):
    n = jnp.maximum(dist, 0)
    max_exact = N_REL_BUCKETS // 2
    nf = jnp.maximum(n, 1).astype(F32)
    large = max_exact + (jnp.log(nf / max_exact) / math.log(REL_MAX_DISTANCE / max_exact)
                         * (N_REL_BUCKETS - max_exact)).astype(I32)
    large = jnp.minimum(large, N_REL_BUCKETS - 1)
    return jnp.where(n < max_exact, n, large)


def _near_bias_tiles(rel_bias, bq):
    assert bq >= REL_MAX_DISTANCE
    key = jnp.arange(2 * bq, dtype=I32)[:, None]
    qry = jnp.arange(bq, dtype=I32)[None, :]
    bucket = _rel_bucket(qry + bq - key)
    rel = (rel_bias - rel_bias[N_REL_BUCKETS - 1]).T
    tiles = jnp.broadcast_to(rel[:, 0][:, None, None], (rel.shape[0],) + bucket.shape)
    for b in range(1, N_REL_BUCKETS):
        tiles = jnp.where(bucket[None] >= b, rel[:, b][:, None, None], tiles)
    return tiles


def _dsa_attention(qt, kv, ki_b, wt, rel_bias, bq):
    S = kv.shape[0]
    top_k = min(TOPK_MAX, S // 4)
    n_blk = S // bq
    bias = _near_bias_tiles(rel_bias, bq)
    kernel = functools.partial(_dsa_kernel, bq=bq, top_k=top_k)
    return pl.pallas_call(
        kernel,
        grid=(n_blk,),
        in_specs=[pl.BlockSpec((N_IDX_HEADS * IDX_HEAD_DIM, bq), lambda i: (1, i)),
                  pl.BlockSpec((N_IDX_HEADS, bq), lambda i: (0, i)),
                  pl.BlockSpec((ATTN_WIDTH, bq), lambda i: (0, i)),
                  _resident((S, IDX_HEAD_DIM), lambda i: (0, 0)),
                  _resident((S, ATTN_WIDTH), lambda i: (0, 0)),
                  _resident((S, ATTN_WIDTH), lambda i: (0, 1)),
                  _resident((N_ATTN_HEADS, 2 * bq, bq), lambda i: (0, 0, 0))],
        out_specs=pl.BlockSpec((bq, ATTN_WIDTH), lambda i: (i, 0)),
        out_shape=jax.ShapeDtypeStruct((S, ATTN_WIDTH), BF16),
        scratch_shapes=[pltpu.VMEM((n_blk, bq, bq), F32),
                        pltpu.VMEM((N_ATTN_HEADS, bq), F32),
                        pltpu.VMEM((N_ATTN_HEADS, bq), F32),
                        pltpu.VMEM((N_ATTN_HEADS, ATTN_HEAD_DIM, bq), F32)],
        compiler_params=_params("arbitrary"),
        name="dsa_attention",
    )(qt, wt, qt, ki_b, kv, kv, bias)


def _sgu_kernel(gu_ref, gv_ref, g_ref, b_ref, ws_ref, bs_ref, o_ref, *, n_chunk):
    u = _gelu(gu_ref[...])
    v = _layer_norm(_gelu(gv_ref[...]), g_ref[...], b_ref[...]).astype(BF16)
    r = lax.broadcasted_iota(I32, (GMLP_CHUNK, GMLP_CHUNK), 0)
    c = lax.broadcasted_iota(I32, (GMLP_CHUNK, GMLP_CHUNK), 1)
    for g in range(GMLP_GROUPS):
        wg = jnp.where(c <= r, ws_ref[g], 0.0).astype(BF16)
        bcol = bs_ref[:, g:g + 1]
        cols = slice(g * LANES, (g + 1) * LANES)
        for n in range(n_chunk):
            rws = slice(n * GMLP_CHUNK, (n + 1) * GMLP_CHUNK)
            mixed = jnp.dot(wg, v[rws, cols], preferred_element_type=F32) + bcol
            o_ref[rws, cols] = (u[rws, cols] * mixed).astype(o_ref.dtype)


def _sgu(zg, ln_g, ln_b, w_s, b_s, tm):
    S = zg.shape[0]
    kernel = functools.partial(_sgu_kernel, n_chunk=tm // GMLP_CHUNK)
    return pl.pallas_call(
        kernel,
        grid=(S // tm,),
        in_specs=[pl.BlockSpec((tm, GMLP_WIDTH), lambda i: (i, 0)),
                  pl.BlockSpec((tm, GMLP_WIDTH), lambda i: (i, 1)),
                  pl.BlockSpec((1, GMLP_WIDTH), lambda i: (0, 0)),
                  pl.BlockSpec((1, GMLP_WIDTH), lambda i: (0, 0)),
                  pl.BlockSpec((GMLP_GROUPS, GMLP_CHUNK, GMLP_CHUNK), lambda i: (0, 0, 0)),
                  pl.BlockSpec((GMLP_CHUNK, GMLP_GROUPS), lambda i: (0, 0))],
        out_specs=pl.BlockSpec((tm, GMLP_WIDTH), lambda i: (i, 0)),
        out_shape=jax.ShapeDtypeStruct((S, GMLP_WIDTH), BF16),
        compiler_params=_params("parallel"),
        name="gmlp_sgu",
    )(zg, zg, ln_g.reshape(1, -1), ln_b.reshape(1, -1), w_s, b_s.T)


def _merge_kernel(at_ref, gm_ref, ga_ref, gg_ref, h_ref, wa_ref, wg_ref, wo_ref, g_ref, b_ref,
                  h1_ref, h1b_ref, h1bt_ref):
    a1 = jnp.dot(at_ref[...], wa_ref[...], preferred_element_type=F32)
    a2 = jnp.dot(gm_ref[...], wg_ref[...], preferred_element_type=F32)
    merged = jax.nn.sigmoid(ga_ref[...]) * a1 + jax.nn.sigmoid(gg_ref[...]) * a2
    y = DEEPNORM_ALPHA * h_ref[...] + jnp.dot(merged.astype(BF16), wo_ref[...], preferred_element_type=F32)
    h1 = _layer_norm(y, g_ref[...], b_ref[...])
    h1_ref[...] = h1
    h1b_ref[...] = h1.astype(BF16)
    h1bt_ref[...] = h1.T.astype(BF16)


def _merge(attn, gm, zg, h, wa, wg, wo, ln_g, ln_b, tm):
    S, D = h.shape
    W = attn.shape[1]
    return pl.pallas_call(
        _merge_kernel,
        grid=(S // tm,),
        in_specs=[pl.BlockSpec((tm, W), lambda i: (i, 0)),
                  pl.BlockSpec((tm, W), lambda i: (i, 0)),
                  pl.BlockSpec((tm, D), lambda i: (i, 1)),
                  pl.BlockSpec((tm, D), lambda i: (i, 2)),
                  pl.BlockSpec((tm, D), lambda i: (i, 0)),
                  _resident((W, D), lambda i: (0, 0)),
                  _resident((W, D), lambda i: (0, 0)),
                  _resident((D, D), lambda i: (0, 0)),
                  pl.BlockSpec((1, D), lambda i: (0, 0)),
                  pl.BlockSpec((1, D), lambda i: (0, 0))],
        out_specs=[pl.BlockSpec((tm, D), lambda i: (i, 0)),
                   pl.BlockSpec((tm, D), lambda i: (i, 0)),
                   pl.BlockSpec((D, tm), lambda i: (0, i))],
        out_shape=[jax.ShapeDtypeStruct((S, D), F32), jax.ShapeDtypeStruct((S, D), BF16),
                   jax.ShapeDtypeStruct((D, S), BF16)],
        compiler_params=_params("parallel"),
        name="merge_ln1",
    )(attn, gm, zg, zg, h, wa, wg, wo, ln_g.reshape(1, D), ln_b.reshape(1, D))


def _peer_pairs():
    pairs = [(a, b) for a in range(PEER_TOPK) for b in range(PEER_TOPK) if (a + 1) * (b + 1) <= PEER_TOPK]
    return sorted(pairs, key=lambda ab: ab[0] * PEER_TOPK + ab[1])


def _extract_top(work, n_rounds, on_pick, exact):
    n = float(work.shape[0])
    iota = lax.broadcasted_iota(I32, work.shape, 0).astype(F32)
    for r in range(n_rounds):
        mx = jnp.max(work, axis=0, keepdims=True)
        pick = work == mx
        if exact:
            first = jnp.min(jnp.where(pick, iota, n), axis=0, keepdims=True)
            pick = iota == first
        on_pick(r, mx, pick)
        work = jnp.where(pick, -jnp.inf, work)


def _route_kernel(hb_ref, wq_ref, sk_ref, u_ref, v_ref, l1_ref, a_ref, r2_ref, b_ref, ub_ref, vb_ref, *, T):
    ub_ref[...] = u_ref[...].astype(BF16)
    vb_ref[...] = v_ref[...].astype(BF16)
    qt = jnp.dot(hb_ref[...], wq_ref[...], preferred_element_type=F32).T
    pairs = _peer_pairs()

    def route(h, exact):
        miscount = jnp.zeros((1, T), F32)
        s, rank, vals = [], [], []
        for p in range(2):
            idx = 2 * h + p
            sp = jnp.dot(sk_ref[idx], qt[idx * LANES:(idx + 1) * LANES, :].astype(BF16),
                         preferred_element_type=F32)
            state = {"rank": jnp.full(sp.shape, float(PEER_TOPK), F32), "vals": []}

            def on_pick(r, mx, pick, state=state):
                state["rank"] = jnp.where(pick, float(r), state["rank"])
                state["vals"].append(mx)

            _extract_top(sp, PEER_TOPK, on_pick, exact)
            n_ranked = jnp.sum(jnp.where(state["rank"] < float(PEER_TOPK), 1.0, 0.0), axis=0, keepdims=True)
            miscount = jnp.maximum(miscount, jnp.abs(n_ranked - float(PEER_TOPK)))
            s.append(sp)
            rank.append(state["rank"])
            vals.append(state["vals"])
        cand = jnp.concatenate([vals[0][a] + vals[1][b] for a, b in pairs], axis=0)
        picked = {"mask": jnp.zeros(cand.shape, jnp.bool_)}

        def on_pick_c(r, mx, pick, picked=picked):
            picked["mask"] = picked["mask"] | pick

        _extract_top(cand, PEER_TOPK, on_pick_c, True)
        top = vals[0][0] + vals[1][0]
        selw = jnp.where(picked["mask"], jnp.exp(cand - top), 0.0)
        z = jnp.sum(selw, axis=0, keepdims=True)
        self32 = jnp.where(picked["mask"], 1.0, 0.0)
        l1 = jnp.zeros(s[0].shape, F32)
        pos = 0
        for a in range(PEER_TOPK):
            cnt = sum(1 for ab in pairs if ab[0] == a)
            n_a = jnp.sum(self32[pos:pos + cnt, :], axis=0, keepdims=True)
            pos += cnt
            l1 = jnp.where(rank[0] == float(a), n_a, l1)
        l1_ref[h] = l1
        a_ref[h] = jnp.exp(s[0] - vals[0][0])
        r2_ref[h] = rank[1].astype(r2_ref.dtype)
        b_ref[h] = (jnp.exp(s[1] - vals[1][0]) / z).astype(b_ref.dtype)
        return miscount

    miscount = functools.reduce(jnp.maximum, [route(h, exact=False) for h in range(PEER_HEADS)])

    @pl.when(jnp.max(miscount) > 0.0)
    def _():
        for h in range(PEER_HEADS):
            route(h, exact=True)


def _peer_route(h1b, wq, sk, expert_u, expert_v, T):
    S, D = h1b.shape
    n_steps = S // T
    E = expert_u.shape[0]
    kernel = functools.partial(_route_kernel, T=T)
    shp = jax.ShapeDtypeStruct((PEER_HEADS, PEER_N_KEYS, S), F32)
    shp_b = jax.ShapeDtypeStruct((PEER_HEADS, PEER_N_KEYS, S), BF16)
    spec = pl.BlockSpec((PEER_HEADS, PEER_N_KEYS, T), lambda t: (0, 0, t))
    table = pl.BlockSpec((E // n_steps, D), lambda t: (t, 0))
    return pl.pallas_call(
        kernel,
        grid=(n_steps,),
        in_specs=[pl.BlockSpec((T, D), lambda t: (t, 0)),
                  _resident(wq.shape, lambda t: (0, 0)),
                  _resident(sk.shape, lambda t: (0, 0, 0)),
                  table, table],
        out_specs=[spec, spec, spec, spec, table, table],
        out_shape=[shp, shp, shp_b, shp_b,
                   jax.ShapeDtypeStruct(expert_u.shape, BF16), jax.ShapeDtypeStruct(expert_v.shape, BF16)],
        compiler_params=_params("parallel"),
        name="peer_route",
    )(h1b, wq, sk, expert_u, expert_v)


def _peer_kernel(ht_ref, u_ref, v_ref, l1_ref, a_ref, r2_ref, b_ref, o_ref, act_ref, *, n_i1, ce):
    @pl.when(pl.program_id(1) == 0)
    def _():
        o_ref[...] = jnp.zeros_like(o_ref)

    T = ht_ref.shape[1]
    n_slab = PEER_N_KEYS // PACKED_ROWS
    i1_per_chunk = ce // PEER_N_KEYS
    r2 = [r2_ref[h] for h in range(PEER_HEADS)]
    b = [b_ref[h] for h in range(PEER_HEADS)]

    def gates(c):
        tiles = []
        for j in range(c * i1_per_chunk, (c + 1) * i1_per_chunk):
            gate = [None] * n_slab
            for h in range(PEER_HEADS):
                l1 = jnp.broadcast_to(l1_ref[h, j:j + 1, :], (PACKED_ROWS, T)).astype(BF16)
                a = jnp.broadcast_to(a_ref[h, j:j + 1, :], (PACKED_ROWS, T)).astype(BF16)
                for s in range(n_slab):
                    slab = slice(s * PACKED_ROWS, (s + 1) * PACKED_ROWS)
                    val = jnp.where(r2[h][slab] < l1, b[h][slab], 0.0) * a
                    gate[s] = val if gate[s] is None else gate[s] + val
            tiles.extend(gate)
        return tiles

    n_chunk = n_i1 * PEER_N_KEYS // ce
    zero = jnp.minimum(pl.program_id(1), 0)
    for c in range(n_chunk):
        act_ref[c + zero] = jnp.dot(u_ref[c * ce:(c + 1) * ce, :], ht_ref[...],
                                    preferred_element_type=F32)
    tiles = []
    for c in range(n_chunk):
        gel = _gelu(act_ref[c + zero]).astype(BF16)
        g = gates(c)
        tiles.extend(gel[k * PACKED_ROWS:(k + 1) * PACKED_ROWS] * g[k] for k in range(len(g)))
    pt = jnp.concatenate(tiles, axis=0)
    o_ref[...] += lax.dot_general(pt, v_ref[...], TN_DIMS, preferred_element_type=F32)


def _peer_experts(h1bt, u_b, v_b, l1, a, r2, b, T, eb):
    D, S = h1bt.shape
    E = u_b.shape[0]
    n_i1 = eb // PEER_N_KEYS
    ce = MXU_DIM // 2
    kernel = functools.partial(_peer_kernel, n_i1=n_i1, ce=ce)
    row_spec = pl.BlockSpec((PEER_HEADS, n_i1, T), lambda t, e: (0, e, t))
    full_spec = pl.BlockSpec((PEER_HEADS, PEER_N_KEYS, T), lambda t, e: (0, 0, t))
    return pl.pallas_call(
        kernel,
        grid=(S // T, E // eb),
        in_specs=[pl.BlockSpec((D, T), lambda t, e: (0, t)),
                  pl.BlockSpec((eb, D), lambda t, e: (e, 0)),
                  pl.BlockSpec((eb, D), lambda t, e: (e, 0)),
                  row_spec, row_spec, full_spec, full_spec],
        out_specs=pl.BlockSpec((T, D), lambda t, e: (t, 0)),
        out_shape=jax.ShapeDtypeStruct((S, D), F32),
        scratch_shapes=[pltpu.VMEM((eb // ce, ce, T), F32)],
        compiler_params=_params("parallel", "arbitrary"),
        name="peer_experts",
    )(h1bt, u_b, v_b, l1, a, r2, b)


def _ple_kernel(h1_ref, peer_ref, p_ref, wg_ref, wp_ref, g_ref, b_ref, o_ref):
    r = DEEPNORM_ALPHA * h1_ref[...] + peer_ref[...]
    gate = jax.nn.sigmoid(jnp.dot(r.astype(BF16), wg_ref[...], preferred_element_type=F32))
    proj = jnp.dot(p_ref[...], wp_ref[...], preferred_element_type=F32)
    o_ref[...] = _layer_norm(r + gate * proj, g_ref[...], b_ref[...])


def _ple(h1, peer, p_b, wg, wp, ln_g, ln_b, tm):
    S, D = h1.shape
    P = p_b.shape[1]
    return pl.pallas_call(
        _ple_kernel,
        grid=(S // tm,),
        in_specs=[pl.BlockSpec((tm, D), lambda i: (i, 0)),
                  pl.BlockSpec((tm, D), lambda i: (i, 0)),
                  pl.BlockSpec((tm, P), lambda i: (i, 0)),
                  _resident((D, D), lambda i: (0, 0)),
                  _resident((P, D), lambda i: (0, 0)),
                  pl.BlockSpec((1, D), lambda i: (0, 0)),
                  pl.BlockSpec((1, D), lambda i: (0, 0))],
        out_specs=pl.BlockSpec((tm, D), lambda i: (i, 0)),
        out_shape=jax.ShapeDtypeStruct((S, D), F32),
        compiler_params=_params("parallel"),
        name="ple_ln2",
    )(h1, peer, p_b, wg, wp, ln_g.reshape(1, D), ln_b.reshape(1, D))


def _layer(x, p, ln_emb_g, ln_emb_b, rel_bias, w_in, gmlp_ln_g, gmlp_ln_b, gmlp_w_s, gmlp_b_s,
           w_br_attn, w_br_gmlp, w_mix_out, ln1_g, ln1_b, peer_w_q, peer_sub_keys, peer_u, peer_v,
           ple_w_proj, ple_w_gate, ln2_g, ln2_b):
    S, D = x.shape
    tm = min(512, S)
    o_k, o_v, o_qi = ATTN_WIDTH, 2 * ATTN_WIDTH, 3 * ATTN_WIDTH
    o_ki = o_qi + N_IDX_HEADS * IDX_HEAD_DIM
    o_gu = o_ki + IDX_HEAD_DIM + N_IDX_HEADS

    h, hb = _ln_embed(x, ln_emb_g, ln_emb_b, tm)
    wqt = jnp.concatenate([w_in[:, :o_k], w_in[:, o_qi:o_ki]], axis=1).T.astype(BF16)
    w1 = w_in[:, o_k:o_qi].astype(BF16)
    w2 = jnp.pad(w_in[:, o_ki:o_gu], ((0, 0), (0, LANES - (o_gu - o_ki)))).astype(BF16)
    w3 = w_in[:, o_gu:].astype(BF16)
    qt = _matmul_t(wqt, hb, BF16, tm, "in_proj_q")
    kv = _matmul(hb, w1, BF16, min(1024, S), 512, "in_proj_kv")
    kw = _matmul(hb, w2, F32, min(1024, S), LANES, "in_proj_index")
    zg = _matmul(hb, w3, F32, min(1024, S), 512, "in_proj_gmlp")

    attn = _dsa_attention(qt, kv, kw[:, :IDX_HEAD_DIM].astype(BF16),
                          kw[:, IDX_HEAD_DIM:IDX_HEAD_DIM + N_IDX_HEADS].T, rel_bias, bq=256)
    gm = _sgu(zg, gmlp_ln_g, gmlp_ln_b, gmlp_w_s, gmlp_b_s, tm)
    h1, h1b, h1bt = _merge(attn, gm, zg, h, w_br_attn.astype(BF16), w_br_gmlp.astype(BF16),
                           w_mix_out.astype(BF16), ln1_g, ln1_b, min(256, S))

    sk = peer_sub_keys.reshape(PEER_HEADS * 2, PEER_N_KEYS, -1).astype(BF16)
    l1, a, r2, b, u_b, v_b = _peer_route(h1b, peer_w_q.astype(BF16), sk, peer_u, peer_v, min(256, S))
    peer = _peer_experts(h1bt, u_b, v_b, l1, a, r2, b, T=min(512, S), eb=1024)
    return _ple(h1, peer, p.astype(BF16), ple_w_gate.astype(BF16), ple_w_proj.astype(BF16),
                ln2_g, ln2_b, min(512, S))


def kernel(x, p, positions, ln_emb_g, ln_emb_b, rel_bias, w_in, gmlp_ln_g, gmlp_ln_b, gmlp_w_s, gmlp_b_s, w_br_attn, w_br_gmlp, w_mix_out, ln1_g, ln1_b, peer_w_q, peer_sub_keys, peer_u, peer_v, ple_w_proj, ple_w_gate, ln2_g, ln2_b):
    del positions
    assert x.shape[0] == 1 and w_in.shape[0] == DEPTH
    out = _layer(x[0], p[0, 0], ln_emb_g, ln_emb_b, rel_bias, w_in[0], gmlp_ln_g[0], gmlp_ln_b[0],
                 gmlp_w_s[0], gmlp_b_s[0], w_br_attn[0], w_br_gmlp[0], w_mix_out[0], ln1_g[0],
                 ln1_b[0], peer_w_q[0], peer_sub_keys[0], peer_u[0], peer_v[0], ple_w_proj[0],
                 ple_w_gate[0], ln2_g[0], ln2_b[0])
    return out[None]
```

```python
import functools
import math

import jax
import jax.numpy as jnp
from jax import lax
from jax.experimental import pallas as pl
from jax.experimental.pallas import tpu as pltpu

F32 = jnp.float32
BF16 = jnp.bfloat16
I32 = jnp.int32

LN_EPS = 1e-5
DEPTH = 1
DEEPNORM_ALPHA = (2.0 * DEPTH) ** 0.25

N_ATTN_HEADS = 8
ATTN_HEAD_DIM = 128
ATTN_WIDTH = N_ATTN_HEADS * ATTN_HEAD_DIM
N_IDX_HEADS = 16
IDX_HEAD_DIM = 64
TOPK_MAX = 256
GMLP_CHUNK = 128
GMLP_GROUPS = 8
GMLP_WIDTH = 1024
N_REL_BUCKETS = 32
REL_MAX_DISTANCE = 128
PEER_HEADS = 8
PEER_N_KEYS = 128
PEER_TOPK = 16

LANES = 128
SUBLANES = 8
PACKED_ROWS = 2 * SUBLANES
MXU_DIM = 256
NEG = -1e30
SEARCH_STEPS_MAX = 64
NT_DIMS = (((1,), (1,)), ((), ()))
TN_DIMS = (((0,), (0,)), ((), ()))
VMEM_LIMIT = 56 * 1024 * 1024


def _params(*sem):
    return pltpu.CompilerParams(dimension_semantics=sem, vmem_limit_bytes=VMEM_LIMIT)


def _resident(shape, index_map):
    return pl.BlockSpec(shape, index_map, pipeline_mode=pl.Buffered(1))


def _gelu(x):
    return 0.5 * x * (1.0 + jnp.tanh(0.7978845608028654 * (x + 0.044715 * (x * x * x))))


def _layer_norm(x, g, b):
    mu = jnp.mean(x, axis=-1, keepdims=True)
    xc = x - mu
    var = jnp.mean(xc * xc, axis=-1, keepdims=True)
    return xc * lax.rsqrt(var + LN_EPS) * g + b


def _ln_kernel(x_ref, g_ref, b_ref, h_ref, hb_ref):
    y = _layer_norm(x_ref[...], g_ref[...], b_ref[...])
    h_ref[...] = y
    hb_ref[...] = y.astype(BF16)


def _ln_embed(x, g, b, tm):
    S, D = x.shape
    return pl.pallas_call(
        _ln_kernel,
        grid=(S // tm,),
        in_specs=[pl.BlockSpec((tm, D), lambda i: (i, 0)),
                  pl.BlockSpec((1, D), lambda i: (0, 0)),
                  pl.BlockSpec((1, D), lambda i: (0, 0))],
        out_specs=[pl.BlockSpec((tm, D), lambda i: (i, 0)),
                   pl.BlockSpec((tm, D), lambda i: (i, 0))],
        out_shape=[jax.ShapeDtypeStruct((S, D), F32), jax.ShapeDtypeStruct((S, D), BF16)],
        compiler_params=_params("parallel"),
        name="ln_embed",
    )(x, g.reshape(1, D), b.reshape(1, D))


def _mm_kernel(a_ref, w_ref, o_ref):
    o_ref[...] = jnp.dot(a_ref[...], w_ref[...], preferred_element_type=F32).astype(o_ref.dtype)


def _matmul(a, w, out_dtype, tm, tn, name):
    M, K = a.shape
    N = w.shape[1]
    return pl.pallas_call(
        _mm_kernel,
        grid=(M // tm, N // tn),
        in_specs=[pl.BlockSpec((tm, K), lambda i, j: (i, 0)),
                  pl.BlockSpec((K, tn), lambda i, j: (0, j))],
        out_specs=pl.BlockSpec((tm, tn), lambda i, j: (i, j)),
        out_shape=jax.ShapeDtypeStruct((M, N), out_dtype),
        compiler_params=_params("parallel", "arbitrary"),
        name=name,
    )(a, w)


def _mm_t_kernel(wt_ref, a_ref, o_ref):
    o_ref[...] = lax.dot_general(wt_ref[...], a_ref[...], NT_DIMS,
                                 preferred_element_type=F32).astype(o_ref.dtype)


def _matmul_t(wt, a, out_dtype, tm, name):
    M, K = a.shape
    N = wt.shape[0]
    return pl.pallas_call(
        _mm_t_kernel,
        grid=(M // tm,),
        in_specs=[_resident((N, K), lambda i: (0, 0)),
                  pl.BlockSpec((tm, K), lambda i: (i, 0))],
        out_specs=pl.BlockSpec((N, tm), lambda i: (0, i)),
        out_shape=jax.ShapeDtypeStruct((N, M), out_dtype),
        compiler_params=_params("parallel"),
        name=name,
    )(wt, a)


def _reduce_row_groups(t, op):
    parts = [t[j * SUBLANES:(j + 1) * SUBLANES, :] for j in range(t.shape[0] // SUBLANES)]
    while len(parts) > 1:
        parts = [op(parts[j], parts[j + 1]) for j in range(0, len(parts) - 1, 2)] + (
            [parts[-1]] if len(parts) % 2 else [])
    return parts[0]


def _dsa_kernel(qit_ref, wt_ref, qt_ref, ki_ref, k_ref, v_ref, bias_ref, o_ref,
                sc_ref, m_ref, l_ref, acc_ref, *, bq, top_k):
    i = pl.program_id(0)
    w = wt_ref[...] * ((IDX_HEAD_DIM ** -0.5) * (N_IDX_HEADS ** -0.5))
    key_i = lax.broadcasted_iota(I32, (bq, bq), 0)
    qry_i = lax.broadcasted_iota(I32, (bq, bq), 1)
    causal = key_i <= qry_i

    def rows(c):
        return pl.ds(pl.multiple_of(c * bq, bq), bq)

    def scores(c):
        kic = ki_ref[rows(c), :]
        acc = jnp.zeros((bq, bq), F32)
        for h in range(N_IDX_HEADS):
            d = jnp.dot(kic, qit_ref[h * IDX_HEAD_DIM:(h + 1) * IDX_HEAD_DIM, :],
                        preferred_element_type=F32)
            acc = acc + w[h:h + 1, :] * jnp.maximum(d, 0.0)
        return acc

    def score_chunk(c, carry):
        sc_ref[c] = scores(c)
        return carry

    lax.fori_loop(0, i, score_chunk, 0)
    sc_ref[i] = jnp.where(causal, scores(i), -jnp.inf)

    def lowest_highest(c, lh):
        x = sc_ref[c]
        return (jnp.minimum(lh[0], _reduce_row_groups(x, jnp.minimum)),
                jnp.maximum(lh[1], _reduce_row_groups(x, jnp.maximum)))

    lo8, hi8 = lax.fori_loop(0, i, lowest_highest, (jnp.full((SUBLANES, bq), jnp.inf, F32),
                                                    jnp.full((SUBLANES, bq), -jnp.inf, F32)))
    diag = sc_ref[i]
    lo8 = jnp.minimum(lo8, _reduce_row_groups(jnp.where(causal, diag, jnp.inf), jnp.minimum))
    hi8 = jnp.maximum(hi8, _reduce_row_groups(diag, jnp.maximum))
    lo0 = jnp.min(lo8, axis=0, keepdims=True)
    hi0 = jnp.max(hi8, axis=0, keepdims=True)

    def count_ge(cand):
        def body(c, cnt):
            return cnt + _reduce_row_groups(jnp.where(sc_ref[c] >= cand, 1.0, 0.0), jnp.add)
        cnt = lax.fori_loop(0, i + 1, body, jnp.zeros((SUBLANES, bq), F32))
        return jnp.sum(cnt, axis=0, keepdims=True)

    n_keys = (i * bq + 1 + lax.broadcasted_iota(I32, (1, bq), 1)).astype(F32)
    few_keys = n_keys < float(top_k)

    def bisect(state):
        it, lo, hi, settled = state
        mid = lo + (hi - lo) * 0.5
        cnt = count_ge(mid)
        enough = cnt >= float(top_k)
        open_ = settled == 0.0
        lo_new = jnp.where(open_ & enough, mid, lo)
        hi_new = jnp.where(open_ & jnp.logical_not(enough), mid, hi)
        done = (cnt == float(top_k)) | (mid <= lo) | (mid >= hi)
        return it + 1, lo_new, hi_new, jnp.where(done, 1.0, settled)

    def unsettled(state):
        it, _, _, settled = state
        return (it < SEARCH_STEPS_MAX) & (jnp.min(settled) == 0.0)

    _, lo, _, _ = lax.while_loop(
        unsettled, bisect, (jnp.int32(0), lo0, hi0, jnp.where(few_keys, 1.0, 0.0)))
    thresh = jnp.where(few_keys, -jnp.inf, lo)

    def mask_chunk(c, carry):
        sc_ref[c] = jnp.where(sc_ref[c] >= thresh, 0.0, NEG)
        return carry

    lax.fori_loop(0, i, mask_chunk, 0)
    sc_ref[i] = jnp.where(causal, jnp.where(sc_ref[i] >= thresh, 0.0, NEG), NEG)

    scale = ATTN_HEAD_DIM ** -0.5
    m_ref[...] = jnp.full(m_ref.shape, NEG, F32)
    l_ref[...] = jnp.zeros(l_ref.shape, F32)
    acc_ref[...] = jnp.zeros(acc_ref.shape, F32)

    def attend(c, near):
        mbias = sc_ref[c]
        m_all, l_all = m_ref[...], l_ref[...]
        m_rows, l_rows = [], []

        def qk(h):
            cols = slice(h * ATTN_HEAD_DIM, (h + 1) * ATTN_HEAD_DIM)
            return jnp.dot(k_ref[rows(c), cols], qt_ref[cols, :], preferred_element_type=F32)

        ahead = N_ATTN_HEADS
        s_raw = [qk(h) for h in range(ahead)]
        for h in range(N_ATTN_HEADS):
            cols = slice(h * ATTN_HEAD_DIM, (h + 1) * ATTN_HEAD_DIM)
            s = s_raw[h] * scale + mbias
            if h + ahead < N_ATTN_HEADS:
                s_raw.append(qk(h + ahead))
            if near is not None:
                s = s + bias_ref[h, near * bq:(near + 1) * bq, :]
            m_old = m_all[h:h + 1, :]
            m_new = jnp.maximum(m_old, jnp.max(_reduce_row_groups(s, jnp.maximum), axis=0, keepdims=True))
            p = jnp.exp(s - m_new)
            a = jnp.exp(m_old - m_new)
            m_rows.append(m_new)
            l_rows.append(a * l_all[h:h + 1, :]
                          + jnp.sum(_reduce_row_groups(p, jnp.add), axis=0, keepdims=True))
            pv = lax.dot_general(v_ref[rows(c), cols], p.astype(BF16), TN_DIMS, preferred_element_type=F32)
            acc_ref[h] = a * acc_ref[h] + pv
        m_ref[...] = jnp.concatenate(m_rows, axis=0)
        l_ref[...] = jnp.concatenate(l_rows, axis=0)

    def far_chunk(c, carry):
        attend(c, None)
        return carry

    lax.fori_loop(0, jnp.maximum(i - 1, 0), far_chunk, 0)

    @pl.when(i > 0)
    def _():
        attend(i - 1, 0)

    attend(i, 1)
    for h in range(N_ATTN_HEADS):
        out_t = acc_ref[h] / l_ref[h:h + 1, :]
        o_ref[:, h * ATTN_HEAD_DIM:(h + 1) * ATTN_HEAD_DIM] = out_t.T.astype(o_ref.dtype)


def _rel_bucket(dist):
    n = jnp.maximum(dist, 0)
    max_exact = N_REL_BUCKETS // 2
    nf = jnp.maximum(n, 1).astype(F32)
    large = max_exact + (jnp.log(nf / max_exact) / math.log(REL_MAX_DISTANCE / max_exact)
                         * (N_REL_BUCKETS - max_exact)).astype(I32)
    large = jnp.minimum(large, N_REL_BUCKETS - 1)
    return jnp.where(n < max_exact, n, large)


def _near_bias_tiles(rel_bias, bq):
    assert bq >= REL_MAX_DISTANCE
    key = jnp.arange(2 * bq, dtype=I32)[:, None]
    qry = jnp.arange(bq, dtype=I32)[None, :]
    bucket = _rel_bucket(qry + bq - key)
    rel = (rel_bias - rel_bias[N_REL_BUCKETS - 1]).T
    tiles = jnp.broadcast_to(rel[:, 0][:, None, None], (rel.shape[0],) + bucket.shape)
    for b in range(1, N_REL_BUCKETS):
        tiles = jnp.where(bucket[None] >= b, rel[:, b][:, None, None], tiles)
    return tiles


def _dsa_attention(qt, kv, ki_b, wt, rel_bias, bq):
    S = kv.shape[0]
    top_k = min(TOPK_MAX, S // 4)
    n_blk = S // bq
    bias = _near_bias_tiles(rel_bias, bq)
    kernel = functools.partial(_dsa_kernel, bq=bq, top_k=top_k)
    return pl.pallas_call(
        kernel,
        grid=(n_blk,),
        in_specs=[pl.BlockSpec((N_IDX_HEADS * IDX_HEAD_DIM, bq), lambda i: (1, i)),
                  pl.BlockSpec((N_IDX_HEADS, bq), lambda i: (0, i)),
                  pl.BlockSpec((ATTN_WIDTH, bq), lambda i: (0, i)),
                  _resident((S, IDX_HEAD_DIM), lambda i: (0, 0)),
                  _resident((S, ATTN_WIDTH), lambda i: (0, 0)),
                  _resident((S, ATTN_WIDTH), lambda i: (0, 1)),
                  _resident((N_ATTN_HEADS, 2 * bq, bq), lambda i: (0, 0, 0))],
        out_specs=pl.BlockSpec((bq, ATTN_WIDTH), lambda i: (i, 0)),
        out_shape=jax.ShapeDtypeStruct((S, ATTN_WIDTH), BF16),
        scratch_shapes=[pltpu.VMEM((n_blk, bq, bq), F32),
                        pltpu.VMEM((N_ATTN_HEADS, bq), F32),
                        pltpu.VMEM((N_ATTN_HEADS, bq), F32),
                        pltpu.VMEM((N_ATTN_HEADS, ATTN_HEAD_DIM, bq), F32)],
        compiler_params=_params("arbitrary"),
        name="dsa_attention",
    )(qt, wt, qt, ki_b, kv, kv, bias)


def _sgu_kernel(gu_ref, gv_ref, g_ref, b_ref, ws_ref, bs_ref, o_ref, *, n_chunk):
    u = _gelu(gu_ref[...])
    v = _layer_norm(_gelu(gv_ref[...]), g_ref[...], b_ref[...]).astype(BF16)
    r = lax.broadcasted_iota(I32, (GMLP_CHUNK, GMLP_CHUNK), 0)
    c = lax.broadcasted_iota(I32, (GMLP_CHUNK, GMLP_CHUNK), 1)
    for g in range(GMLP_GROUPS):
        wg = jnp.where(c <= r, ws_ref[g], 0.0).astype(BF16)
        bcol = bs_ref[:, g:g + 1]
        cols = slice(g * LANES, (g + 1) * LANES)
        for n in range(n_chunk):
            rws = slice(n * GMLP_CHUNK, (n + 1) * GMLP_CHUNK)
            mixed = jnp.dot(wg, v[rws, cols], preferred_element_type=F32) + bcol
            o_ref[rws, cols] = (u[rws, cols] * mixed).astype(o_ref.dtype)


def _sgu(zg, ln_g, ln_b, w_s, b_s, tm):
    S = zg.shape[0]
    kernel = functools.partial(_sgu_kernel, n_chunk=tm // GMLP_CHUNK)
    return pl.pallas_call(
        kernel,
        grid=(S // tm,),
        in_specs=[pl.BlockSpec((tm, GMLP_WIDTH), lambda i: (i, 0)),
                  pl.BlockSpec((tm, GMLP_WIDTH), lambda i: (i, 1)),
                  pl.BlockSpec((1, GMLP_WIDTH), lambda i: (0, 0)),
                  pl.BlockSpec((1, GMLP_WIDTH), lambda i: (0, 0)),
                  pl.BlockSpec((GMLP_GROUPS, GMLP_CHUNK, GMLP_CHUNK), lambda i: (0, 0, 0)),
                  pl.BlockSpec((GMLP_CHUNK, GMLP_GROUPS), lambda i: (0, 0))],
        out_specs=pl.BlockSpec((tm, GMLP_WIDTH), lambda i: (i, 0)),
        out_shape=jax.ShapeDtypeStruct((S, GMLP_WIDTH), BF16),
        compiler_params=_params("parallel"),
        name="gmlp_sgu",
    )(zg, zg, ln_g.reshape(1, -1), ln_b.reshape(1, -1), w_s, b_s.T)


def _merge_kernel(at_ref, gm_ref, ga_ref, gg_ref, h_ref, wa_ref, wg_ref, wo_ref, g_ref, b_ref,
                  h1_ref, h1b_ref, h1bt_ref):
    a1 = jnp.dot(at_ref[...], wa_ref[...], preferred_element_type=F32)
    a2 = jnp.dot(gm_ref[...], wg_ref[...], preferred_element_type=F32)
    merged = jax.nn.sigmoid(ga_ref[...]) * a1 + jax.nn.sigmoid(gg_ref[...]) * a2
    y = DEEPNORM_ALPHA * h_ref[...] + jnp.dot(merged.astype(BF16), wo_ref[...], preferred_element_type=F32)
    h1 = _layer_norm(y, g_ref[...], b_ref[...])
    h1_ref[...] = h1
    h1b_ref[...] = h1.astype(BF16)
    h1bt_ref[...] = h1.T.astype(BF16)


def _merge(attn, gm, zg, h, wa, wg, wo, ln_g, ln_b, tm):
    S, D = h.shape
    W = attn.shape[1]
    return pl.pallas_call(
        _merge_kernel,
        grid=(S // tm,),
        in_specs=[pl.BlockSpec((tm, W), lambda i: (i, 0)),
                  pl.BlockSpec((tm, W), lambda i: (i, 0)),
                  pl.BlockSpec((tm, D), lambda i: (i, 1)),
                  pl.BlockSpec((tm, D), lambda i: (i, 2)),
                  pl.BlockSpec((tm, D), lambda i: (i, 0)),
                  _resident((W, D), lambda i: (0, 0)),
                  _resident((W, D), lambda i: (0, 0)),
                  _resident((D, D), lambda i: (0, 0)),
                  pl.BlockSpec((1, D), lambda i: (0, 0)),
                  pl.BlockSpec((1, D), lambda i: (0, 0))],
        out_specs=[pl.BlockSpec((tm, D), lambda i: (i, 0)),
                   pl.BlockSpec((tm, D), lambda i: (i, 0)),
                   pl.BlockSpec((D, tm), lambda i: (0, i))],
        out_shape=[jax.ShapeDtypeStruct((S, D), F32), jax.ShapeDtypeStruct((S, D), BF16),
                   jax.ShapeDtypeStruct((D, S), BF16)],
        compiler_params=_params("parallel"),
        name="merge_ln1",
    )(attn, gm, zg, zg, h, wa, wg, wo, ln_g.reshape(1, D), ln_b.reshape(1, D))


def _peer_pairs():
    pairs = [(a, b) for a in range(PEER_TOPK) for b in range(PEER_TOPK) if (a + 1) * (b + 1) <= PEER_TOPK]
    return sorted(pairs, key=lambda ab: ab[0] * PEER_TOPK + ab[1])


def _extract_top(work, n_rounds, on_pick, exact):
    n = float(work.shape[0])
    iota = lax.broadcasted_iota(I32, work.shape, 0).astype(F32)
    for r in range(n_rounds):
        mx = jnp.max(work, axis=0, keepdims=True)
        pick = work == mx
        if exact:
            first = jnp.min(jnp.where(pick, iota, n), axis=0, keepdims=True)
            pick = iota == first
        on_pick(r, mx, pick)
        work = jnp.where(pick, -jnp.inf, work)


def _route_kernel(hb_ref, wq_ref, sk_ref, u_ref, v_ref, l1_ref, a_ref, r2_ref, b_ref, ub_ref, vb_ref, *, T):
    ub_ref[...] = u_ref[...].astype(BF16)
    vb_ref[...] = v_ref[...].astype(BF16)
    qt = jnp.dot(hb_ref[...], wq_ref[...], preferred_element_type=F32).T
    pairs = _peer_pairs()

    def route(h, exact):
        miscount = jnp.zeros((1, T), F32)
        s, rank, vals = [], [], []
        for p in range(2):
            idx = 2 * h + p
            sp = jnp.dot(sk_ref[idx], qt[idx * LANES:(idx + 1) * LANES, :].astype(BF16),
                         preferred_element_type=F32)
            state = {"rank": jnp.full(sp.shape, float(PEER_TOPK), F32), "vals": []}

            def on_pick(r, mx, pick, state=state):
                state["rank"] = jnp.where(pick, float(r), state["rank"])
                state["vals"].append(mx)

            _extract_top(sp, PEER_TOPK, on_pick, exact)
            n_ranked = jnp.sum(jnp.where(state["rank"] < float(PEER_TOPK), 1.0, 0.0), axis=0, keepdims=True)
            miscount = jnp.maximum(miscount, jnp.abs(n_ranked - float(PEER_TOPK)))
            s.append(sp)
            rank.append(state["rank"])
            vals.append(state["vals"])
        cand = jnp.concatenate([vals[0][a] + vals[1][b] for a, b in pairs], axis=0)
        picked = {"mask": jnp.zeros(cand.shape, jnp.bool_)}

        def on_pick_c(r, mx, pick, picked=picked):
            picked["mask"] = picked["mask"] | pick

        _extract_top(cand, PEER_TOPK, on_pick_c, True)
        top = vals[0][0] + vals[1][0]
        selw = jnp.where(picked["mask"], jnp.exp(cand - top), 0.0)
        z = jnp.sum(selw, axis=0, keepdims=True)
        self32 = jnp.where(picked["mask"], 1.0, 0.0)
        l1 = jnp.zeros(s[0].shape, F32)
        pos = 0
        for a in range(PEER_TOPK):
            cnt = sum(1 for ab in pairs if ab[0] == a)
            n_a = jnp.sum(self32[pos:pos + cnt, :], axis=0, keepdims=True)
            pos += cnt
            l1 = jnp.where(rank[0] == float(a), n_a, l1)
        l1_ref[h] = l1
        a_ref[h] = jnp.exp(s[0] - vals[0][0])
        r2_ref[h] = rank[1].astype(r2_ref.dtype)
        b_ref[h] = (jnp.exp(s[1] - vals[1][0]) / z).astype(b_ref.dtype)
        return miscount

    miscount = functools.reduce(jnp.maximum, [route(h, exact=False) for h in range(PEER_HEADS)])

    @pl.when(jnp.max(miscount) > 0.0)
    def _():
        for h in range(PEER_HEADS):
            route(h, exact=True)


def _peer_route(h1b, wq, sk, expert_u, expert_v, T):
    S, D = h1b.shape
    n_steps = S // T
    E = expert_u.shape[0]
    kernel = functools.partial(_route_kernel, T=T)
    shp = jax.ShapeDtypeStruct((PEER_HEADS, PEER_N_KEYS, S), F32)
    shp_b = jax.ShapeDtypeStruct((PEER_HEADS, PEER_N_KEYS, S), BF16)
    spec = pl.BlockSpec((PEER_HEADS, PEER_N_KEYS, T), lambda t: (0, 0, t))
    table = pl.BlockSpec((E // n_steps, D), lambda t: (t, 0))
    return pl.pallas_call(
        kernel,
        grid=(n_steps,),
        in_specs=[pl.BlockSpec((T, D), lambda t: (t, 0)),
                  _resident(wq.shape, lambda t: (0, 0)),
                  _resident(sk.shape, lambda t: (0, 0, 0)),
                  table, table],
        out_specs=[spec, spec, spec, spec, table, table],
        out_shape=[shp, shp, shp_b, shp_b,
                   jax.ShapeDtypeStruct(expert_u.shape, BF16), jax.ShapeDtypeStruct(expert_v.shape, BF16)],
        compiler_params=_params("parallel"),
        name="peer_route",
    )(h1b, wq, sk, expert_u, expert_v)


def _peer_kernel(ht_ref, u_ref, v_ref, l1_ref, a_ref, r2_ref, b_ref, o_ref, act_ref, *, n_i1, ce):
    @pl.when(pl.program_id(1) == 0)
    def _():
        o_ref[...] = jnp.zeros_like(o_ref)

    T = ht_ref.shape[1]
    n_slab = PEER_N_KEYS // PACKED_ROWS
    i1_per_chunk = ce // PEER_N_KEYS
    r2 = [r2_ref[h] for h in range(PEER_HEADS)]
    b = [b_ref[h] for h in range(PEER_HEADS)]

    def gates(c):
        tiles = []
        for j in range(c * i1_per_chunk, (c + 1) * i1_per_chunk):
            gate = [None] * n_slab
            for h in range(PEER_HEADS):
                l1 = jnp.broadcast_to(l1_ref[h, j:j + 1, :], (PACKED_ROWS, T)).astype(BF16)
                a = jnp.broadcast_to(a_ref[h, j:j + 1, :], (PACKED_ROWS, T)).astype(BF16)
                for s in range(n_slab):
                    slab = slice(s * PACKED_ROWS, (s + 1) * PACKED_ROWS)
                    val = jnp.where(r2[h][slab] < l1, b[h][slab], 0.0) * a
                    gate[s] = val if gate[s] is None else gate[s] + val
            tiles.extend(gate)
        return tiles

    n_chunk = n_i1 * PEER_N_KEYS // ce
    zero = jnp.minimum(pl.program_id(1), 0)
    for c in range(n_chunk):
        act_ref[c + zero] = jnp.dot(u_ref[c * ce:(c + 1) * ce, :], ht_ref[...],
                                    preferred_element_type=F32)
    tiles = []
    for c in range(n_chunk):
        gel = _gelu(act_ref[c + zero].astype(BF16))
        g = gates(c)
        tiles.extend(gel[k * PACKED_ROWS:(k + 1) * PACKED_ROWS] * g[k] for k in range(len(g)))
    pt = jnp.concatenate(tiles, axis=0)
    o_ref[...] += lax.dot_general(pt, v_ref[...], TN_DIMS, preferred_element_type=F32)


def _peer_experts(h1bt, u_b, v_b, l1, a, r2, b, T, eb):
    D, S = h1bt.shape
    E = u_b.shape[0]
    n_i1 = eb // PEER_N_KEYS
    ce = MXU_DIM // 2
    kernel = functools.partial(_peer_kernel, n_i1=n_i1, ce=ce)
    row_spec = pl.BlockSpec((PEER_HEADS, n_i1, T), lambda t, e: (0, e, t))
    full_spec = pl.BlockSpec((PEER_HEADS, PEER_N_KEYS, T), lambda t, e: (0, 0, t))
    return pl.pallas_call(
        kernel,
        grid=(S // T, E // eb),
        in_specs=[pl.BlockSpec((D, T), lambda t, e: (0, t)),
                  pl.BlockSpec((eb, D), lambda t, e: (e, 0)),
                  pl.BlockSpec((eb, D), lambda t, e: (e, 0)),
                  row_spec, row_spec, full_spec, full_spec],
        out_specs=pl.BlockSpec((T, D), lambda t, e: (t, 0)),
        out_shape=jax.ShapeDtypeStruct((S, D), F32),
        scratch_shapes=[pltpu.VMEM((eb // ce, ce, T), F32)],
        compiler_params=_params("parallel", "arbitrary"),
        name="peer_experts",
    )(h1bt, u_b, v_b, l1, a, r2, b)


def _ple_kernel(h1_ref, peer_ref, p_ref, wg_ref, wp_ref, g_ref, b_ref, o_ref):
    r = DEEPNORM_ALPHA * h1_ref[...] + peer_ref[...]
    gate = jax.nn.sigmoid(jnp.dot(r.astype(BF16), wg_ref[...], preferred_element_type=F32))
    proj = jnp.dot(p_ref[...], wp_ref[...], preferred_element_type=F32)
    o_ref[...] = _layer_norm(r + gate * proj, g_ref[...], b_ref[...])


def _ple(h1, peer, p_b, wg, wp, ln_g, ln_b, tm):
    S, D = h1.shape
    P = p_b.shape[1]
    return pl.pallas_call(
        _ple_kernel,
        grid=(S // tm,),
        in_specs=[pl.BlockSpec((tm, D), lambda i: (i, 0)),
                  pl.BlockSpec((tm, D), lambda i: (i, 0)),
                  pl.BlockSpec((tm, P), lambda i: (i, 0)),
                  _resident((D, D), lambda i: (0, 0)),
                  _resident((P, D), lambda i: (0, 0)),
                  pl.BlockSpec((1, D), lambda i: (0, 0)),
                  pl.BlockSpec((1, D), lambda i: (0, 0))],
        out_specs=pl.BlockSpec((tm, D), lambda i: (i, 0)),
        out_shape=jax.ShapeDtypeStruct((S, D), F32),
        compiler_params=_params("parallel"),
        name="ple_ln2",
    )(h1, peer, p_b, wg, wp, ln_g.reshape(1, D), ln_b.reshape(1, D))


def _layer(x, p, ln_emb_g, ln_emb_b, rel_bias, w_in, gmlp_ln_g, gmlp_ln_b, gmlp_w_s, gmlp_b_s,
           w_br_attn, w_br_gmlp, w_mix_out, ln1_g, ln1_b, peer_w_q, peer_sub_keys, peer_u, peer_v,
           ple_w_proj, ple_w_gate, ln2_g, ln2_b):
    S, D = x.shape
    tm = min(512, S)
    o_k, o_v, o_qi = ATTN_WIDTH, 2 * ATTN_WIDTH, 3 * ATTN_WIDTH
    o_ki = o_qi + N_IDX_HEADS * IDX_HEAD_DIM
    o_gu = o_ki + IDX_HEAD_DIM + N_IDX_HEADS

    h, hb = _ln_embed(x, ln_emb_g, ln_emb_b, tm)
    wqt = jnp.concatenate([w_in[:, :o_k], w_in[:, o_qi:o_ki]], axis=1).T.astype(BF16)
    w1 = w_in[:, o_k:o_qi].astype(BF16)
    w2 = jnp.pad(w_in[:, o_ki:o_gu], ((0, 0), (0, LANES - (o_gu - o_ki)))).astype(BF16)
    w3 = w_in[:, o_gu:].astype(BF16)
    qt = _matmul_t(wqt, hb, BF16, tm, "in_proj_q")
    kv = _matmul(hb, w1, BF16, min(1024, S), 512, "in_proj_kv")
    kw = _matmul(hb, w2, F32, min(1024, S), LANES, "in_proj_index")
    zg = _matmul(hb, w3, F32, min(1024, S), 512, "in_proj_gmlp")

    attn = _dsa_attention(qt, kv, kw[:, :IDX_HEAD_DIM].astype(BF16),
                          kw[:, IDX_HEAD_DIM:IDX_HEAD_DIM + N_IDX_HEADS].T, rel_bias, bq=256)
    gm = _sgu(zg, gmlp_ln_g, gmlp_ln_b, gmlp_w_s, gmlp_b_s, tm)
    h1, h1b, h1bt = _merge(attn, gm, zg, h, w_br_attn.astype(BF16), w_br_gmlp.astype(BF16),
                           w_mix_out.astype(BF16), ln1_g, ln1_b, min(256, S))

    sk = peer_sub_keys.reshape(PEER_HEADS * 2, PEER_N_KEYS, -1).astype(BF16)
    l1, a, r2, b, u_b, v_b = _peer_route(h1b, peer_w_q.astype(BF16), sk, peer_u, peer_v, min(256, S))
    peer = _peer_experts(h1bt, u_b, v_b, l1, a, r2, b, T=min(512, S), eb=1024)
    return _ple(h1, peer, p.astype(BF16), ple_w_gate.astype(BF16), ple_w_proj.astype(BF16),
                ln2_g, ln2_b, min(512, S))


def kernel(x, p, positions, ln_emb_g, ln_emb_b, rel_bias, w_in, gmlp_ln_g, gmlp_ln_b, gmlp_w_s, gmlp_b_s, w_br_attn, w_br_gmlp, w_mix_out, ln1_g, ln1_b, peer_w_q, peer_sub_keys, peer_u, peer_v, ple_w_proj, ple_w_gate, ln2_g, ln2_b):
    del positions
    assert x.shape[0] == 1 and w_in.shape[0] == DEPTH
    out = _layer(x[0], p[0, 0], ln_emb_g, ln_emb_b, rel_bias, w_in[0], gmlp_ln_g[0], gmlp_ln_b[0],
                 gmlp_w_s[0], gmlp_b_s[0], w_br_attn[0], w_br_gmlp[0], w_mix_out[0], ln1_g[0],
                 ln1_b[0], peer_w_q[0], peer_sub_keys[0], peer_u[0], peer_v[0], ple_w_proj[0],
                 ple_w_gate[0], ln2_g[0], ln2_b[0])
    return out[None]
```

```python
import functools
import math

import jax
import jax.numpy as jnp
from jax import lax
from jax.experimental import pallas as pl
from jax.experimental.pallas import tpu as pltpu

F32 = jnp.float32
BF16 = jnp.bfloat16
I32 = jnp.int32

LN_EPS = 1e-5
DEPTH = 1
DEEPNORM_ALPHA = (2.0 * DEPTH) ** 0.25

N_ATTN_HEADS = 8
ATTN_HEAD_DIM = 128
ATTN_WIDTH = N_ATTN_HEADS * ATTN_HEAD_DIM
N_IDX_HEADS = 16
IDX_HEAD_DIM = 64
TOPK_MAX = 256
GMLP_CHUNK = 128
GMLP_GROUPS = 8
GMLP_WIDTH = 1024
N_REL_BUCKETS = 32
REL_MAX_DISTANCE = 128
PEER_HEADS = 8
PEER_N_KEYS = 128
PEER_TOPK = 16

LANES = 128
SUBLANES = 8
PACKED_ROWS = 2 * SUBLANES
MXU_DIM = 256
NEG = -1e30
SEARCH_STEPS_MAX = 64
NT_DIMS = (((1,), (1,)), ((), ()))
TN_DIMS = (((0,), (0,)), ((), ()))
VMEM_LIMIT = 56 * 1024 * 1024


def _params(*sem):
    return pltpu.CompilerParams(dimension_semantics=sem, vmem_limit_bytes=VMEM_LIMIT)


def _resident(shape, index_map):
    return pl.BlockSpec(shape, index_map, pipeline_mode=pl.Buffered(1))


def _gelu(x):
    return 0.5 * x * (1.0 + jnp.tanh(0.7978845608028654 * (x + 0.044715 * (x * x * x))))


def _layer_norm(x, g, b):
    mu = jnp.mean(x, axis=-1, keepdims=True)
    xc = x - mu
    var = jnp.mean(xc * xc, axis=-1, keepdims=True)
    return xc * lax.rsqrt(var + LN_EPS) * g + b


def _ln_kernel(x_ref, g_ref, b_ref, h_ref, hb_ref):
    y = _layer_norm(x_ref[...], g_ref[...], b_ref[...])
    h_ref[...] = y
    hb_ref[...] = y.astype(BF16)


def _ln_embed(x, g, b, tm):
    S, D = x.shape
    return pl.pallas_call(
        _ln_kernel,
        grid=(S // tm,),
        in_specs=[pl.BlockSpec((tm, D), lambda i: (i, 0)),
                  pl.BlockSpec((1, D), lambda i: (0, 0)),
                  pl.BlockSpec((1, D), lambda i: (0, 0))],
        out_specs=[pl.BlockSpec((tm, D), lambda i: (i, 0)),
                   pl.BlockSpec((tm, D), lambda i: (i, 0))],
        out_shape=[jax.ShapeDtypeStruct((S, D), F32), jax.ShapeDtypeStruct((S, D), BF16)],
        compiler_params=_params("parallel"),
        name="ln_embed",
    )(x, g.reshape(1, D), b.reshape(1, D))


def _mm_kernel(a_ref, w_ref, o_ref):
    o_ref[...] = jnp.dot(a_ref[...], w_ref[...], preferred_element_type=F32).astype(o_ref.dtype)


def _matmul(a, w, out_dtype, tm, tn, name):
    M, K = a.shape
    N = w.shape[1]
    return pl.pallas_call(
        _mm_kernel,
        grid=(M // tm, N // tn),
        in_specs=[pl.BlockSpec((tm, K), lambda i, j: (i, 0)),
                  pl.BlockSpec((K, tn), lambda i, j: (0, j))],
        out_specs=pl.BlockSpec((tm, tn), lambda i, j: (i, j)),
        out_shape=jax.ShapeDtypeStruct((M, N), out_dtype),
        compiler_params=_params("parallel", "arbitrary"),
        name=name,
    )(a, w)


def _mm_t_kernel(wt_ref, a_ref, o_ref):
    o_ref[...] = lax.dot_general(wt_ref[...], a_ref[...], NT_DIMS,
                                 preferred_element_type=F32).astype(o_ref.dtype)


def _matmul_t(wt, a, out_dtype, tm, name):
    M, K = a.shape
    N = wt.shape[0]
    return pl.pallas_call(
        _mm_t_kernel,
        grid=(M // tm,),
        in_specs=[_resident((N, K), lambda i: (0, 0)),
                  pl.BlockSpec((tm, K), lambda i: (i, 0))],
        out_specs=pl.BlockSpec((N, tm), lambda i: (0, i)),
        out_shape=jax.ShapeDtypeStruct((N, M), out_dtype),
        compiler_params=_params("parallel"),
        name=name,
    )(wt, a)


def _reduce_row_groups(t, op):
    parts = [t[j * SUBLANES:(j + 1) * SUBLANES, :] for j in range(t.shape[0] // SUBLANES)]
    while len(parts) > 1:
        parts = [op(parts[j], parts[j + 1]) for j in range(0, len(parts) - 1, 2)] + (
            [parts[-1]] if len(parts) % 2 else [])
    return parts[0]


def _dsa_kernel(qit_ref, wt_ref, qt_ref, ki_ref, k_ref, v_ref, bias_ref, o_ref,
                sc_ref, m_ref, l_ref, acc_ref, *, bq, top_k):
    i = pl.program_id(0)
    w = wt_ref[...] * ((IDX_HEAD_DIM ** -0.5) * (N_IDX_HEADS ** -0.5))
    key_i = lax.broadcasted_iota(I32, (bq, bq), 0)
    qry_i = lax.broadcasted_iota(I32, (bq, bq), 1)
    causal = key_i <= qry_i

    def rows(c):
        return pl.ds(pl.multiple_of(c * bq, bq), bq)

    def scores(c):
        kic = ki_ref[rows(c), :]
        acc = jnp.zeros((bq, bq), F32)
        for h in range(N_IDX_HEADS):
            d = jnp.dot(kic, qit_ref[h * IDX_HEAD_DIM:(h + 1) * IDX_HEAD_DIM, :],
                        preferred_element_type=F32)
            acc = acc + w[h:h + 1, :] * jnp.maximum(d, 0.0)
        return acc

    def score_chunk(c, carry):
        sc_ref[c] = scores(c)
        return carry

    lax.fori_loop(0, i, score_chunk, 0)
    sc_ref[i] = jnp.where(causal, scores(i), -jnp.inf)

    def lowest_highest(c, lh):
        x = sc_ref[c]
        return (jnp.minimum(lh[0], _reduce_row_groups(x, jnp.minimum)),
                jnp.maximum(lh[1], _reduce_row_groups(x, jnp.maximum)))

    lo8, hi8 = lax.fori_loop(0, i, lowest_highest, (jnp.full((SUBLANES, bq), jnp.inf, F32),
                                                    jnp.full((SUBLANES, bq), -jnp.inf, F32)))
    diag = sc_ref[i]
    lo8 = jnp.minimum(lo8, _reduce_row_groups(jnp.where(causal, diag, jnp.inf), jnp.minimum))
    hi8 = jnp.maximum(hi8, _reduce_row_groups(diag, jnp.maximum))
    lo0 = jnp.min(lo8, axis=0, keepdims=True)
    hi0 = jnp.max(hi8, axis=0, keepdims=True)

    def count_ge(cand):
        def body(c, cnt):
            return cnt + _reduce_row_groups(jnp.where(sc_ref[c] >= cand, 1.0, 0.0), jnp.add)
        cnt = lax.fori_loop(0, i + 1, body, jnp.zeros((SUBLANES, bq), F32))
        return jnp.sum(cnt, axis=0, keepdims=True)

    n_keys = (i * bq + 1 + lax.broadcasted_iota(I32, (1, bq), 1)).astype(F32)
    few_keys = n_keys < float(top_k)

    def bisect(state):
        it, lo, hi, settled = state
        mid = lo + (hi - lo) * 0.5
        cnt = count_ge(mid)
        enough = cnt >= float(top_k)
        open_ = settled == 0.0
        lo_new = jnp.where(open_ & enough, mid, lo)
        hi_new = jnp.where(open_ & jnp.logical_not(enough), mid, hi)
        done = (cnt == float(top_k)) | (mid <= lo) | (mid >= hi)
        return it + 1, lo_new, hi_new, jnp.where(done, 1.0, settled)

    def unsettled(state):
        it, _, _, settled = state
        return (it < SEARCH_STEPS_MAX) & (jnp.min(settled) == 0.0)

    _, lo, _, _ = lax.while_loop(
        unsettled, bisect, (jnp.int32(0), lo0, hi0, jnp.where(few_keys, 1.0, 0.0)))
    thresh = jnp.where(few_keys, -jnp.inf, lo)

    def mask_chunk(c, carry):
        sc_ref[c] = jnp.where(sc_ref[c] >= thresh, 0.0, NEG)
        return carry

    lax.fori_loop(0, i, mask_chunk, 0)
    sc_ref[i] = jnp.where(causal, jnp.where(sc_ref[i] >= thresh, 0.0, NEG), NEG)

    scale = ATTN_HEAD_DIM ** -0.5
    m_ref[...] = jnp.full(m_ref.shape, NEG, F32)
    l_ref[...] = jnp.zeros(l_ref.shape, F32)
    acc_ref[...] = jnp.zeros(acc_ref.shape, F32)

    def attend(c, near):
        mbias = sc_ref[c]
        m_all, l_all = m_ref[...], l_ref[...]
        m_rows, l_rows = [], []

        def qk(h):
            cols = slice(h * ATTN_HEAD_DIM, (h + 1) * ATTN_HEAD_DIM)
            return jnp.dot(k_ref[rows(c), cols], qt_ref[cols, :], preferred_element_type=F32)

        s_raw = [qk(h) for h in range(N_ATTN_HEADS)]
        for h in range(N_ATTN_HEADS):
            cols = slice(h * ATTN_HEAD_DIM, (h + 1) * ATTN_HEAD_DIM)
            s = s_raw[h] * scale + mbias
            if near is not None:
                s = s + bias_ref[h, near * bq:(near + 1) * bq, :]
            m_old = m_all[h:h + 1, :]
            m_new = jnp.maximum(m_old, jnp.max(_reduce_row_groups(s, jnp.maximum), axis=0, keepdims=True))
            p = jnp.exp(s - m_new)
            a = jnp.exp(m_old - m_new)
            m_rows.append(m_new)
            l_rows.append(a * l_all[h:h + 1, :]
                          + jnp.sum(_reduce_row_groups(p, jnp.add), axis=0, keepdims=True))
            pv = lax.dot_general(v_ref[rows(c), cols], p.astype(BF16), TN_DIMS, preferred_element_type=F32)
            acc_ref[h] = a * acc_ref[h] + pv
        m_ref[...] = jnp.concatenate(m_rows, axis=0)
        l_ref[...] = jnp.concatenate(l_rows, axis=0)

    def far_chunk(c, carry):
        attend(c, None)
        return carry

    lax.fori_loop(0, jnp.maximum(i - 1, 0), far_chunk, 0)

    @pl.when(i > 0)
    def _():
        attend(i - 1, 0)

    attend(i, 1)
    for h in range(N_ATTN_HEADS):
        out_t = acc_ref[h] / l_ref[h:h + 1, :]
        o_ref[:, h * ATTN_HEAD_DIM:(h + 1) * ATTN_HEAD_DIM] = out_t.T.astype(o_ref.dtype)


def _rel_bucket(dist):
    n = jnp.maximum(dist, 0)
    max_exact = N_REL_BUCKETS // 2
    nf = jnp.maximum(n, 1).astype(F32)
    large = max_exact + (jnp.log(nf / max_exact) / math.log(REL_MAX_DISTANCE / max_exact)
                         * (N_REL_BUCKETS - max_exact)).astype(I32)
    large = jnp.minimum(large, N_REL_BUCKETS - 1)
    return jnp.where(n < max_exact, n, large)


def _near_bias_tiles(rel_bias, bq):
    assert bq >= REL_MAX_DISTANCE
    key = jnp.arange(2 * bq, dtype=I32)[:, None]
    qry = jnp.arange(bq, dtype=I32)[None, :]
    bucket = _rel_bucket(qry + bq - key)
    rel = (rel_bias - rel_bias[N_REL_BUCKETS - 1]).T
    tiles = jnp.broadcast_to(rel[:, 0][:, None, None], (rel.shape[0],) + bucket.shape)
    for b in range(1, N_REL_BUCKETS):
        tiles = jnp.where(bucket[None] >= b, rel[:, b][:, None, None], tiles)
    return tiles


def _dsa_attention(qt, kv, ki_b, wt, rel_bias, bq):
    S = kv.shape[0]
    top_k = min(TOPK_MAX, S // 4)
    n_blk = S // bq
    bias = _near_bias_tiles(rel_bias, bq)
    kernel = functools.partial(_dsa_kernel, bq=bq, top_k=top_k)
    return pl.pallas_call(
        kernel,
        grid=(n_blk,),
        in_specs=[pl.BlockSpec((N_IDX_HEADS * IDX_HEAD_DIM, bq), lambda i: (1, i)),
                  pl.BlockSpec((N_IDX_HEADS, bq), lambda i: (0, i)),
                  pl.BlockSpec((ATTN_WIDTH, bq), lambda i: (0, i)),
                  _resident((S, IDX_HEAD_DIM), lambda i: (0, 0)),
                  _resident((S, ATTN_WIDTH), lambda i: (0, 0)),
                  _resident((S, ATTN_WIDTH), lambda i: (0, 1)),
                  _resident((N_ATTN_HEADS, 2 * bq, bq), lambda i: (0, 0, 0))],
        out_specs=pl.BlockSpec((bq, ATTN_WIDTH), lambda i: (i, 0)),
        out_shape=jax.ShapeDtypeStruct((S, ATTN_WIDTH), BF16),
        scratch_shapes=[pltpu.VMEM((n_blk, bq, bq), F32),
                        pltpu.VMEM((N_ATTN_HEADS, bq), F32),
                        pltpu.VMEM((N_ATTN_HEADS, bq), F32),
                        pltpu.VMEM((N_ATTN_HEADS, ATTN_HEAD_DIM, bq), F32)],
        compiler_params=_params("arbitrary"),
        name="dsa_attention",
    )(qt, wt, qt, ki_b, kv, kv, bias)


def _sgu_kernel(gu_ref, gv_ref, g_ref, b_ref, ws_ref, bs_ref, o_ref, *, n_chunk):
    u = _gelu(gu_ref[...])
    v = _layer_norm(_gelu(gv_ref[...]), g_ref[...], b_ref[...]).astype(BF16)
    r = lax.broadcasted_iota(I32, (GMLP_CHUNK, GMLP_CHUNK), 0)
    c = lax.broadcasted_iota(I32, (GMLP_CHUNK, GMLP_CHUNK), 1)
    for g in range(GMLP_GROUPS):
        wg = jnp.where(c <= r, ws_ref[g], 0.0).astype(BF16)
        bcol = bs_ref[:, g:g + 1]
        cols = slice(g * LANES, (g + 1) * LANES)
        for n in range(n_chunk):
            rws = slice(n * GMLP_CHUNK, (n + 1) * GMLP_CHUNK)
            mixed = jnp.dot(wg, v[rws, cols], preferred_element_type=F32) + bcol
            o_ref[rws, cols] = (u[rws, cols] * mixed).astype(o_ref.dtype)


def _sgu(zg, ln_g, ln_b, w_s, b_s, tm):
    S = zg.shape[0]
    kernel = functools.partial(_sgu_kernel, n_chunk=tm // GMLP_CHUNK)
    return pl.pallas_call(
        kernel,
        grid=(S // tm,),
        in_specs=[pl.BlockSpec((tm, GMLP_WIDTH), lambda i: (i, 0)),
                  pl.BlockSpec((tm, GMLP_WIDTH), lambda i: (i, 1)),
                  pl.BlockSpec((1, GMLP_WIDTH), lambda i: (0, 0)),
                  pl.BlockSpec((1, GMLP_WIDTH), lambda i: (0, 0)),
                  pl.BlockSpec((GMLP_GROUPS, GMLP_CHUNK, GMLP_CHUNK), lambda i: (0, 0, 0)),
                  pl.BlockSpec((GMLP_CHUNK, GMLP_GROUPS), lambda i: (0, 0))],
        out_specs=pl.BlockSpec((tm, GMLP_WIDTH), lambda i: (i, 0)),
        out_shape=jax.ShapeDtypeStruct((S, GMLP_WIDTH), BF16),
        compiler_params=_params("parallel"),
        name="gmlp_sgu",
    )(zg, zg, ln_g.reshape(1, -1), ln_b.reshape(1, -1), w_s, b_s.T)


def _merge_kernel(at_ref, gm_ref, ga_ref, gg_ref, h_ref, wa_ref, wg_ref, wo_ref, g_ref, b_ref,
                  h1_ref, h1b_ref, h1bt_ref):
    a1 = jnp.dot(at_ref[...], wa_ref[...], preferred_element_type=F32)
    a2 = jnp.dot(gm_ref[...], wg_ref[...], preferred_element_type=F32)
    merged = jax.nn.sigmoid(ga_ref[...]) * a1 + jax.nn.sigmoid(gg_ref[...]) * a2
    y = DEEPNORM_ALPHA * h_ref[...] + jnp.dot(merged.astype(BF16), wo_ref[...], preferred_element_type=F32)
    h1 = _layer_norm(y, g_ref[...], b_ref[...])
    h1_ref[...] = h1
    h1b_ref[...] = h1.astype(BF16)
    h1bt_ref[...] = h1.T.astype(BF16)


def _merge(attn, gm, zg, h, wa, wg, wo, ln_g, ln_b, tm):
    S, D = h.shape
    W = attn.shape[1]
    return pl.pallas_call(
        _merge_kernel,
        grid=(S // tm,),
        in_specs=[pl.BlockSpec((tm, W), lambda i: (i, 0)),
                  pl.BlockSpec((tm, W), lambda i: (i, 0)),
                  pl.BlockSpec((tm, D), lambda i: (i, 1)),
                  pl.BlockSpec((tm, D), lambda i: (i, 2)),
                  pl.BlockSpec((tm, D), lambda i: (i, 0)),
                  _resident((W, D), lambda i: (0, 0)),
                  _resident((W, D), lambda i: (0, 0)),
                  _resident((D, D), lambda i: (0, 0)),
                  pl.BlockSpec((1, D), lambda i: (0, 0)),
                  pl.BlockSpec((1, D), lambda i: (0, 0))],
        out_specs=[pl.BlockSpec((tm, D), lambda i: (i, 0)),
                   pl.BlockSpec((tm, D), lambda i: (i, 0)),
                   pl.BlockSpec((D, tm), lambda i: (0, i))],
        out_shape=[jax.ShapeDtypeStruct((S, D), F32), jax.ShapeDtypeStruct((S, D), BF16),
                   jax.ShapeDtypeStruct((D, S), BF16)],
        compiler_params=_params("parallel"),
        name="merge_ln1",
    )(attn, gm, zg, zg, h, wa, wg, wo, ln_g.reshape(1, D), ln_b.reshape(1, D))


def _peer_pairs():
    pairs = [(a, b) for a in range(PEER_TOPK) for b in range(PEER_TOPK) if (a + 1) * (b + 1) <= PEER_TOPK]
    return sorted(pairs, key=lambda ab: ab[0] * PEER_TOPK + ab[1])


def _extract_top(work, n_rounds, on_pick, exact):
    n = float(work.shape[0])
    iota = lax.broadcasted_iota(I32, work.shape, 0).astype(F32)
    for r in range(n_rounds):
        mx = jnp.max(work, axis=0, keepdims=True)
        pick = work == mx
        if exact:
            first = jnp.min(jnp.where(pick, iota, n), axis=0, keepdims=True)
            pick = iota == first
        on_pick(r, mx, pick)
        work = jnp.where(pick, -jnp.inf, work)


def _route_kernel(hb_ref, wq_ref, sk_ref, u_ref, v_ref, l1_ref, a_ref, r2_ref, b_ref, ub_ref, vb_ref, *, T):
    ub_ref[...] = u_ref[...].astype(BF16)
    vb_ref[...] = v_ref[...].astype(BF16)
    qt = jnp.dot(hb_ref[...], wq_ref[...], preferred_element_type=F32).T
    pairs = _peer_pairs()

    def route(h, exact):
        miscount = jnp.zeros((1, T), F32)
        s, rank, vals = [], [], []
        for p in range(2):
            idx = 2 * h + p
            sp = jnp.dot(sk_ref[idx], qt[idx * LANES:(idx + 1) * LANES, :].astype(BF16),
                         preferred_element_type=F32)
            state = {"rank": jnp.full(sp.shape, float(PEER_TOPK), F32), "vals": []}

            def on_pick(r, mx, pick, state=state):
                state["rank"] = jnp.where(pick, float(r), state["rank"])
                state["vals"].append(mx)

            _extract_top(sp, PEER_TOPK, on_pick, exact)
            n_ranked = jnp.sum(jnp.where(state["rank"] < float(PEER_TOPK), 1.0, 0.0), axis=0, keepdims=True)
            miscount = jnp.maximum(miscount, jnp.abs(n_ranked - float(PEER_TOPK)))
            s.append(sp)
            rank.append(state["rank"])
            vals.append(state["vals"])
        cand = jnp.concatenate([vals[0][a] + vals[1][b] for a, b in pairs], axis=0)
        picked = {"mask": jnp.zeros(cand.shape, jnp.bool_)}

        def on_pick_c(r, mx, pick, picked=picked):
            picked["mask"] = picked["mask"] | pick

        _extract_top(cand, PEER_TOPK, on_pick_c, True)
        top = vals[0][0] + vals[1][0]
        selw = jnp.where(picked["mask"], jnp.exp(cand - top), 0.0)
        z = jnp.sum(selw, axis=0, keepdims=True)
        self32 = jnp.where(picked["mask"], 1.0, 0.0)
        l1 = jnp.zeros(s[0].shape, F32)
        pos = 0
        for a in range(PEER_TOPK):
            cnt = sum(1 for ab in pairs if ab[0] == a)
            n_a = jnp.sum(self32[pos:pos + cnt, :], axis=0, keepdims=True)
            pos += cnt
            l1 = jnp.where(rank[0] == float(a), n_a, l1)
        l1_ref[h] = l1
        a_ref[h] = jnp.exp(s[0] - vals[0][0])
        r2_ref[h] = rank[1].astype(r2_ref.dtype)
        b_ref[h] = (jnp.exp(s[1] - vals[1][0]) / z).astype(b_ref.dtype)
        return miscount

    miscount = functools.reduce(jnp.maximum, [route(h, exact=False) for h in range(PEER_HEADS)])

    @pl.when(jnp.max(miscount) > 0.0)
    def _():
        for h in range(PEER_HEADS):
            route(h, exact=True)


def _peer_route(h1b, wq, sk, expert_u, expert_v, T):
    S, D = h1b.shape
    n_steps = S // T
    E = expert_u.shape[0]
    kernel = functools.partial(_route_kernel, T=T)
    shp = jax.ShapeDtypeStruct((PEER_HEADS, PEER_N_KEYS, S), F32)
    shp_b = jax.ShapeDtypeStruct((PEER_HEADS, PEER_N_KEYS, S), BF16)
    spec = pl.BlockSpec((PEER_HEADS, PEER_N_KEYS, T), lambda t: (0, 0, t))
    table = pl.BlockSpec((E // n_steps, D), lambda t: (t, 0))
    return pl.pallas_call(
        kernel,
        grid=(n_steps,),
        in_specs=[pl.BlockSpec((T, D), lambda t: (t, 0)),
                  _resident(wq.shape, lambda t: (0, 0)),
                  _resident(sk.shape, lambda t: (0, 0, 0)),
                  table, table],
        out_specs=[spec, spec, spec, spec, table, table],
        out_shape=[shp, shp, shp_b, shp_b,
                   jax.ShapeDtypeStruct(expert_u.shape, BF16), jax.ShapeDtypeStruct(expert_v.shape, BF16)],
        compiler_params=_params("parallel"),
        name="peer_route",
    )(h1b, wq, sk, expert_u, expert_v)


def _peer_kernel(ht_ref, u_ref, v_ref, l1_ref, a_ref, r2_ref, b_ref, o_ref, act_ref, *, n_i1, ce):
    @pl.when(pl.program_id(1) == 0)
    def _():
        o_ref[...] = jnp.zeros_like(o_ref)

    T = ht_ref.shape[1]
    n_slab = PEER_N_KEYS // PACKED_ROWS
    i1_per_chunk = ce // PEER_N_KEYS
    r2 = [r2_ref[h] for h in range(PEER_HEADS)]
    b = [b_ref[h] for h in range(PEER_HEADS)]

    def gates(c):
        tiles = []
        for j in range(c * i1_per_chunk, (c + 1) * i1_per_chunk):
            gate = [None] * n_slab
            for h in range(PEER_HEADS):
                l1 = jnp.broadcast_to(l1_ref[h, j:j + 1, :], (PACKED_ROWS, T)).astype(BF16)
                a = jnp.broadcast_to(a_ref[h, j:j + 1, :], (PACKED_ROWS, T)).astype(BF16)
                for s in range(n_slab):
                    slab = slice(s * PACKED_ROWS, (s + 1) * PACKED_ROWS)
                    val = jnp.where(r2[h][slab] < l1, b[h][slab], 0.0) * a
                    gate[s] = val if gate[s] is None else gate[s] + val
            tiles.extend(gate)
        return tiles

    n_chunk = n_i1 * PEER_N_KEYS // ce
    zero = jnp.minimum(pl.program_id(1), 0)
    for c in range(n_chunk):
        act_ref[c + zero] = jnp.dot(u_ref[c * ce:(c + 1) * ce, :], ht_ref[...],
                                    preferred_element_type=F32)
    tiles = []
    for c in range(n_chunk):
        gel = _gelu(act_ref[c + zero].astype(BF16))
        g = gates(c)
        tiles.extend(gel[k * PACKED_ROWS:(k + 1) * PACKED_ROWS] * g[k] for k in range(len(g)))
    pt = jnp.concatenate(tiles, axis=0)
    o_ref[...] += lax.dot_general(pt, v_ref[...], TN_DIMS, preferred_element_type=F32)


def _peer_experts(h1bt, u_b, v_b, l1, a, r2, b, T, eb):
    D, S = h1bt.shape
    E = u_b.shape[0]
    n_i1 = eb // PEER_N_KEYS
    ce = MXU_DIM // 2
    kernel = functools.partial(_peer_kernel, n_i1=n_i1, ce=ce)
    row_spec = pl.BlockSpec((PEER_HEADS, n_i1, T), lambda t, e: (0, e, t))
    full_spec = pl.BlockSpec((PEER_HEADS, PEER_N_KEYS, T), lambda t, e: (0, 0, t))
    return pl.pallas_call(
        kernel,
        grid=(S // T, E // eb),
        in_specs=[pl.BlockSpec((D, T), lambda t, e: (0, t)),
                  pl.BlockSpec((eb, D), lambda t, e: (e, 0)),
                  pl.BlockSpec((eb, D), lambda t, e: (e, 0)),
                  row_spec, row_spec, full_spec, full_spec],
        out_specs=pl.BlockSpec((T, D), lambda t, e: (t, 0)),
        out_shape=jax.ShapeDtypeStruct((S, D), F32),
        scratch_shapes=[pltpu.VMEM((eb // ce, ce, T), F32)],
        compiler_params=_params("parallel", "arbitrary"),
        name="peer_experts",
    )(h1bt, u_b, v_b, l1, a, r2, b)


def _ple_kernel(h1_ref, peer_ref, p_ref, wg_ref, wp_ref, g_ref, b_ref, o_ref):
    r = DEEPNORM_ALPHA * h1_ref[...] + peer_ref[...]
    gate = jax.nn.sigmoid(jnp.dot(r.astype(BF16), wg_ref[...], preferred_element_type=F32))
    proj = jnp.dot(p_ref[...], wp_ref[...], preferred_element_type=F32)
    o_ref[...] = _layer_norm(r + gate * proj, g_ref[...], b_ref[...])


def _ple(h1, peer, p_b, wg, wp, ln_g, ln_b, tm):
    S, D = h1.shape
    P = p_b.shape[1]
    return pl.pallas_call(
        _ple_kernel,
        grid=(S // tm,),
        in_specs=[pl.BlockSpec((tm, D), lambda i: (i, 0)),
                  pl.BlockSpec((tm, D), lambda i: (i, 0)),
                  pl.BlockSpec((tm, P), lambda i: (i, 0)),
                  _resident((D, D), lambda i: (0, 0)),
                  _resident((P, D), lambda i: (0, 0)),
                  pl.BlockSpec((1, D), lambda i: (0, 0)),
                  pl.BlockSpec((1, D), lambda i: (0, 0))],
        out_specs=pl.BlockSpec((tm, D), lambda i: (i, 0)),
        out_shape=jax.ShapeDtypeStruct((S, D), F32),
        compiler_params=_params("parallel"),
        name="ple_ln2",
    )(h1, peer, p_b, wg, wp, ln_g.reshape(1, D), ln_b.reshape(1, D))


def _layer(x, p, ln_emb_g, ln_emb_b, rel_bias, w_in, gmlp_ln_g, gmlp_ln_b, gmlp_w_s, gmlp_b_s,
           w_br_attn, w_br_gmlp, w_mix_out, ln1_g, ln1_b, peer_w_q, peer_sub_keys, peer_u, peer_v,
           ple_w_proj, ple_w_gate, ln2_g, ln2_b):
    S, D = x.shape
    tm = min(512, S)
    o_k, o_v, o_qi = ATTN_WIDTH, 2 * ATTN_WIDTH, 3 * ATTN_WIDTH
    o_ki = o_qi + N_IDX_HEADS * IDX_HEAD_DIM
    o_gu = o_ki + IDX_HEAD_DIM + N_IDX_HEADS

    h, hb = _ln_embed(x, ln_emb_g, ln_emb_b, tm)
    wqt = jnp.concatenate([w_in[:, :o_k], w_in[:, o_qi:o_ki]], axis=1).T.astype(BF16)
    w1 = w_in[:, o_k:o_qi].astype(BF16)
    w2 = jnp.pad(w_in[:, o_ki:o_gu], ((0, 0), (0, LANES - (o_gu - o_ki)))).astype(BF16)
    w3 = w_in[:, o_gu:].astype(BF16)
    qt = _matmul_t(wqt, hb, BF16, tm, "in_proj_q")
    kv = _matmul(hb, w1, BF16, min(1024, S), 512, "in_proj_kv")
    kw = _matmul(hb, w2, F32, min(1024, S), LANES, "in_proj_index")
    zg = _matmul(hb, w3, F32, min(1024, S), 512, "in_proj_gmlp")

    attn = _dsa_attention(qt, kv, kw[:, :IDX_HEAD_DIM].astype(BF16),
                          kw[:, IDX_HEAD_DIM:IDX_HEAD_DIM + N_IDX_HEADS].T, rel_bias, bq=256)
    gm = _sgu(zg, gmlp_ln_g, gmlp_ln_b, gmlp_w_s, gmlp_b_s, tm)
    h1, h1b, h1bt = _merge(attn, gm, zg, h, w_br_attn.astype(BF16), w_br_gmlp.astype(BF16),
                           w_mix_out.astype(BF16), ln1_g, ln1_b, min(256, S))

    sk = peer_sub_keys.reshape(PEER_HEADS * 2, PEER_N_KEYS, -1).astype(BF16)
    l1, a, r2, b, u_b, v_b = _peer_route(h1b, peer_w_q.astype(BF16), sk, peer_u, peer_v, min(256, S))
    peer = _peer_experts(h1bt, u_b, v_b, l1, a, r2, b, T=min(512, S), eb=1024)
    return _ple(h1, peer, p.astype(BF16), ple_w_gate.astype(BF16), ple_w_proj.astype(BF16),
                ln2_g, ln2_b, min(512, S))


def kernel(x, p, positions, ln_emb_g, ln_emb_b, rel_bias, w_in, gmlp_ln_g, gmlp_ln_b, gmlp_w_s, gmlp_b_s, w_br_attn, w_br_gmlp, w_mix_out, ln1_g, ln1_b, peer_w_q, peer_sub_keys, peer_u, peer_v, ple_w_proj, ple_w_gate, ln2_g, ln2_b):
    del positions
    assert x.shape[0] == 1 and w_in.shape[0] == DEPTH
    out = _layer(x[0], p[0, 0], ln_emb_g, ln_emb_b, rel_bias, w_in[0], gmlp_ln_g[0], gmlp_ln_b[0],
                 gmlp_w_s[0], gmlp_b_s[0], w_br_attn[0], w_br_gmlp[0], w_mix_out[0], ln1_g[0],
                 ln1_b[0], peer_w_q[0], peer_sub_keys[0], peer_u[0], peer_v[0], ple_w_proj[0],
                 ple_w_gate[0], ln2_g[0], ln2_b[0])
    return out[None]
```

```python
import functools
import math

import jax
import jax.numpy as jnp
from jax import lax
from jax.experimental import pallas as pl
from jax.experimental.pallas import tpu as pltpu

F32 = jnp.float32
BF16 = jnp.bfloat16
I32 = jnp.int32

LN_EPS = 1e-5
DEPTH = 1
DEEPNORM_ALPHA = (2.0 * DEPTH) ** 0.25

N_ATTN_HEADS = 8
ATTN_HEAD_DIM = 128
ATTN_WIDTH = N_ATTN_HEADS * ATTN_HEAD_DIM
N_IDX_HEADS = 16
IDX_HEAD_DIM = 64
TOPK_MAX = 256
GMLP_CHUNK = 128
GMLP_GROUPS = 8
GMLP_WIDTH = 1024
N_REL_BUCKETS = 32
REL_MAX_DISTANCE = 128
PEER_HEADS = 8
PEER_N_KEYS = 128
PEER_TOPK = 16

LANES = 128
SUBLANES = 8
PACKED_ROWS = 2 * SUBLANES
MXU_DIM = 256
NEG = -1e30
SEARCH_STEPS_MAX = 64
NT_DIMS = (((1,), (1,)), ((), ()))
TN_DIMS = (((0,), (0,)), ((), ()))
VMEM_LIMIT = 56 * 1024 * 1024


def _params(*sem):
    return pltpu.CompilerParams(dimension_semantics=sem, vmem_limit_bytes=VMEM_LIMIT)


def _resident(shape, index_map):
    return pl.BlockSpec(shape, index_map, pipeline_mode=pl.Buffered(1))


def _gelu(x):
    return 0.5 * x * (1.0 + jnp.tanh(0.7978845608028654 * (x + 0.044715 * (x * x * x))))


def _layer_norm(x, g, b):
    mu = jnp.mean(x, axis=-1, keepdims=True)
    xc = x - mu
    var = jnp.mean(xc * xc, axis=-1, keepdims=True)
    return xc * lax.rsqrt(var + LN_EPS) * g + b


def _ln_kernel(x_ref, g_ref, b_ref, h_ref, hb_ref):
    y = _layer_norm(x_ref[...], g_ref[...], b_ref[...])
    h_ref[...] = y
    hb_ref[...] = y.astype(BF16)


def _ln_embed(x, g, b, tm):
    S, D = x.shape
    return pl.pallas_call(
        _ln_kernel,
        grid=(S // tm,),
        in_specs=[pl.BlockSpec((tm, D), lambda i: (i, 0)),
                  pl.BlockSpec((1, D), lambda i: (0, 0)),
                  pl.BlockSpec((1, D), lambda i: (0, 0))],
        out_specs=[pl.BlockSpec((tm, D), lambda i: (i, 0)),
                   pl.BlockSpec((tm, D), lambda i: (i, 0))],
        out_shape=[jax.ShapeDtypeStruct((S, D), F32), jax.ShapeDtypeStruct((S, D), BF16)],
        compiler_params=_params("parallel"),
        name="ln_embed",
    )(x, g.reshape(1, D), b.reshape(1, D))


def _mm_kernel(a_ref, w_ref, o_ref):
    o_ref[...] = jnp.dot(a_ref[...], w_ref[...], preferred_element_type=F32).astype(o_ref.dtype)


def _matmul(a, w, out_dtype, tm, tn, name):
    M, K = a.shape
    N = w.shape[1]
    return pl.pallas_call(
        _mm_kernel,
        grid=(M // tm, N // tn),
        in_specs=[pl.BlockSpec((tm, K), lambda i, j: (i, 0)),
                  pl.BlockSpec((K, tn), lambda i, j: (0, j))],
        out_specs=pl.BlockSpec((tm, tn), lambda i, j: (i, j)),
        out_shape=jax.ShapeDtypeStruct((M, N), out_dtype),
        compiler_params=_params("parallel", "arbitrary"),
        name=name,
    )(a, w)


def _mm_t_kernel(wt_ref, a_ref, o_ref):
    o_ref[...] = lax.dot_general(wt_ref[...], a_ref[...], NT_DIMS,
                                 preferred_element_type=F32).astype(o_ref.dtype)


def _matmul_t(wt, a, out_dtype, tm, name):
    M, K = a.shape
    N = wt.shape[0]
    return pl.pallas_call(
        _mm_t_kernel,
        grid=(M // tm,),
        in_specs=[_resident((N, K), lambda i: (0, 0)),
                  pl.BlockSpec((tm, K), lambda i: (i, 0))],
        out_specs=pl.BlockSpec((N, tm), lambda i: (0, i)),
        out_shape=jax.ShapeDtypeStruct((N, M), out_dtype),
        compiler_params=_params("parallel"),
        name=name,
    )(wt, a)


def _reduce_row_groups(t, op):
    parts = [t[j * SUBLANES:(j + 1) * SUBLANES, :] for j in range(t.shape[0] // SUBLANES)]
    while len(parts) > 1:
        parts = [op(parts[j], parts[j + 1]) for j in range(0, len(parts) - 1, 2)] + (
            [parts[-1]] if len(parts) % 2 else [])
    return parts[0]


def _dsa_kernel(qit_ref, wt_ref, qt_ref, ki_ref, k_ref, v_ref, bias_ref, o_ref,
                sc_ref, m_ref, l_ref, acc_ref, *, bq, top_k):
    i = pl.program_id(0)
    w = wt_ref[...] * ((IDX_HEAD_DIM ** -0.5) * (N_IDX_HEADS ** -0.5))
    key_i = lax.broadcasted_iota(I32, (bq, bq), 0)
    qry_i = lax.broadcasted_iota(I32, (bq, bq), 1)
    causal = key_i <= qry_i

    def rows(c):
        return pl.ds(pl.multiple_of(c * bq, bq), bq)

    def scores(c):
        kic = ki_ref[rows(c), :]
        acc = jnp.zeros((bq, bq), F32)
        for h in range(N_IDX_HEADS):
            d = jnp.dot(kic, qit_ref[h * IDX_HEAD_DIM:(h + 1) * IDX_HEAD_DIM, :],
                        preferred_element_type=F32)
            acc = acc + w[h:h + 1, :] * jnp.maximum(d, 0.0)
        return acc

    def score_chunk(c, carry):
        sc_ref[c] = scores(c)
        return carry

    lax.fori_loop(0, i, score_chunk, 0)
    sc_ref[i] = jnp.where(causal, scores(i), -jnp.inf)

    def lowest_highest(c, lh):
        x = sc_ref[c]
        return (jnp.minimum(lh[0], _reduce_row_groups(x, jnp.minimum)),
                jnp.maximum(lh[1], _reduce_row_groups(x, jnp.maximum)))

    lo8, hi8 = lax.fori_loop(0, i, lowest_highest, (jnp.full((SUBLANES, bq), jnp.inf, F32),
                                                    jnp.full((SUBLANES, bq), -jnp.inf, F32)))
    diag = sc_ref[i]
    lo8 = jnp.minimum(lo8, _reduce_row_groups(jnp.where(causal, diag, jnp.inf), jnp.minimum))
    hi8 = jnp.maximum(hi8, _reduce_row_groups(diag, jnp.maximum))
    lo0 = jnp.min(lo8, axis=0, keepdims=True)
    hi0 = jnp.max(hi8, axis=0, keepdims=True)

    def count_ge(cand):
        def body(c, cnt):
            return cnt + _reduce_row_groups(jnp.where(sc_ref[c] >= cand, 1.0, 0.0), jnp.add)
        cnt = lax.fori_loop(0, i + 1, body, jnp.zeros((SUBLANES, bq), F32))
        return jnp.sum(cnt, axis=0, keepdims=True)

    n_keys = (i * bq + 1 + lax.broadcasted_iota(I32, (1, bq), 1)).astype(F32)
    few_keys = n_keys < float(top_k)

    def bisect(state):
        it, lo, hi, settled = state
        mid = lo + (hi - lo) * 0.5
        cnt = count_ge(mid)
        enough = cnt >= float(top_k)
        open_ = settled == 0.0
        lo_new = jnp.where(open_ & enough, mid, lo)
        hi_new = jnp.where(open_ & jnp.logical_not(enough), mid, hi)
        done = (cnt == float(top_k)) | (mid <= lo) | (mid >= hi)
        return it + 1, lo_new, hi_new, jnp.where(done, 1.0, settled)

    def unsettled(state):
        it, _, _, settled = state
        return (it < SEARCH_STEPS_MAX) & (jnp.min(settled) == 0.0)

    _, lo, _, _ = lax.while_loop(
        unsettled, lambda st: bisect(bisect(st)), (jnp.int32(0), lo0, hi0, jnp.where(few_keys, 1.0, 0.0)))
    thresh = jnp.where(few_keys, -jnp.inf, lo)

    def mask_chunk(c, carry):
        sc_ref[c] = jnp.where(sc_ref[c] >= thresh, 0.0, NEG)
        return carry

    lax.fori_loop(0, i, mask_chunk, 0)
    sc_ref[i] = jnp.where(causal, jnp.where(sc_ref[i] >= thresh, 0.0, NEG), NEG)

    scale = ATTN_HEAD_DIM ** -0.5
    m_ref[...] = jnp.full(m_ref.shape, NEG, F32)
    l_ref[...] = jnp.zeros(l_ref.shape, F32)
    acc_ref[...] = jnp.zeros(acc_ref.shape, F32)

    def attend(c, near):
        mbias = sc_ref[c]
        m_all, l_all = m_ref[...], l_ref[...]
        m_rows, l_rows = [], []

        def qk(h):
            cols = slice(h * ATTN_HEAD_DIM, (h + 1) * ATTN_HEAD_DIM)
            return jnp.dot(k_ref[rows(c), cols], qt_ref[cols, :], preferred_element_type=F32)

        s_raw = [qk(h) for h in range(N_ATTN_HEADS)]
        for h in range(N_ATTN_HEADS):
            cols = slice(h * ATTN_HEAD_DIM, (h + 1) * ATTN_HEAD_DIM)
            s = s_raw[h] * scale + mbias
            if near is not None:
                s = s + bias_ref[h, near * bq:(near + 1) * bq, :]
            m_old = m_all[h:h + 1, :]
            m_new = jnp.maximum(m_old, jnp.max(_reduce_row_groups(s, jnp.maximum), axis=0, keepdims=True))
            p = jnp.exp(s - m_new)
            a = jnp.exp(m_old - m_new)
            m_rows.append(m_new)
            l_rows.append(a * l_all[h:h + 1, :]
                          + jnp.sum(_reduce_row_groups(p, jnp.add), axis=0, keepdims=True))
            pv = lax.dot_general(v_ref[rows(c), cols], p.astype(BF16), TN_DIMS, preferred_element_type=F32)
            acc_ref[h] = a * acc_ref[h] + pv
        m_ref[...] = jnp.concatenate(m_rows, axis=0)
        l_ref[...] = jnp.concatenate(l_rows, axis=0)

    def far_chunk(c, carry):
        attend(c, None)
        return carry

    lax.fori_loop(0, jnp.maximum(i - 1, 0), far_chunk, 0)

    @pl.when(i > 0)
    def _():
        attend(i - 1, 0)

    attend(i, 1)
    for h in range(N_ATTN_HEADS):
        out_t = acc_ref[h] / l_ref[h:h + 1, :]
        o_ref[:, h * ATTN_HEAD_DIM:(h + 1) * ATTN_HEAD_DIM] = out_t.T.astype(o_ref.dtype)


def _rel_bucket(dist):
    n = jnp.maximum(dist, 0)
    max_exact = N_REL_BUCKETS // 2
    nf = jnp.maximum(n, 1).astype(F32)
    large = max_exact + (jnp.log(nf / max_exact) / math.log(REL_MAX_DISTANCE / max_exact)
                         * (N_REL_BUCKETS - max_exact)).astype(I32)
    large = jnp.minimum(large, N_REL_BUCKETS - 1)
    return jnp.where(n < max_exact, n, large)


def _near_bias_tiles(rel_bias, bq):
    assert bq >= REL_MAX_DISTANCE
    key = jnp.arange(2 * bq, dtype=I32)[:, None]
    qry = jnp.arange(bq, dtype=I32)[None, :]
    bucket = _rel_bucket(qry + bq - key)
    rel = (rel_bias - rel_bias[N_REL_BUCKETS - 1]).T
    tiles = jnp.broadcast_to(rel[:, 0][:, None, None], (rel.shape[0],) + bucket.shape)
    for b in range(1, N_REL_BUCKETS):
        tiles = jnp.where(bucket[None] >= b, rel[:, b][:, None, None], tiles)
    return tiles


def _dsa_attention(qt, kv, ki_b, wt, rel_bias, bq):
    S = kv.shape[0]
    top_k = min(TOPK_MAX, S // 4)
    n_blk = S // bq
    bias = _near_bias_tiles(rel_bias, bq)
    kernel = functools.partial(_dsa_kernel, bq=bq, top_k=top_k)
    return pl.pallas_call(
        kernel,
        grid=(n_blk,),
        in_specs=[pl.BlockSpec((N_IDX_HEADS * IDX_HEAD_DIM, bq), lambda i: (1, i)),
                  pl.BlockSpec((N_IDX_HEADS, bq), lambda i: (0, i)),
                  pl.BlockSpec((ATTN_WIDTH, bq), lambda i: (0, i)),
                  _resident((S, IDX_HEAD_DIM), lambda i: (0, 0)),
                  _resident((S, ATTN_WIDTH), lambda i: (0, 0)),
                  _resident((S, ATTN_WIDTH), lambda i: (0, 1)),
                  _resident((N_ATTN_HEADS, 2 * bq, bq), lambda i: (0, 0, 0))],
        out_specs=pl.BlockSpec((bq, ATTN_WIDTH), lambda i: (i, 0)),
        out_shape=jax.ShapeDtypeStruct((S, ATTN_WIDTH), BF16),
        scratch_shapes=[pltpu.VMEM((n_blk, bq, bq), F32),
                        pltpu.VMEM((N_ATTN_HEADS, bq), F32),
                        pltpu.VMEM((N_ATTN_HEADS, bq), F32),
                        pltpu.VMEM((N_ATTN_HEADS, ATTN_HEAD_DIM, bq), F32)],
        compiler_params=_params("arbitrary"),
        name="dsa_attention",
    )(qt, wt, qt, ki_b, kv, kv, bias)


def _sgu_kernel(gu_ref, gv_ref, g_ref, b_ref, ws_ref, bs_ref, o_ref, *, n_chunk):
    u = _gelu(gu_ref[...])
    v = _layer_norm(_gelu(gv_ref[...]), g_ref[...], b_ref[...]).astype(BF16)
    r = lax.broadcasted_iota(I32, (GMLP_CHUNK, GMLP_CHUNK), 0)
    c = lax.broadcasted_iota(I32, (GMLP_CHUNK, GMLP_CHUNK), 1)
    for g in range(GMLP_GROUPS):
        wg = jnp.where(c <= r, ws_ref[g], 0.0).astype(BF16)
        bcol = bs_ref[:, g:g + 1]
        cols = slice(g * LANES, (g + 1) * LANES)
        for n in range(n_chunk):
            rws = slice(n * GMLP_CHUNK, (n + 1) * GMLP_CHUNK)
            mixed = jnp.dot(wg, v[rws, cols], preferred_element_type=F32) + bcol
            o_ref[rws, cols] = (u[rws, cols] * mixed).astype(o_ref.dtype)


def _sgu(zg, ln_g, ln_b, w_s, b_s, tm):
    S = zg.shape[0]
    kernel = functools.partial(_sgu_kernel, n_chunk=tm // GMLP_CHUNK)
    return pl.pallas_call(
        kernel,
        grid=(S // tm,),
        in_specs=[pl.BlockSpec((tm, GMLP_WIDTH), lambda i: (i, 0)),
                  pl.BlockSpec((tm, GMLP_WIDTH), lambda i: (i, 1)),
                  pl.BlockSpec((1, GMLP_WIDTH), lambda i: (0, 0)),
                  pl.BlockSpec((1, GMLP_WIDTH), lambda i: (0, 0)),
                  pl.BlockSpec((GMLP_GROUPS, GMLP_CHUNK, GMLP_CHUNK), lambda i: (0, 0, 0)),
                  pl.BlockSpec((GMLP_CHUNK, GMLP_GROUPS), lambda i: (0, 0))],
        out_specs=pl.BlockSpec((tm, GMLP_WIDTH), lambda i: (i, 0)),
        out_shape=jax.ShapeDtypeStruct((S, GMLP_WIDTH), BF16),
        compiler_params=_params("parallel"),
        name="gmlp_sgu",
    )(zg, zg, ln_g.reshape(1, -1), ln_b.reshape(1, -1), w_s, b_s.T)


def _merge_kernel(at_ref, gm_ref, ga_ref, gg_ref, h_ref, wa_ref, wg_ref, wo_ref, g_ref, b_ref,
                  h1_ref, h1b_ref, h1bt_ref):
    a1 = jnp.dot(at_ref[...], wa_ref[...], preferred_element_type=F32)
    a2 = jnp.dot(gm_ref[...], wg_ref[...], preferred_element_type=F32)
    merged = jax.nn.sigmoid(ga_ref[...]) * a1 + jax.nn.sigmoid(gg_ref[...]) * a2
    y = DEEPNORM_ALPHA * h_ref[...] + jnp.dot(merged.astype(BF16), wo_ref[...], preferred_element_type=F32)
    h1 = _layer_norm(y, g_ref[...], b_ref[...])
    h1_ref[...] = h1
    h1b_ref[...] = h1.astype(BF16)
    h1bt_ref[...] = h1.T.astype(BF16)


def _merge(attn, gm, zg, h, wa, wg, wo, ln_g, ln_b, tm):
    S, D = h.shape
    W = attn.shape[1]
    return pl.pallas_call(
        _merge_kernel,
        grid=(S // tm,),
        in_specs=[pl.BlockSpec((tm, W), lambda i: (i, 0)),
                  pl.BlockSpec((tm, W), lambda i: (i, 0)),
                  pl.BlockSpec((tm, D), lambda i: (i, 1)),
                  pl.BlockSpec((tm, D), lambda i: (i, 2)),
                  pl.BlockSpec((tm, D), lambda i: (i, 0)),
                  _resident((W, D), lambda i: (0, 0)),
                  _resident((W, D), lambda i: (0, 0)),
                  _resident((D, D), lambda i: (0, 0)),
                  pl.BlockSpec((1, D), lambda i: (0, 0)),
                  pl.BlockSpec((1, D), lambda i: (0, 0))],
        out_specs=[pl.BlockSpec((tm, D), lambda i: (i, 0)),
                   pl.BlockSpec((tm, D), lambda i: (i, 0)),
                   pl.BlockSpec((D, tm), lambda i: (0, i))],
        out_shape=[jax.ShapeDtypeStruct((S, D), F32), jax.ShapeDtypeStruct((S, D), BF16),
                   jax.ShapeDtypeStruct((D, S), BF16)],
        compiler_params=_params("parallel"),
        name="merge_ln1",
    )(attn, gm, zg, zg, h, wa, wg, wo, ln_g.reshape(1, D), ln_b.reshape(1, D))


def _peer_pairs():
    pairs = [(a, b) for a in range(PEER_TOPK) for b in range(PEER_TOPK) if (a + 1) * (b + 1) <= PEER_TOPK]
    return sorted(pairs, key=lambda ab: ab[0] * PEER_TOPK + ab[1])


def _extract_top(work, n_rounds, on_pick, exact):
    n = float(work.shape[0])
    iota = lax.broadcasted_iota(I32, work.shape, 0).astype(F32)
    for r in range(n_rounds):
        mx = jnp.max(work, axis=0, keepdims=True)
        pick = work == mx
        if exact:
            first = jnp.min(jnp.where(pick, iota, n), axis=0, keepdims=True)
            pick = iota == first
        on_pick(r, mx, pick)
        work = jnp.where(pick, -jnp.inf, work)


def _route_kernel(hb_ref, wq_ref, sk_ref, u_ref, v_ref, l1_ref, a_ref, r2_ref, b_ref, ub_ref, vb_ref, *, T):
    ub_ref[...] = u_ref[...].astype(BF16)
    vb_ref[...] = v_ref[...].astype(BF16)
    qt = jnp.dot(hb_ref[...], wq_ref[...], preferred_element_type=F32).T
    pairs = _peer_pairs()

    def route(h, exact):
        miscount = jnp.zeros((1, T), F32)
        s, rank, vals = [], [], []
        for p in range(2):
            idx = 2 * h + p
            sp = jnp.dot(sk_ref[idx], qt[idx * LANES:(idx + 1) * LANES, :].astype(BF16),
                         preferred_element_type=F32)
            state = {"rank": jnp.full(sp.shape, float(PEER_TOPK), F32), "vals": []}

            def on_pick(r, mx, pick, state=state):
                state["rank"] = jnp.where(pick, float(r), state["rank"])
                state["vals"].append(mx)

            _extract_top(sp, PEER_TOPK, on_pick, exact)
            n_ranked = jnp.sum(jnp.where(state["rank"] < float(PEER_TOPK), 1.0, 0.0), axis=0, keepdims=True)
            miscount = jnp.maximum(miscount, jnp.abs(n_ranked - float(PEER_TOPK)))
            s.append(sp)
            rank.append(state["rank"])
            vals.append(state["vals"])
        cand = jnp.concatenate([vals[0][a] + vals[1][b] for a, b in pairs], axis=0)
        picked = {"mask": jnp.zeros(cand.shape, jnp.bool_)}

        def on_pick_c(r, mx, pick, picked=picked):
            picked["mask"] = picked["mask"] | pick

        _extract_top(cand, PEER_TOPK, on_pick_c, True)
        top = vals[0][0] + vals[1][0]
        selw = jnp.where(picked["mask"], jnp.exp(cand - top), 0.0)
        z = jnp.sum(selw, axis=0, keepdims=True)
        self32 = jnp.where(picked["mask"], 1.0, 0.0)
        l1 = jnp.zeros(s[0].shape, F32)
        pos = 0
        for a in range(PEER_TOPK):
            cnt = sum(1 for ab in pairs if ab[0] == a)
            n_a = jnp.sum(self32[pos:pos + cnt, :], axis=0, keepdims=True)
            pos += cnt
            l1 = jnp.where(rank[0] == float(a), n_a, l1)
        l1_ref[h] = l1
        a_ref[h] = jnp.exp(s[0] - vals[0][0])
        r2_ref[h] = rank[1].astype(r2_ref.dtype)
        b_ref[h] = (jnp.exp(s[1] - vals[1][0]) / z).astype(b_ref.dtype)
        return miscount

    miscount = functools.reduce(jnp.maximum, [route(h, exact=False) for h in range(PEER_HEADS)])

    @pl.when(jnp.max(miscount) > 0.0)
    def _():
        for h in range(PEER_HEADS):
            route(h, exact=True)


def _peer_route(h1b, wq, sk, expert_u, expert_v, T):
    S, D = h1b.shape
    n_steps = S // T
    E = expert_u.shape[0]
    kernel = functools.partial(_route_kernel, T=T)
    shp = jax.ShapeDtypeStruct((PEER_HEADS, PEER_N_KEYS, S), F32)
    shp_b = jax.ShapeDtypeStruct((PEER_HEADS, PEER_N_KEYS, S), BF16)
    spec = pl.BlockSpec((PEER_HEADS, PEER_N_KEYS, T), lambda t: (0, 0, t))
    table = pl.BlockSpec((E // n_steps, D), lambda t: (t, 0))
    return pl.pallas_call(
        kernel,
        grid=(n_steps,),
        in_specs=[pl.BlockSpec((T, D), lambda t: (t, 0)),
                  _resident(wq.shape, lambda t: (0, 0)),
                  _resident(sk.shape, lambda t: (0, 0, 0)),
                  table, table],
        out_specs=[spec, spec, spec, spec, table, table],
        out_shape=[shp, shp, shp_b, shp_b,
                   jax.ShapeDtypeStruct(expert_u.shape, BF16), jax.ShapeDtypeStruct(expert_v.shape, BF16)],
        compiler_params=_params("parallel"),
        name="peer_route",
    )(h1b, wq, sk, expert_u, expert_v)


def _peer_kernel(ht_ref, u_ref, v_ref, l1_ref, a_ref, r2_ref, b_ref, o_ref, act_ref, *, n_i1, ce):
    @pl.when(pl.program_id(1) == 0)
    def _():
        o_ref[...] = jnp.zeros_like(o_ref)

    T = ht_ref.shape[1]
    n_slab = PEER_N_KEYS // PACKED_ROWS
    i1_per_chunk = ce // PEER_N_KEYS
    r2 = [r2_ref[h] for h in range(PEER_HEADS)]
    b = [b_ref[h] for h in range(PEER_HEADS)]

    def gates(c):
        tiles = []
        for j in range(c * i1_per_chunk, (c + 1) * i1_per_chunk):
            gate = [None] * n_slab
            for h in range(PEER_HEADS):
                l1 = jnp.broadcast_to(l1_ref[h, j:j + 1, :], (PACKED_ROWS, T)).astype(BF16)
                a = jnp.broadcast_to(a_ref[h, j:j + 1, :], (PACKED_ROWS, T)).astype(BF16)
                for s in range(n_slab):
                    slab = slice(s * PACKED_ROWS, (s + 1) * PACKED_ROWS)
                    val = jnp.where(r2[h][slab] < l1, b[h][slab], 0.0) * a
                    gate[s] = val if gate[s] is None else gate[s] + val
            tiles.extend(gate)
        return tiles

    n_chunk = n_i1 * PEER_N_KEYS // ce
    zero = jnp.minimum(pl.program_id(1), 0)
    for c in range(n_chunk):
        act_ref[c + zero] = jnp.dot(u_ref[c * ce:(c + 1) * ce, :], ht_ref[...],
                                    preferred_element_type=F32)
    tiles = []
    for c in range(n_chunk):
        gel = _gelu(act_ref[c + zero].astype(BF16))
        g = gates(c)
        tiles.extend(gel[k * PACKED_ROWS:(k + 1) * PACKED_ROWS] * g[k] for k in range(len(g)))
    pt = jnp.concatenate(tiles, axis=0)
    o_ref[...] += lax.dot_general(pt, v_ref[...], TN_DIMS, preferred_element_type=F32)


def _peer_experts(h1bt, u_b, v_b, l1, a, r2, b, T, eb):
    D, S = h1bt.shape
    E = u_b.shape[0]
    n_i1 = eb // PEER_N_KEYS
    ce = MXU_DIM // 2
    kernel = functools.partial(_peer_kernel, n_i1=n_i1, ce=ce)
    row_spec = pl.BlockSpec((PEER_HEADS, n_i1, T), lambda t, e: (0, e, t))
    full_spec = pl.BlockSpec((PEER_HEADS, PEER_N_KEYS, T), lambda t, e: (0, 0, t))
    return pl.pallas_call(
        kernel,
        grid=(S // T, E // eb),
        in_specs=[pl.BlockSpec((D, T), lambda t, e: (0, t)),
                  pl.BlockSpec((eb, D), lambda t, e: (e, 0)),
                  pl.BlockSpec((eb, D), lambda t, e: (e, 0)),
                  row_spec, row_spec, full_spec, full_spec],
        out_specs=pl.BlockSpec((T, D), lambda t, e: (t, 0)),
        out_shape=jax.ShapeDtypeStruct((S, D), F32),
        scratch_shapes=[pltpu.VMEM((eb // ce, ce, T), F32)],
        compiler_params=_params("parallel", "arbitrary"),
        name="peer_experts",
    )(h1bt, u_b, v_b, l1, a, r2, b)


def _ple_kernel(h1_ref, peer_ref, p_ref, wg_ref, wp_ref, g_ref, b_ref, o_ref):
    r = DEEPNORM_ALPHA * h1_ref[...] + peer_ref[...]
    gate = jax.nn.sigmoid(jnp.dot(r.astype(BF16), wg_ref[...], preferred_element_type=F32))
    proj = jnp.dot(p_ref[...], wp_ref[...], preferred_element_type=F32)
    o_ref[...] = _layer_norm(r + gate * proj, g_ref[...], b_ref[...])


def _ple(h1, peer, p_b, wg, wp, ln_g, ln_b, tm):
    S, D = h1.shape
    P = p_b.shape[1]
    return pl.pallas_call(
        _ple_kernel,
        grid=(S // tm,),
        in_specs=[pl.BlockSpec((tm, D), lambda i: (i, 0)),
                  pl.BlockSpec((tm, D), lambda i: (i, 0)),
                  pl.BlockSpec((tm, P), lambda i: (i, 0)),
                  _resident((D, D), lambda i: (0, 0)),
                  _resident((P, D), lambda i: (0, 0)),
                  pl.BlockSpec((1, D), lambda i: (0, 0)),
                  pl.BlockSpec((1, D), lambda i: (0, 0))],
        out_specs=pl.BlockSpec((tm, D), lambda i: (i, 0)),
        out_shape=jax.ShapeDtypeStruct((S, D), F32),
        compiler_params=_params("parallel"),
        name="ple_ln2",
    )(h1, peer, p_b, wg, wp, ln_g.reshape(1, D), ln_b.reshape(1, D))


def _layer(x, p, ln_emb_g, ln_emb_b, rel_bias, w_in, gmlp_ln_g, gmlp_ln_b, gmlp_w_s, gmlp_b_s,
           w_br_attn, w_br_gmlp, w_mix_out, ln1_g, ln1_b, peer_w_q, peer_sub_keys, peer_u, peer_v,
           ple_w_proj, ple_w_gate, ln2_g, ln2_b):
    S, D = x.shape
    tm = min(512, S)
    o_k, o_v, o_qi = ATTN_WIDTH, 2 * ATTN_WIDTH, 3 * ATTN_WIDTH
    o_ki = o_qi + N_IDX_HEADS * IDX_HEAD_DIM
    o_gu = o_ki + IDX_HEAD_DIM + N_IDX_HEADS

    h, hb = _ln_embed(x, ln_emb_g, ln_emb_b, tm)
    wqt = jnp.concatenate([w_in[:, :o_k], w_in[:, o_qi:o_ki]], axis=1).T.astype(BF16)
    w1 = w_in[:, o_k:o_qi].astype(BF16)
    w2 = jnp.pad(w_in[:, o_ki:o_gu], ((0, 0), (0, LANES - (o_gu - o_ki)))).astype(BF16)
    w3 = w_in[:, o_gu:].astype(BF16)
    qt = _matmul_t(wqt, hb, BF16, tm, "in_proj_q")
    kv = _matmul(hb, w1, BF16, min(1024, S), 512, "in_proj_kv")
    kw = _matmul(hb, w2, F32, min(1024, S), LANES, "in_proj_index")
    zg = _matmul(hb, w3, F32, min(1024, S), 512, "in_proj_gmlp")

    attn = _dsa_attention(qt, kv, kw[:, :IDX_HEAD_DIM].astype(BF16),
                          kw[:, IDX_HEAD_DIM:IDX_HEAD_DIM + N_IDX_HEADS].T, rel_bias, bq=256)
    gm = _sgu(zg, gmlp_ln_g, gmlp_ln_b, gmlp_w_s, gmlp_b_s, tm)
    h1, h1b, h1bt = _merge(attn, gm, zg, h, w_br_attn.astype(BF16), w_br_gmlp.astype(BF16),
                           w_mix_out.astype(BF16), ln1_g, ln1_b, min(256, S))

    sk = peer_sub_keys.reshape(PEER_HEADS * 2, PEER_N_KEYS, -1).astype(BF16)
    l1, a, r2, b, u_b, v_b = _peer_route(h1b, peer_w_q.astype(BF16), sk, peer_u, peer_v, min(256, S))
    peer = _peer_experts(h1bt, u_b, v_b, l1, a, r2, b, T=min(512, S), eb=1024)
    return _ple(h1, peer, p.astype(BF16), ple_w_gate.astype(BF16), ple_w_proj.astype(BF16),
                ln2_g, ln2_b, min(512, S))


def kernel(x, p, positions, ln_emb_g, ln_emb_b, rel_bias, w_in, gmlp_ln_g, gmlp_ln_b, gmlp_w_s, gmlp_b_s, w_br_attn, w_br_gmlp, w_mix_out, ln1_g, ln1_b, peer_w_q, peer_sub_keys, peer_u, peer_v, ple_w_proj, ple_w_gate, ln2_g, ln2_b):
    del positions
    assert x.shape[0] == 1 and w_in.shape[0] == DEPTH
    out = _layer(x[0], p[0, 0], ln_emb_g, ln_emb_b, rel_bias, w_in[0], gmlp_ln_g[0], gmlp_ln_b[0],
                 gmlp_w_s[0], gmlp_b_s[0], w_br_attn[0], w_br_gmlp[0], w_mix_out[0], ln1_g[0],
                 ln1_b[0], peer_w_q[0], peer_sub_keys[0], peer_u[0], peer_v[0], ple_w_proj[0],
                 ple_w_gate[0], ln2_g[0], ln2_b[0])
    return out[None]
```

```python
import functools
import math

import jax
import jax.numpy as jnp
from jax import lax
from jax.experimental import pallas as pl
from jax.experimental.pallas import tpu as pltpu

F32 = jnp.float32
BF16 = jnp.bfloat16
I32 = jnp.int32

LN_EPS = 1e-5
DEPTH = 1
DEEPNORM_ALPHA = (2.0 * DEPTH) ** 0.25

N_ATTN_HEADS = 8
ATTN_HEAD_DIM = 128
ATTN_WIDTH = N_ATTN_HEADS * ATTN_HEAD_DIM
N_IDX_HEADS = 16
IDX_HEAD_DIM = 64
TOPK_MAX = 256
GMLP_CHUNK = 128
GMLP_GROUPS = 8
GMLP_WIDTH = 1024
N_REL_BUCKETS = 32
REL_MAX_DISTANCE = 128
PEER_HEADS = 8
PEER_N_KEYS = 128
PEER_TOPK = 16

LANES = 128
SUBLANES = 8
PACKED_ROWS = 2 * SUBLANES
MXU_DIM = 256
NEG = -1e30
SEARCH_STEPS_MAX = 64
NT_DIMS = (((1,), (1,)), ((), ()))
TN_DIMS = (((0,), (0,)), ((), ()))
VMEM_LIMIT = 56 * 1024 * 1024


def _params(*sem):
    return pltpu.CompilerParams(dimension_semantics=sem, vmem_limit_bytes=VMEM_LIMIT)


def _resident(shape, index_map):
    return pl.BlockSpec(shape, index_map, pipeline_mode=pl.Buffered(1))


def _gelu(x):
    return 0.5 * x * (1.0 + jnp.tanh(0.7978845608028654 * (x + 0.044715 * (x * x * x))))


def _layer_norm(x, g, b):
    mu = jnp.mean(x, axis=-1, keepdims=True)
    xc = x - mu
    var = jnp.mean(xc * xc, axis=-1, keepdims=True)
    return xc * lax.rsqrt(var + LN_EPS) * g + b


def _ln_kernel(x_ref, g_ref, b_ref, h_ref, hb_ref):
    y = _layer_norm(x_ref[...], g_ref[...], b_ref[...])
    h_ref[...] = y
    hb_ref[...] = y.astype(BF16)


def _ln_embed(x, g, b, tm):
    S, D = x.shape
    return pl.pallas_call(
        _ln_kernel,
        grid=(S // tm,),
        in_specs=[pl.BlockSpec((tm, D), lambda i: (i, 0)),
                  pl.BlockSpec((1, D), lambda i: (0, 0)),
                  pl.BlockSpec((1, D), lambda i: (0, 0))],
        out_specs=[pl.BlockSpec((tm, D), lambda i: (i, 0)),
                   pl.BlockSpec((tm, D), lambda i: (i, 0))],
        out_shape=[jax.ShapeDtypeStruct((S, D), F32), jax.ShapeDtypeStruct((S, D), BF16)],
        compiler_params=_params("parallel"),
        name="ln_embed",
    )(x, g.reshape(1, D), b.reshape(1, D))


def _mm_kernel(a_ref, w_ref, o_ref):
    o_ref[...] = jnp.dot(a_ref[...], w_ref[...], preferred_element_type=F32).astype(o_ref.dtype)


def _matmul(a, w, out_dtype, tm, tn, name):
    M, K = a.shape
    N = w.shape[1]
    return pl.pallas_call(
        _mm_kernel,
        grid=(M // tm, N // tn),
        in_specs=[pl.BlockSpec((tm, K), lambda i, j: (i, 0)),
                  pl.BlockSpec((K, tn), lambda i, j: (0, j))],
        out_specs=pl.BlockSpec((tm, tn), lambda i, j: (i, j)),
        out_shape=jax.ShapeDtypeStruct((M, N), out_dtype),
        compiler_params=_params("parallel", "arbitrary"),
        name=name,
    )(a, w)


def _mm_t_kernel(wt_ref, a_ref, o_ref):
    o_ref[...] = lax.dot_general(wt_ref[...], a_ref[...], NT_DIMS,
                                 preferred_element_type=F32).astype(o_ref.dtype)


def _matmul_t(wt, a, out_dtype, tm, name):
    M, K = a.shape
    N = wt.shape[0]
    return pl.pallas_call(
        _mm_t_kernel,
        grid=(M // tm,),
        in_specs=[_resident((N, K), lambda i: (0, 0)),
                  pl.BlockSpec((tm, K), lambda i: (i, 0))],
        out_specs=pl.BlockSpec((N, tm), lambda i: (0, i)),
        out_shape=jax.ShapeDtypeStruct((N, M), out_dtype),
        compiler_params=_params("parallel"),
        name=name,
    )(wt, a)


def _reduce_row_groups(t, op):
    parts = [t[j * SUBLANES:(j + 1) * SUBLANES, :] for j in range(t.shape[0] // SUBLANES)]
    while len(parts) > 1:
        parts = [op(parts[j], parts[j + 1]) for j in range(0, len(parts) - 1, 2)] + (
            [parts[-1]] if len(parts) % 2 else [])
    return parts[0]


def _dsa_kernel(qit_ref, wt_ref, qt_ref, ki_ref, k_ref, v_ref, bias_ref, o_ref,
                sc_ref, m_ref, l_ref, acc_ref, *, bq, top_k):
    i = pl.program_id(0)
    w = wt_ref[...] * ((IDX_HEAD_DIM ** -0.5) * (N_IDX_HEADS ** -0.5))
    key_i = lax.broadcasted_iota(I32, (bq, bq), 0)
    qry_i = lax.broadcasted_iota(I32, (bq, bq), 1)
    causal = key_i <= qry_i

    def rows(c):
        return pl.ds(pl.multiple_of(c * bq, bq), bq)

    def scores(c):
        kic = ki_ref[rows(c), :]
        acc = jnp.zeros((bq, bq), F32)
        for h in range(N_IDX_HEADS):
            d = jnp.dot(kic, qit_ref[h * IDX_HEAD_DIM:(h + 1) * IDX_HEAD_DIM, :],
                        preferred_element_type=F32)
            acc = acc + w[h:h + 1, :] * jnp.maximum(d, 0.0)
        return acc

    def score_chunk(c, carry):
        sc_ref[c] = scores(c)
        return carry

    lax.fori_loop(0, i, score_chunk, 0)
    sc_ref[i] = jnp.where(causal, scores(i), -jnp.inf)

    def lowest_highest(c, lh):
        x = sc_ref[c]
        return (jnp.minimum(lh[0], _reduce_row_groups(x, jnp.minimum)),
                jnp.maximum(lh[1], _reduce_row_groups(x, jnp.maximum)))

    lo8, hi8 = lax.fori_loop(0, i, lowest_highest, (jnp.full((SUBLANES, bq), jnp.inf, F32),
                                                    jnp.full((SUBLANES, bq), -jnp.inf, F32)))
    diag = sc_ref[i]
    lo8 = jnp.minimum(lo8, _reduce_row_groups(jnp.where(causal, diag, jnp.inf), jnp.minimum))
    hi8 = jnp.maximum(hi8, _reduce_row_groups(diag, jnp.maximum))
    lo0 = jnp.min(lo8, axis=0, keepdims=True)
    hi0 = jnp.max(hi8, axis=0, keepdims=True)

    def count_ge(cand):
        def body(c, cnt):
            return cnt + _reduce_row_groups(jnp.where(sc_ref[c] >= cand, 1.0, 0.0), jnp.add)
        cnt = lax.fori_loop(0, i + 1, body, jnp.zeros((SUBLANES, bq), F32))
        return jnp.sum(cnt, axis=0, keepdims=True)

    n_keys = (i * bq + 1 + lax.broadcasted_iota(I32, (1, bq), 1)).astype(F32)
    few_keys = n_keys < float(top_k)

    def bisect(state):
        it, lo, hi, settled = state
        mid = lo + (hi - lo) * 0.5
        cnt = count_ge(mid)
        enough = cnt >= float(top_k)
        open_ = settled == 0.0
        lo_new = jnp.where(open_ & enough, mid, lo)
        hi_new = jnp.where(open_ & jnp.logical_not(enough), mid, hi)
        done = (cnt == float(top_k)) | (mid <= lo) | (mid >= hi)
        return it + 1, lo_new, hi_new, jnp.where(done, 1.0, settled)

    def unsettled(state):
        it, _, _, settled = state
        return (it < SEARCH_STEPS_MAX) & (jnp.min(settled) == 0.0)

    _, lo, _, _ = lax.while_loop(
        unsettled, lambda st: bisect(bisect(st)), (jnp.int32(0), lo0, hi0, jnp.where(few_keys, 1.0, 0.0)))
    thresh = jnp.where(few_keys, -jnp.inf, lo)

    def mask_chunk(c, carry):
        sc_ref[c] = jnp.where(sc_ref[c] >= thresh, 0.0, NEG)
        return carry

    lax.fori_loop(0, i, mask_chunk, 0)
    sc_ref[i] = jnp.where(causal, jnp.where(sc_ref[i] >= thresh, 0.0, NEG), NEG)

    scale = ATTN_HEAD_DIM ** -0.5
    m_ref[...] = jnp.full(m_ref.shape, NEG, F32)
    l_ref[...] = jnp.zeros(l_ref.shape, F32)
    acc_ref[...] = jnp.zeros(acc_ref.shape, F32)

    def attend(c, near):
        mbias = sc_ref[c]
        m_all, l_all = m_ref[...], l_ref[...]
        m_rows, l_rows = [], []

        def qk(h):
            cols = slice(h * ATTN_HEAD_DIM, (h + 1) * ATTN_HEAD_DIM)
            return jnp.dot(k_ref[rows(c), cols], qt_ref[cols, :], preferred_element_type=F32)

        s_raw = [qk(h) for h in range(N_ATTN_HEADS)]
        for h in range(N_ATTN_HEADS):
            cols = slice(h * ATTN_HEAD_DIM, (h + 1) * ATTN_HEAD_DIM)
            s = s_raw[h] * scale + mbias
            if near is not None:
                s = s + bias_ref[h, near * bq:(near + 1) * bq, :]
            m_old = m_all[h:h + 1, :]
            m_new = jnp.maximum(m_old, jnp.max(_reduce_row_groups(s, jnp.maximum), axis=0, keepdims=True))
            p = jnp.exp(s - m_new)
            a = jnp.exp(m_old - m_new)
            m_rows.append(m_new)
            l_rows.append(a * l_all[h:h + 1, :]
                          + jnp.sum(_reduce_row_groups(p, jnp.add), axis=0, keepdims=True))
            pv = lax.dot_general(v_ref[rows(c), cols], p.astype(BF16), TN_DIMS, preferred_element_type=F32)
            acc_ref[h] = a * acc_ref[h] + pv
        m_ref[...] = jnp.concatenate(m_rows, axis=0)
        l_ref[...] = jnp.concatenate(l_rows, axis=0)

    def far_chunk(c, carry):
        attend(c, None)
        return carry

    lax.fori_loop(0, jnp.maximum(i - 1, 0), far_chunk, 0)

    @pl.when(i > 0)
    def _():
        attend(i - 1, 0)

    attend(i, 1)
    for h in range(N_ATTN_HEADS):
        out_t = acc_ref[h] / l_ref[h:h + 1, :]
        o_ref[:, h * ATTN_HEAD_DIM:(h + 1) * ATTN_HEAD_DIM] = out_t.T.astype(o_ref.dtype)


def _rel_bucket(dist):
    n = jnp.maximum(dist, 0)
    max_exact = N_REL_BUCKETS // 2
    nf = jnp.maximum(n, 1).astype(F32)
    large = max_exact + (jnp.log(nf / max_exact) / math.log(REL_MAX_DISTANCE / max_exact)
                         * (N_REL_BUCKETS - max_exact)).astype(I32)
    large = jnp.minimum(large, N_REL_BUCKETS - 1)
    return jnp.where(n < max_exact, n, large)


def _near_bias_tiles(rel_bias, bq):
    assert bq >= REL_MAX_DISTANCE
    key = jnp.arange(2 * bq, dtype=I32)[:, None]
    qry = jnp.arange(bq, dtype=I32)[None, :]
    bucket = _rel_bucket(qry + bq - key)
    rel = (rel_bias - rel_bias[N_REL_BUCKETS - 1]).T
    tiles = jnp.broadcast_to(rel[:, 0][:, None, None], (rel.shape[0],) + bucket.shape)
    for b in range(1, N_REL_BUCKETS):
        tiles = jnp.where(bucket[None] >= b, rel[:, b][:, None, None], tiles)
    return tiles


def _dsa_attention(qt, kv, ki_b, wt, rel_bias, bq):
    S = kv.shape[0]
    top_k = min(TOPK_MAX, S // 4)
    n_blk = S // bq
    bias = _near_bias_tiles(rel_bias, bq)
    kernel = functools.partial(_dsa_kernel, bq=bq, top_k=top_k)
    return pl.pallas_call(
        kernel,
        grid=(n_blk,),
        in_specs=[pl.BlockSpec((N_IDX_HEADS * IDX_HEAD_DIM, bq), lambda i: (1, i)),
                  pl.BlockSpec((N_IDX_HEADS, bq), lambda i: (0, i)),
                  pl.BlockSpec((ATTN_WIDTH, bq), lambda i: (0, i)),
                  _resident((S, IDX_HEAD_DIM), lambda i: (0, 0)),
                  _resident((S, ATTN_WIDTH), lambda i: (0, 0)),
                  _resident((S, ATTN_WIDTH), lambda i: (0, 1)),
                  _resident((N_ATTN_HEADS, 2 * bq, bq), lambda i: (0, 0, 0))],
        out_specs=pl.BlockSpec((bq, ATTN_WIDTH), lambda i: (i, 0)),
        out_shape=jax.ShapeDtypeStruct((S, ATTN_WIDTH), BF16),
        scratch_shapes=[pltpu.VMEM((n_blk, bq, bq), F32),
                        pltpu.VMEM((N_ATTN_HEADS, bq), F32),
                        pltpu.VMEM((N_ATTN_HEADS, bq), F32),
                        pltpu.VMEM((N_ATTN_HEADS, ATTN_HEAD_DIM, bq), F32)],
        compiler_params=_params("arbitrary"),
        name="dsa_attention",
    )(qt, wt, qt, ki_b, kv, kv, bias)


def _sgu_kernel(gu_ref, gv_ref, g_ref, b_ref, ws_ref, bs_ref, o_ref, *, n_chunk):
    u = _gelu(gu_ref[...])
    v = _layer_norm(_gelu(gv_ref[...]), g_ref[...], b_ref[...]).astype(BF16)
    r = lax.broadcasted_iota(I32, (GMLP_CHUNK, GMLP_CHUNK), 0)
    c = lax.broadcasted_iota(I32, (GMLP_CHUNK, GMLP_CHUNK), 1)
    for g in range(GMLP_GROUPS):
        wg = jnp.where(c <= r, ws_ref[g], 0.0).astype(BF16)
        bcol = bs_ref[:, g:g + 1]
        cols = slice(g * LANES, (g + 1) * LANES)
        for n in range(n_chunk):
            rws = slice(n * GMLP_CHUNK, (n + 1) * GMLP_CHUNK)
            mixed = jnp.dot(wg, v[rws, cols], preferred_element_type=F32) + bcol
            o_ref[rws, cols] = (u[rws, cols] * mixed).astype(o_ref.dtype)


def _sgu(zg, ln_g, ln_b, w_s, b_s, tm):
    S = zg.shape[0]
    kernel = functools.partial(_sgu_kernel, n_chunk=tm // GMLP_CHUNK)
    return pl.pallas_call(
        kernel,
        grid=(S // tm,),
        in_specs=[pl.BlockSpec((tm, GMLP_WIDTH), lambda i: (i, 0)),
                  pl.BlockSpec((tm, GMLP_WIDTH), lambda i: (i, 1)),
                  pl.BlockSpec((1, GMLP_WIDTH), lambda i: (0, 0)),
                  pl.BlockSpec((1, GMLP_WIDTH), lambda i: (0, 0)),
                  pl.BlockSpec((GMLP_GROUPS, GMLP_CHUNK, GMLP_CHUNK), lambda i: (0, 0, 0)),
                  pl.BlockSpec((GMLP_CHUNK, GMLP_GROUPS), lambda i: (0, 0))],
        out_specs=pl.BlockSpec((tm, GMLP_WIDTH), lambda i: (i, 0)),
        out_shape=jax.ShapeDtypeStruct((S, GMLP_WIDTH), BF16),
        compiler_params=_params("parallel"),
        name="gmlp_sgu",
    )(zg, zg, ln_g.reshape(1, -1), ln_b.reshape(1, -1), w_s, b_s.T)


def _merge_kernel(at_ref, gu_ref, gv_ref, sg_ref, sb_ref, ws_ref, bs_ref, ga_ref, gg_ref, h_ref,
                  wa_ref, wg_ref, wo_ref, g_ref, b_ref, h1_ref, h1b_ref, h1bt_ref, gm_ref):
    _sgu_kernel(gu_ref, gv_ref, sg_ref, sb_ref, ws_ref, bs_ref, gm_ref,
                n_chunk=gm_ref.shape[0] // GMLP_CHUNK)
    a1 = jnp.dot(at_ref[...], wa_ref[...], preferred_element_type=F32)
    a2 = jnp.dot(gm_ref[...], wg_ref[...], preferred_element_type=F32)
    merged = jax.nn.sigmoid(ga_ref[...]) * a1 + jax.nn.sigmoid(gg_ref[...]) * a2
    y = DEEPNORM_ALPHA * h_ref[...] + jnp.dot(merged.astype(BF16), wo_ref[...], preferred_element_type=F32)
    h1 = _layer_norm(y, g_ref[...], b_ref[...])
    h1_ref[...] = h1
    h1b_ref[...] = h1.astype(BF16)
    h1bt_ref[...] = h1.T.astype(BF16)


def _merge(attn, zg, sgu_ln_g, sgu_ln_b, w_s, b_s, h, wa, wg, wo, ln_g, ln_b, tm):
    S, D = h.shape
    W = attn.shape[1]
    return pl.pallas_call(
        _merge_kernel,
        grid=(S // tm,),
        in_specs=[pl.BlockSpec((tm, W), lambda i: (i, 0)),
                  pl.BlockSpec((tm, GMLP_WIDTH), lambda i: (i, 0)),
                  pl.BlockSpec((tm, GMLP_WIDTH), lambda i: (i, 1)),
                  pl.BlockSpec((1, GMLP_WIDTH), lambda i: (0, 0)),
                  pl.BlockSpec((1, GMLP_WIDTH), lambda i: (0, 0)),
                  pl.BlockSpec((GMLP_GROUPS, GMLP_CHUNK, GMLP_CHUNK), lambda i: (0, 0, 0)),
                  pl.BlockSpec((GMLP_CHUNK, GMLP_GROUPS), lambda i: (0, 0)),
                  pl.BlockSpec((tm, D), lambda i: (i, 1)),
                  pl.BlockSpec((tm, D), lambda i: (i, 2)),
                  pl.BlockSpec((tm, D), lambda i: (i, 0)),
                  _resident((W, D), lambda i: (0, 0)),
                  _resident((W, D), lambda i: (0, 0)),
                  _resident((D, D), lambda i: (0, 0)),
                  pl.BlockSpec((1, D), lambda i: (0, 0)),
                  pl.BlockSpec((1, D), lambda i: (0, 0))],
        out_specs=[pl.BlockSpec((tm, D), lambda i: (i, 0)),
                   pl.BlockSpec((tm, D), lambda i: (i, 0)),
                   pl.BlockSpec((D, tm), lambda i: (0, i))],
        out_shape=[jax.ShapeDtypeStruct((S, D), F32), jax.ShapeDtypeStruct((S, D), BF16),
                   jax.ShapeDtypeStruct((D, S), BF16)],
        scratch_shapes=[pltpu.VMEM((tm, GMLP_WIDTH), BF16)],
        compiler_params=_params("parallel"),
        name="merge_ln1",
    )(attn, zg, zg, sgu_ln_g.reshape(1, -1), sgu_ln_b.reshape(1, -1), w_s, b_s.T, zg, zg, h, wa, wg, wo,
      ln_g.reshape(1, D), ln_b.reshape(1, D))


def _peer_pairs():
    pairs = [(a, b) for a in range(PEER_TOPK) for b in range(PEER_TOPK) if (a + 1) * (b + 1) <= PEER_TOPK]
    return sorted(pairs, key=lambda ab: ab[0] * PEER_TOPK + ab[1])


def _extract_top(work, n_rounds, on_pick, exact):
    n = float(work.shape[0])
    iota = lax.broadcasted_iota(I32, work.shape, 0).astype(F32)
    for r in range(n_rounds):
        mx = jnp.max(work, axis=0, keepdims=True)
        pick = work == mx
        if exact:
            first = jnp.min(jnp.where(pick, iota, n), axis=0, keepdims=True)
            pick = iota == first
        on_pick(r, mx, pick)
        work = jnp.where(pick, -jnp.inf, work)


def _route_kernel(hb_ref, wq_ref, sk_ref, u_ref, v_ref, l1_ref, a_ref, r2_ref, b_ref, ub_ref, vb_ref, *, T):
    ub_ref[...] = u_ref[...].astype(BF16)
    vb_ref[...] = v_ref[...].astype(BF16)
    qt = jnp.dot(hb_ref[...], wq_ref[...], preferred_element_type=F32).T
    pairs = _peer_pairs()

    def route(h, exact):
        miscount = jnp.zeros((1, T), F32)
        s, rank, vals = [], [], []
        for p in range(2):
            idx = 2 * h + p
            sp = jnp.dot(sk_ref[idx], qt[idx * LANES:(idx + 1) * LANES, :].astype(BF16),
                         preferred_element_type=F32)
            state = {"rank": jnp.full(sp.shape, float(PEER_TOPK), F32), "vals": []}

            def on_pick(r, mx, pick, state=state):
                state["rank"] = jnp.where(pick, float(r), state["rank"])
                state["vals"].append(mx)

            _extract_top(sp, PEER_TOPK, on_pick, exact)
            n_ranked = jnp.sum(jnp.where(state["rank"] < float(PEER_TOPK), 1.0, 0.0), axis=0, keepdims=True)
            miscount = jnp.maximum(miscount, jnp.abs(n_ranked - float(PEER_TOPK)))
            s.append(sp)
            rank.append(state["rank"])
            vals.append(state["vals"])
        cand = jnp.concatenate([vals[0][a] + vals[1][b] for a, b in pairs], axis=0)
        picked = {"mask": jnp.zeros(cand.shape, jnp.bool_)}

        def on_pick_c(r, mx, pick, picked=picked):
            picked["mask"] = picked["mask"] | pick

        _extract_top(cand, PEER_TOPK, on_pick_c, True)
        top = vals[0][0] + vals[1][0]
        selw = jnp.where(picked["mask"], jnp.exp(cand - top), 0.0)
        z = jnp.sum(selw, axis=0, keepdims=True)
        self32 = jnp.where(picked["mask"], 1.0, 0.0)
        l1 = jnp.zeros(s[0].shape, F32)
        pos = 0
        for a in range(PEER_TOPK):
            cnt = sum(1 for ab in pairs if ab[0] == a)
            n_a = jnp.sum(self32[pos:pos + cnt, :], axis=0, keepdims=True)
            pos += cnt
            l1 = jnp.where(rank[0] == float(a), n_a, l1)
        l1_ref[h] = l1
        a_ref[h] = jnp.exp(s[0] - vals[0][0])
        r2_ref[h] = rank[1].astype(r2_ref.dtype)
        b_ref[h] = (jnp.exp(s[1] - vals[1][0]) / z).astype(b_ref.dtype)
        return miscount

    miscount = functools.reduce(jnp.maximum, [route(h, exact=False) for h in range(PEER_HEADS)])

    @pl.when(jnp.max(miscount) > 0.0)
    def _():
        for h in range(PEER_HEADS):
            route(h, exact=True)


def _peer_route(h1b, wq, sk, expert_u, expert_v, T):
    S, D = h1b.shape
    n_steps = S // T
    E = expert_u.shape[0]
    kernel = functools.partial(_route_kernel, T=T)
    shp = jax.ShapeDtypeStruct((PEER_HEADS, PEER_N_KEYS, S), F32)
    shp_b = jax.ShapeDtypeStruct((PEER_HEADS, PEER_N_KEYS, S), BF16)
    spec = pl.BlockSpec((PEER_HEADS, PEER_N_KEYS, T), lambda t: (0, 0, t))
    table = pl.BlockSpec((E // n_steps, D), lambda t: (t, 0))
    return pl.pallas_call(
        kernel,
        grid=(n_steps,),
        in_specs=[pl.BlockSpec((T, D), lambda t: (t, 0)),
                  _resident(wq.shape, lambda t: (0, 0)),
                  _resident(sk.shape, lambda t: (0, 0, 0)),
                  table, table],
        out_specs=[spec, spec, spec, spec, table, table],
        out_shape=[shp, shp, shp_b, shp_b,
                   jax.ShapeDtypeStruct(expert_u.shape, BF16), jax.ShapeDtypeStruct(expert_v.shape, BF16)],
        compiler_params=_params("parallel"),
        name="peer_route",
    )(h1b, wq, sk, expert_u, expert_v)


def _peer_kernel(ht_ref, u_ref, v_ref, l1_ref, a_ref, r2_ref, b_ref, o_ref, act_ref, *, n_i1, ce):
    @pl.when(pl.program_id(1) == 0)
    def _():
        o_ref[...] = jnp.zeros_like(o_ref)

    T = ht_ref.shape[1]
    n_slab = PEER_N_KEYS // PACKED_ROWS
    i1_per_chunk = ce // PEER_N_KEYS
    r2 = [r2_ref[h] for h in range(PEER_HEADS)]
    b = [b_ref[h] for h in range(PEER_HEADS)]

    def gates(c):
        tiles = []
        for j in range(c * i1_per_chunk, (c + 1) * i1_per_chunk):
            gate = [None] * n_slab
            for h in range(PEER_HEADS):
                l1 = jnp.broadcast_to(l1_ref[h, j:j + 1, :], (PACKED_ROWS, T)).astype(BF16)
                a = jnp.broadcast_to(a_ref[h, j:j + 1, :], (PACKED_ROWS, T)).astype(BF16)
                for s in range(n_slab):
                    slab = slice(s * PACKED_ROWS, (s + 1) * PACKED_ROWS)
                    val = jnp.where(r2[h][slab] < l1, b[h][slab], 0.0) * a
                    gate[s] = val if gate[s] is None else gate[s] + val
            tiles.extend(gate)
        return tiles

    n_chunk = n_i1 * PEER_N_KEYS // ce
    zero = jnp.minimum(pl.program_id(1), 0)
    for c in range(n_chunk):
        act_ref[c + zero] = jnp.dot(u_ref[c * ce:(c + 1) * ce, :], ht_ref[...],
                                    preferred_element_type=F32)
    tiles = []
    for c in range(n_chunk):
        gel = _gelu(act_ref[c + zero].astype(BF16))
        g = gates(c)
        tiles.extend(gel[k * PACKED_ROWS:(k + 1) * PACKED_ROWS] * g[k] for k in range(len(g)))
    pt = jnp.concatenate(tiles, axis=0)
    o_ref[...] += lax.dot_general(pt, v_ref[...], TN_DIMS, preferred_element_type=F32)


def _peer_experts(h1bt, u_b, v_b, l1, a, r2, b, T, eb):
    D, S = h1bt.shape
    E = u_b.shape[0]
    n_i1 = eb // PEER_N_KEYS
    ce = MXU_DIM // 2
    kernel = functools.partial(_peer_kernel, n_i1=n_i1, ce=ce)
    row_spec = pl.BlockSpec((PEER_HEADS, n_i1, T), lambda t, e: (0, e, t))
    full_spec = pl.BlockSpec((PEER_HEADS, PEER_N_KEYS, T), lambda t, e: (0, 0, t))
    return pl.pallas_call(
        kernel,
        grid=(S // T, E // eb),
        in_specs=[pl.BlockSpec((D, T), lambda t, e: (0, t)),
                  pl.BlockSpec((eb, D), lambda t, e: (e, 0)),
                  pl.BlockSpec((eb, D), lambda t, e: (e, 0)),
                  row_spec, row_spec, full_spec, full_spec],
        out_specs=pl.BlockSpec((T, D), lambda t, e: (t, 0)),
        out_shape=jax.ShapeDtypeStruct((S, D), F32),
        scratch_shapes=[pltpu.VMEM((eb // ce, ce, T), F32)],
        compiler_params=_params("parallel", "arbitrary"),
        name="peer_experts",
    )(h1bt, u_b, v_b, l1, a, r2, b)


def _ple_kernel(h1_ref, peer_ref, p_ref, wg_ref, wp_ref, g_ref, b_ref, o_ref):
    r = DEEPNORM_ALPHA * h1_ref[...] + peer_ref[...]
    gate = jax.nn.sigmoid(jnp.dot(r.astype(BF16), wg_ref[...], preferred_element_type=F32))
    proj = jnp.dot(p_ref[...], wp_ref[...], preferred_element_type=F32)
    o_ref[...] = _layer_norm(r + gate * proj, g_ref[...], b_ref[...])


def _ple(h1, peer, p_b, wg, wp, ln_g, ln_b, tm):
    S, D = h1.shape
    P = p_b.shape[1]
    return pl.pallas_call(
        _ple_kernel,
        grid=(S // tm,),
        in_specs=[pl.BlockSpec((tm, D), lambda i: (i, 0)),
                  pl.BlockSpec((tm, D), lambda i: (i, 0)),
                  pl.BlockSpec((tm, P), lambda i: (i, 0)),
                  _resident((D, D), lambda i: (0, 0)),
                  _resident((P, D), lambda i: (0, 0)),
                  pl.BlockSpec((1, D), lambda i: (0, 0)),
                  pl.BlockSpec((1, D), lambda i: (0, 0))],
        out_specs=pl.BlockSpec((tm, D), lambda i: (i, 0)),
        out_shape=jax.ShapeDtypeStruct((S, D), F32),
        compiler_params=_params("parallel"),
        name="ple_ln2",
    )(h1, peer, p_b, wg, wp, ln_g.reshape(1, D), ln_b.reshape(1, D))


def _layer(x, p, ln_emb_g, ln_emb_b, rel_bias, w_in, gmlp_ln_g, gmlp_ln_b, gmlp_w_s, gmlp_b_s,
           w_br_attn, w_br_gmlp, w_mix_out, ln1_g, ln1_b, peer_w_q, peer_sub_keys, peer_u, peer_v,
           ple_w_proj, ple_w_gate, ln2_g, ln2_b):
    S, D = x.shape
    tm = min(512, S)
    o_k, o_v, o_qi = ATTN_WIDTH, 2 * ATTN_WIDTH, 3 * ATTN_WIDTH
    o_ki = o_qi + N_IDX_HEADS * IDX_HEAD_DIM
    o_gu = o_ki + IDX_HEAD_DIM + N_IDX_HEADS

    h, hb = _ln_embed(x, ln_emb_g, ln_emb_b, tm)
    wqt = jnp.concatenate([w_in[:, :o_k], w_in[:, o_qi:o_ki]], axis=1).T.astype(BF16)
    w1 = w_in[:, o_k:o_qi].astype(BF16)
    w2 = jnp.pad(w_in[:, o_ki:o_gu], ((0, 0), (0, LANES - (o_gu - o_ki)))).astype(BF16)
    w3 = w_in[:, o_gu:].astype(BF16)
    qt = _matmul_t(wqt, hb, BF16, tm, "in_proj_q")
    kv = _matmul(hb, w1, BF16, min(1024, S), 512, "in_proj_kv")
    kw = _matmul(hb, w2, F32, min(1024, S), LANES, "in_proj_index")
    zg = _matmul(hb, w3, F32, min(1024, S), 512, "in_proj_gmlp")

    attn = _dsa_attention(qt, kv, kw[:, :IDX_HEAD_DIM].astype(BF16),
                          kw[:, IDX_HEAD_DIM:IDX_HEAD_DIM + N_IDX_HEADS].T, rel_bias, bq=256)
    h1, h1b, h1bt = _merge(attn, zg, gmlp_ln_g, gmlp_ln_b, gmlp_w_s, gmlp_b_s, h,
                           w_br_attn.astype(BF16), w_br_gmlp.astype(BF16),
                           w_mix_out.astype(BF16), ln1_g, ln1_b, min(256, S))

    sk = peer_sub_keys.reshape(PEER_HEADS * 2, PEER_N_KEYS, -1).astype(BF16)
    l1, a, r2, b, u_b, v_b = _peer_route(h1b, peer_w_q.astype(BF16), sk, peer_u, peer_v, min(256, S))
    peer = _peer_experts(h1bt, u_b, v_b, l1, a, r2, b, T=min(512, S), eb=1024)
    return _ple(h1, peer, p.astype(BF16), ple_w_gate.astype(BF16), ple_w_proj.astype(BF16),
                ln2_g, ln2_b, min(512, S))


def kernel(x, p, positions, ln_emb_g, ln_emb_b, rel_bias, w_in, gmlp_ln_g, gmlp_ln_b, gmlp_w_s, gmlp_b_s, w_br_attn, w_br_gmlp, w_mix_out, ln1_g, ln1_b, peer_w_q, peer_sub_keys, peer_u, peer_v, ple_w_proj, ple_w_gate, ln2_g, ln2_b):
    del positions
    assert x.shape[0] == 1 and w_in.shape[0] == DEPTH
    out = _layer(x[0], p[0, 0], ln_emb_g, ln_emb_b, rel_bias, w_in[0], gmlp_ln_g[0], gmlp_ln_b[0],
                 gmlp_w_s[0], gmlp_b_s[0], w_br_attn[0], w_br_gmlp[0], w_mix_out[0], ln1_g[0],
                 ln1_b[0], peer_w_q[0], peer_sub_keys[0], peer_u[0], peer_v[0], ple_w_proj[0],
                 ple_w_gate[0], ln2_g[0], ln2_b[0])
    return out[None]
```
